```python
import math
import jax, jax.numpy as jnp
from jax import lax
import numpy as np

D_MODEL = 1024
BATCH = 8
SEQ = 2048
DEPTH = 4
DEC_BATCH = 128
DEC_SEQ = 1
PAST_LEN = 16384
PAGE_SIZE = 128

N_EVEN = (DEPTH + 1) // 2
N_ODD = DEPTH // 2
RWKV_HEADS = 8
RWKV_HEAD_DIM = 64
RWKV_DIM = RWKV_HEADS * RWKV_HEAD_DIM
W_LORA = 64
A_LORA = 64
V_LORA = 32
G_LORA = 128
OFF_W = 3 * RWKV_DIM
OFF_A = OFF_W + W_LORA
OFF_G = OFF_A + A_LORA
RWKV_COLS = OFF_G + G_LORA
LNX_EPS = 64e-5
LRU_DIM = D_MODEL - RWKV_DIM
LRU_BLOCKS = 8
LRU_BLOCK = LRU_DIM // LRU_BLOCKS
CONV_W = 4
LRU_C = 8.0
IN_COLS = RWKV_COLS + 2 * LRU_DIM
MIX_DIM = RWKV_DIM + LRU_DIM
S5_GROUP = 16
S5_GROUPS = D_MODEL // S5_GROUP
S5_STATE = 64
N_EXPERTS = 16
N_EXPERT_GROUPS = 4
EXPERTS_PER_GROUP = N_EXPERTS // N_EXPERT_GROUPS
TOP_K = 2
D_EXPERT = 256
RMS_EPS = 1e-6

kernel_name = 'hybrid_rwkv7_rglru_s5_moe_step'


def _rmsnorm(x, g):
    x32 = x.astype(jnp.float32)
    y = x32 * lax.rsqrt(jnp.mean(x32 * x32, axis=-1, keepdims=True) + RMS_EPS)
    return (y * g.astype(jnp.float32)).astype(x.dtype)


def _pre(x, c, g, w, b):
    m = (jax.nn.silu(c) @ w + b)[:, None, :]
    shift, scale, gate = m[..., :D_MODEL], m[..., D_MODEL:2 * D_MODEL], m[..., 2 * D_MODEL:]
    h = (_rmsnorm(x, g) * (1.0 + scale) + shift).astype(x.dtype)
    return h, gate


def _linear_scan(a, b, h0):
    def comb(l, r):
        return (l[0] * r[0], r[0] * l[1] + r[1])
    a_cum, b_cum = lax.associative_scan(comb, (a, b), axis=1)
    return a_cum * h0[:, None] + b_cum


def _wkv7(r, w, k, v, a, b, s0):
    def step(s, inp):
        r_t, w_t, k_t, v_t, a_t, b_t = inp
        sa = jnp.einsum('bhij,bhj->bhi', s, a_t)
        s = s * w_t[:, :, None, :] + sa[..., None] * b_t[:, :, None, :] + v_t[..., None] * k_t[:, :, None, :]
        return s, jnp.einsum('bhij,bhj->bhi', s, r_t)
    xs = tuple(jnp.swapaxes(t, 0, 1) for t in (r, w, k, v, a, b))
    s_last, ys = lax.scan(step, s0, xs)
    return jnp.swapaxes(ys, 0, 1), s_last


def _even_mixer(h, s_wkv, s_shift, s_lru, s_conv, v_first, v_lora, w_in, mu, w0, w2, a0, a2, g2,
                k_k, k_a, r_k, lnx_w, lnx_b, conv_w, conv_b, lru_wa, lru_ba, lru_wx, lru_bx, lru_lam, w_out):
    f32 = jnp.float32
    Bn, Tn, _ = h.shape
    proj = (h @ w_in).astype(f32)
    pr = proj[..., :RWKV_COLS]
    prev = jnp.concatenate([s_shift[:, None].astype(f32), pr[:, :-1]], axis=1)
    ps = pr + (prev - pr) * mu.astype(f32)
    r = ps[..., :RWKV_DIM]
    k = ps[..., RWKV_DIM:2 * RWKV_DIM]
    v = ps[..., 2 * RWKV_DIM:OFF_W]
    wd = ps[..., OFF_W:OFF_A]
    ad = ps[..., OFF_A:OFF_G]
    gd = ps[..., OFF_G:RWKV_COLS]
    w_log = -jax.nn.softplus(-(w0 + jnp.tanh(wd) @ w2)) - 0.5
    decay = jnp.exp(-jnp.exp(w_log))
    if v_lora is None:
        v_first = v
    else:
        v0, v1, v2 = v_lora
        v = v + (v_first - v) * jax.nn.sigmoid(v0 + (v @ v1) @ v2)
    a = jax.nn.sigmoid(a0 + ad @ a2)
    g = jax.nn.sigmoid(gd) @ g2
    heads = lambda t: t.reshape(Bn, Tn, RWKV_HEADS, RWKV_HEAD_DIM)
    kk = heads(k * k_k)
    kk = kk * lax.rsqrt(jnp.maximum(jnp.sum(kk * kk, axis=-1, keepdims=True), 1e-24))
    k = k * (1.0 + (a - 1.0) * k_a)
    rh, kh, vh, ah = heads(r), heads(k), heads(v), heads(a)
    y, s_wkv_new = _wkv7(rh, heads(decay), kh, vh, -kk, kk * ah, s_wkv.astype(f32))
    mean = jnp.mean(y, axis=-1, keepdims=True)
    var = jnp.mean(jnp.square(y - mean), axis=-1, keepdims=True)
    y = (y - mean) * lax.rsqrt(var + LNX_EPS) * heads(lnx_w) [0, 0] + heads(lnx_b)[0, 0] if False else (y - mean) * lax.rsqrt(var + LNX_EPS) * lnx_w.reshape(RWKV_HEADS, RWKV_HEAD_DIM) + lnx_b.reshape(RWKV_HEADS, RWKV_HEAD_DIM)
    y = y + jnp.sum(rh * kh * r_k, axis=-1, keepdims=True) * vh
    y_rwkv = y.reshape(Bn, Tn, RWKV_DIM) * g
    lx = proj[..., RWKV_COLS:RWKV_COLS + LRU_DIM]
    lg = proj[..., RWKV_COLS + LRU_DIM:]
    xpad = jnp.concatenate([s_conv.astype(f32), lx], axis=1)
    xc = conv_b.astype(f32)
    for j in range(CONV_W):
        xc = xc + xpad[:, j:j + Tn] * conv_w[j]
    new_conv = xpad[:, Tn:]
    xb = xc.reshape(Bn, Tn, LRU_BLOCKS, LRU_BLOCK)
    gate_r = jax.nn.sigmoid(jnp.einsum('btnc,ncd->btnd', xb, lru_wa) + lru_ba)
    gate_i = jax.nn.sigmoid(jnp.einsum('btnc,ncd->btnd', xb, lru_wx) + lru_bx)
    log_a = -LRU_C * gate_r * jax.nn.softplus(-lru_lam.astype(f32))
    a_t = jnp.exp(log_a).reshape(Bn, Tn, LRU_DIM)
    b_t = (jnp.sqrt(-jnp.expm1(2.0 * log_a)) * gate_i * xb).reshape(Bn, Tn, LRU_DIM)
    hs = _linear_scan(a_t, b_t, s_lru.astype(f32))
    y_lru = hs * jax.nn.gelu(lg)
    out = jnp.concatenate([y_rwkv, y_lru], axis=-1).astype(h.dtype) @ w_out
    return out, v_first, s_wkv_new, pr[:, -1], hs[:, -1], new_conv


def _odd_mixer(h, s_re, s_im, a_re, a_im, log_dt, b_re, b_im, c_re, c_im, d_skip, w_glu, b_glu):
    f32 = jnp.float32
    Bn, Tn, _ = h.shape
    a_re, a_im = a_re.astype(f32), a_im.astype(f32)
    b_re, b_im = b_re.astype(f32), b_im.astype(f32)
    u = h.astype(f32).reshape(Bn, Tn, S5_GROUPS, S5_GROUP)
    dt = jnp.exp(log_dt.astype(f32))[:, None]
    mag = jnp.exp(a_re * dt)
    abr, abi = mag * jnp.cos(a_im * dt), mag * jnp.sin(a_im * dt)
    den = a_re * a_re + a_im * a_im
    nr = abr - 1.0
    cr = (nr * a_re + abi * a_im) / den
    ci = (abi * a_re - nr * a_im) / den
    bbr = cr[..., None] * b_re - ci[..., None] * b_im
    bbi = cr[..., None] * b_im + ci[..., None] * b_re
    bur = jnp.einsum('btgc,gpc->btgp', u, bbr)
    bui = jnp.einsum('btgc,gpc->btgp', u, bbi)
    ar = jnp.broadcast_to(abr, (1, Tn) + abr.shape)
    ai = jnp.broadcast_to(abi, (1, Tn) + abi.shape)
    def comb(l, r):
        lar, lai, lbr, lbi = l
        rar, rai, rbr, rbi = r
        return (lar * rar - lai * rai, lar * rai + lai * rar,
                rar * lbr - rai * lbi + rbr, rar * lbi + rai * lbr + rbi)
    pcr, pci, qr, qi = lax.associative_scan(comb, (ar, ai, bur, bui), axis=1)
    h0r, h0i = s_re.astype(f32)[:, None], s_im.astype(f32)[:, None]
    xr = pcr * h0r - pci * h0i + qr
    xi = pcr * h0i + pci * h0r + qi
    y = (jnp.einsum('btgp,gcp->btgc', xr, c_re) - jnp.einsum('btgp,gcp->btgc', xi, c_im)
         + d_skip.astype(f32).reshape(S5_GROUPS, S5_GROUP) * u)
    z = jax.nn.gelu(y.reshape(Bn, Tn, D_MODEL)).astype(h.dtype)
    gl = z @ w_glu + b_glu
    out = gl[..., :D_MODEL] * jax.nn.sigmoid(gl[..., D_MODEL:])
    return out, xr[:, -1], xi[:, -1]


def _moe(h, router_w, router_b, w_gate, w_up, w_down):
    Bn, Tn, _ = h.shape
    x = h.reshape(Bn * Tn, D_MODEL)
    s = jax.nn.sigmoid((x @ router_w).astype(jnp.float32))
    sel = s + router_b.astype(jnp.float32)
    grp_score = lax.top_k(sel.reshape(-1, N_EXPERT_GROUPS, EXPERTS_PER_GROUP), TOP_K)[0].sum(-1)
    grp = jnp.argmax(grp_score, axis=-1)
    in_grp = jnp.repeat(jax.nn.one_hot(grp, N_EXPERT_GROUPS, dtype=jnp.float32), EXPERTS_PER_GROUP, axis=-1) > 0
    _, idx = lax.top_k(jnp.where(in_grp, sel, -jnp.inf), TOP_K)
    wts = jnp.take_along_axis(s, idx, axis=-1)
    wts = wts / jnp.sum(wts, axis=-1, keepdims=True)
    gates = jnp.sum(jax.nn.one_hot(idx, N_EXPERTS, dtype=jnp.float32) * wts[..., None], axis=1)
    hg = jnp.einsum('nd,edf->nef', x, w_gate)
    hu = jnp.einsum('nd,edf->nef', x, w_up)
    act = jax.nn.silu(hg) * hu * gates[..., None].astype(x.dtype)
    return jnp.einsum('nef,efd->nd', act, w_down).reshape(Bn, Tn, D_MODEL)


def _trunk(x, c, s_wkv, s_shift, s_lru, s_conv, s_s5r, s_s5i, p):
    (norm_g, ada_w, ada_b, final_norm_g, even_w_in, rwkv_mu, rwkv_w0, rwkv_w2, rwkv_a0, rwkv_a2,
     rwkv_g2, rwkv_k_k, rwkv_k_a, rwkv_r_k, rwkv_lnx_w, rwkv_lnx_b, rwkv_v0, rwkv_v1, rwkv_v2,
     lru_conv_w, lru_conv_b, lru_wa, lru_ba, lru_wx, lru_bx, lru_lam, even_w_out,
     s5_a_re, s5_a_im, s5_log_dt, s5_b_re, s5_b_im, s5_c_re, s5_c_im, s5_d, s5_w_glu, s5_b_glu,
     router_w, router_b, moe_w_gate, moe_w_up, moe_w_down) = p
    n_wkv, n_shift, n_lru, n_conv, n_s5r, n_s5i = [], [], [], [], [], []
    v_first = None
    for layer in range(DEPTH):
        e = layer // 2
        h, gate = _pre(x, c, norm_g[layer, 0], ada_w[layer, 0], ada_b[layer, 0])
        if layer % 2 == 0:
            v_lora = None if e == 0 else (rwkv_v0[e - 1], rwkv_v1[e - 1], rwkv_v2[e - 1])
            out, v_first, st_w, st_sh, st_h, st_c = _even_mixer(
                h, s_wkv[e], s_shift[e], s_lru[e], s_conv[e], v_first, v_lora, even_w_in[e], rwkv_mu[e],
                rwkv_w0[e], rwkv_w2[e], rwkv_a0[e], rwkv_a2[e], rwkv_g2[e], rwkv_k_k[e], rwkv_k_a[e],
                rwkv_r_k[e], rwkv_lnx_w[e], rwkv_lnx_b[e], lru_conv_w[e], lru_conv_b[e], lru_wa[e],
                lru_ba[e], lru_wx[e], lru_bx[e], lru_lam[e], even_w_out[e])
            n_wkv.append(st_w); n_shift.append(st_sh); n_lru.append(st_h); n_conv.append(st_c)
        else:
            out, st_r, st_i = _odd_mixer(h, s_s5r[e], s_s5i[e], s5_a_re[e], s5_a_im[e], s5_log_dt[e],
                                         s5_b_re[e], s5_b_im[e], s5_c_re[e], s5_c_im[e], s5_d[e],
                                         s5_w_glu[e], s5_b_glu[e])
            n_s5r.append(st_r); n_s5i.append(st_i)
        x = x + (gate * out).astype(x.dtype)
        h, gate = _pre(x, c, norm_g[layer, 1], ada_w[layer, 1], ada_b[layer, 1])
        x = x + (gate * _moe(h, router_w, router_b, moe_w_gate[layer], moe_w_up[layer], moe_w_down[layer])).astype(x.dtype)
    y = _rmsnorm(x, final_norm_g)
    return (y, jnp.stack(n_wkv).astype(s_wkv.dtype), jnp.stack(n_shift).astype(s_shift.dtype),
            jnp.stack(n_lru).astype(s_lru.dtype), jnp.stack(n_conv).astype(s_conv.dtype),
            jnp.stack(n_s5r).astype(s_s5r.dtype), jnp.stack(n_s5i).astype(s_s5i.dtype))


def _fresh(s, batch):
    return jnp.zeros(s.shape[:1] + (batch,) + s.shape[2:], s.dtype)


def setup_inputs(seed: int = 0) -> dict:
    key = jax.random.key(seed)
    ks = iter(jax.random.split(key, 64))
    f32 = jnp.float32
    def nrm(shape, std):
        return std * jax.random.normal(next(ks), shape, f32)
    def unif(shape, lo, hi):
        return jax.random.uniform(next(ks), shape, f32, lo, hi)
    D = D_MODEL
    a_target = unif((N_EVEN, LRU_BLOCKS, LRU_BLOCK), 0.9, 0.999)
    p_sig = a_target ** (1.0 / LRU_C)
    n_idx = jnp.arange(S5_STATE, dtype=f32)
    inp = {}
    inp['x_prompt'] = nrm((BATCH, SEQ, D), 1.0)
    inp['x_sample'] = nrm((DEC_BATCH, DEC_SEQ, D), 1.0)
    inp['state_wkv'] = nrm((N_EVEN, DEC_BATCH, RWKV_HEADS, RWKV_HEAD_DIM, RWKV_HEAD_DIM), 0.3)
    inp['state_shift'] = nrm((N_EVEN, DEC_BATCH, RWKV_COLS), 1.0)
    inp['state_lru'] = nrm((N_EVEN, DEC_BATCH, LRU_DIM), 0.5)
    inp['state_conv'] = nrm((N_EVEN, DEC_BATCH, CONV_W - 1, LRU_DIM), 1.0)
    inp['state_s5_re'] = nrm((N_ODD, DEC_BATCH, S5_GROUPS, S5_STATE), 0.5)
    inp['state_s5_im'] = nrm((N_ODD, DEC_BATCH, S5_GROUPS, S5_STATE), 0.5)
    inp['c_prompt'] = nrm((BATCH, D), 1.0)
    inp['c_sample'] = nrm((DEC_BATCH, D), 1.0)
    inp['norm_g'] = 1.0 + nrm((DEPTH, 2, D), 0.1)
    inp['ada_w'] = nrm((DEPTH, 2, D, 3 * D), 0.5 * D ** -0.5)
    inp['ada_b'] = nrm((DEPTH, 2, 3 * D), 0.1)
    inp['final_norm_g'] = 1.0 + nrm((D,), 0.1)
    inp['even_w_in'] = nrm((N_EVEN, D, IN_COLS), D ** -0.5)
    inp['rwkv_mu'] = unif((N_EVEN, RWKV_COLS), 0.0, 1.0)
    inp['rwkv_w0'] = unif((N_EVEN, RWKV_DIM), -6.0, 1.0)
    inp['rwkv_w2'] = nrm((N_EVEN, W_LORA, RWKV_DIM), 0.05)
    inp['rwkv_a0'] = nrm((N_EVEN, RWKV_DIM), 0.5)
    inp['rwkv_a2'] = nrm((N_EVEN, A_LORA, RWKV_DIM), 0.5 * A_LORA ** -0.5)
    inp['rwkv_g2'] = nrm((N_EVEN, G_LORA, RWKV_DIM), G_LORA ** -0.5)
    inp['rwkv_k_k'] = 0.85 + nrm((N_EVEN, RWKV_DIM), 0.1)
    inp['rwkv_k_a'] = 1.0 + nrm((N_EVEN, RWKV_DIM), 0.1)
    inp['rwkv_r_k'] = nrm((N_EVEN, RWKV_HEADS, RWKV_HEAD_DIM), 0.1)
    inp['rwkv_lnx_w'] = 1.0 + nrm((N_EVEN, RWKV_DIM), 0.1)
    inp['rwkv_lnx_b'] = nrm((N_EVEN, RWKV_DIM), 0.02)
    inp['rwkv_v0'] = nrm((N_EVEN - 1, RWKV_DIM), 0.5)
    inp['rwkv_v1'] = nrm((N_EVEN - 1, RWKV_DIM, V_LORA), RWKV_DIM ** -0.5)
    inp['rwkv_v2'] = nrm((N_EVEN - 1, V_LORA, RWKV_DIM), 0.5 * V_LORA ** -0.5)
    inp['lru_conv_w'] = nrm((N_EVEN, CONV_W, LRU_DIM), 0.5)
    inp['lru_conv_b'] = nrm((N_EVEN, LRU_DIM), 0.02)
    inp['lru_wa'] = nrm((N_EVEN, LRU_BLOCKS, LRU_BLOCK, LRU_BLOCK), LRU_BLOCK ** -0.5)
    inp['lru_ba'] = nrm((N_EVEN, LRU_BLOCKS, LRU_BLOCK), 0.02)
    inp['lru_wx'] = nrm((N_EVEN, LRU_BLOCKS, LRU_BLOCK, LRU_BLOCK), LRU_BLOCK ** -0.5)
    inp['lru_bx'] = nrm((N_EVEN, LRU_BLOCKS, LRU_BLOCK), 0.02)
    inp['lru_lam'] = jnp.log(p_sig) - jnp.log1p(-p_sig)
    inp['even_w_out'] = nrm((N_EVEN, MIX_DIM, D), MIX_DIM ** -0.5)
    inp['s5_a_re'] = -0.5 * jnp.exp(nrm((N_ODD, S5_GROUPS, S5_STATE), 0.05))
    inp['s5_a_im'] = math.pi * n_idx + nrm((N_ODD, S5_GROUPS, S5_STATE), 0.01)
    inp['s5_log_dt'] = unif((N_ODD, S5_GROUPS), math.log(1e-3), math.log(1e-1))
    inp['s5_b_re'] = nrm((N_ODD, S5_GROUPS, S5_STATE, S5_GROUP), (2 * S5_GROUP) ** -0.5)
    inp['s5_b_im'] = nrm((N_ODD, S5_GROUPS, S5_STATE, S5_GROUP), (2 * S5_GROUP) ** -0.5)
    inp['s5_c_re'] = nrm((N_ODD, S5_GROUPS, S5_GROUP, S5_STATE), (2 * S5_STATE) ** -0.5)
    inp['s5_c_im'] = nrm((N_ODD, S5_GROUPS, S5_GROUP, S5_STATE), (2 * S5_STATE) ** -0.5)
    inp['s5_d'] = nrm((N_ODD, D), 1.0)
    inp['s5_w_glu'] = nrm((N_ODD, D, 2 * D), D ** -0.5)
    inp['s5_b_glu'] = nrm((N_ODD, 2 * D), 0.02)
    inp['router_w'] = nrm((D, N_EXPERTS), D ** -0.5)
    inp['router_b'] = nrm((N_EXPERTS,), 0.01)
    inp['moe_w_gate'] = nrm((DEPTH, N_EXPERTS, D, D_EXPERT), D ** -0.5)
    inp['moe_w_up'] = nrm((DEPTH, N_EXPERTS, D, D_EXPERT), D ** -0.5)
    inp['moe_w_down'] = nrm((DEPTH, N_EXPERTS, D_EXPERT, D), D_EXPERT ** -0.5)
    return inp


def reference(x_prompt, x_sample, state_wkv, state_shift, state_lru, state_conv, state_s5_re, state_s5_im,
              c_prompt, c_sample, norm_g, ada_w, ada_b, final_norm_g, even_w_in, rwkv_mu, rwkv_w0, rwkv_w2,
              rwkv_a0, rwkv_a2, rwkv_g2, rwkv_k_k, rwkv_k_a, rwkv_r_k, rwkv_lnx_w, rwkv_lnx_b, rwkv_v0, rwkv_v1,
              rwkv_v2, lru_conv_w, lru_conv_b, lru_wa, lru_ba, lru_wx, lru_bx, lru_lam, even_w_out,
              s5_a_re, s5_a_im, s5_log_dt, s5_b_re, s5_b_im, s5_c_re, s5_c_im, s5_d, s5_w_glu, s5_b_glu,
              router_w, router_b, moe_w_gate, moe_w_up, moe_w_down):
    params = (norm_g, ada_w, ada_b, final_norm_g, even_w_in, rwkv_mu, rwkv_w0, rwkv_w2, rwkv_a0, rwkv_a2,
              rwkv_g2, rwkv_k_k, rwkv_k_a, rwkv_r_k, rwkv_lnx_w, rwkv_lnx_b, rwkv_v0, rwkv_v1, rwkv_v2,
              lru_conv_w, lru_conv_b, lru_wa, lru_ba, lru_wx, lru_bx, lru_lam, even_w_out,
              s5_a_re, s5_a_im, s5_log_dt, s5_b_re, s5_b_im, s5_c_re, s5_c_im, s5_d, s5_w_glu, s5_b_glu,
              router_w, router_b, moe_w_gate, moe_w_up, moe_w_down)
    bp = x_prompt.shape[0]
    (y_prompt, wkv_p, shift_p, lru_p, conv_p, s5r_p, s5i_p) = _trunk(
        x_prompt, c_prompt, _fresh(state_wkv, bp), _fresh(state_shift, bp), _fresh(state_lru, bp),
        _fresh(state_conv, bp), _fresh(state_s5_re, bp), _fresh(state_s5_im, bp), params)
    (y_sample, wkv_s, shift_s, lru_s, conv_s, s5r_s, s5i_s) = _trunk(
        x_sample, c_sample, state_wkv, state_shift, state_lru, state_conv, state_s5_re, state_s5_im, params)
    return (y_prompt, y_sample, wkv_p, shift_p, lru_p, conv_p, s5r_p, s5i_p,
            wkv_s, shift_s, lru_s, conv_s, s5r_s, s5i_s)
```

```python
import functools
import math

import jax
import jax.numpy as jnp
from jax import lax
from jax.experimental import pallas as pl
from jax.experimental.pallas import tpu as pltpu

F32 = jnp.float32
BF16 = jnp.bfloat16

D = 1024
HEADS = 8
HEAD_DIM = 64
RW = HEADS * HEAD_DIM
OFF_W = 3 * RW
OFF_A = OFF_W + 64
OFF_G = OFF_A + 64
RCOLS = OFF_G + 128
LRU = 512
CONV_W = 4
IN_COLS = RCOLS + 2 * LRU
S5_GROUP = 16
S5_GROUPS = 64
S5_STATE = 64
S5_LANES = S5_GROUPS * S5_STATE
S5_SUPER = 4
N_EXPERTS = 16
N_EGROUPS = 4
EGROUP = 4
D_EXPERT = 256
RMS_EPS = 1e-6
LNX_EPS = 64e-5
LRU_C = 8.0

WKV_CHUNK = 64
MIX_ROWS = 256
S5_STEPS = 32
VMEM_LIMIT = 56 * 1024 * 1024


def _cparams(sem):
    return pltpu.CompilerParams(dimension_semantics=sem, vmem_limit_bytes=VMEM_LIMIT)


def _dot(a, b):
    return jnp.dot(a.astype(BF16), b.astype(BF16), preferred_element_type=F32)


def _dot_hi(a, b):
    return jnp.dot(a, b, precision=lax.Precision.HIGHEST, preferred_element_type=F32)


def _dot_nt(a, b):
    return lax.dot_general(a.astype(BF16), b.astype(BF16), (((1,), (1,)), ((), ())),
                           preferred_element_type=F32)


def _dot_tn(a, b):
    return lax.dot_general(a.astype(BF16), b.astype(BF16), (((0,), (0,)), ((), ())),
                           preferred_element_type=F32)


def _sigmoid(x):
    return 1.0 / (1.0 + jnp.exp(-x))


def _silu(x):
    return x * _sigmoid(x)


def _softplus(x):
    return jnp.maximum(x, 0.0) + jnp.log1p(jnp.exp(-jnp.abs(x)))


def _gelu(x):
    c = math.sqrt(2.0 / math.pi)
    return 0.5 * x * (1.0 + jnp.tanh(c * (x + 0.044715 * (x * x * x))))


def _rms(x, g):
    ms = jnp.mean(x * x, axis=-1, keepdims=True)
    return x * lax.rsqrt(ms + RMS_EPS) * g


def _modulate(y, mod):
    return y * (1.0 + mod[:, D:2 * D]) + mod[:, :D]


def _full(shape):
    n = len(shape)
    return pl.BlockSpec(shape, lambda *_: (0,) * n)


def _resident(shape):
    n = len(shape)
    return pl.BlockSpec(shape, lambda *_: (0,) * n, pipeline_mode=pl.Buffered(1))


def _row_spec(layout, tm, width, nt):
    if layout == "bt":
        return pl.BlockSpec((tm, width), lambda bi, ti: (bi * nt + ti, 0))
    return pl.BlockSpec((tm, width), lambda bi, ti: (ti, bi))


def _as_layout(x, layout, nb, t):
    return x if layout == "bt" else x.reshape(t, nb * x.shape[-1])


def _ada_kernel(c_ref, w_ref, b_ref, o_ref):
    c = c_ref[...]
    o_ref[0] = _dot(_silu(c), w_ref[0]) + b_ref[0]


def _ada_all(c_all, ada_w, ada_b):
    n_sub = ada_w.shape[0] * ada_w.shape[1]
    w = ada_w.reshape(n_sub, D, 3 * D)
    b = ada_b.reshape(n_sub, 1, 3 * D)
    rows = c_all.shape[0]
    tn = 768
    return pl.pallas_call(
        _ada_kernel,
        grid=(n_sub, 3 * D // tn),
        in_specs=[pl.BlockSpec((rows, D), lambda s, j: (0, 0)),
                  pl.BlockSpec((1, D, tn), lambda s, j: (s, 0, j)),
                  pl.BlockSpec((1, 1, tn), lambda s, j: (s, 0, j))],
        out_specs=pl.BlockSpec((1, rows, tn), lambda s, j: (s, 0, j)),
        out_shape=jax.ShapeDtypeStruct((n_sub, rows, 3 * D), F32),
        compiler_params=_cparams(("parallel", "parallel")),
        name="ada_ln",
    )(c_all, w, b)


def _in_proj_kernel(x_ref, mod_ref, g_ref, w_ref, o_ref):
    h = _modulate(_rms(x_ref[...], g_ref[...]), mod_ref[0])
    o_ref[...] = _dot(h, w_ref[...])


def _in_proj(x, mod, norm_g, w_in_bf, nb, t, tm):
    nt = t // tm
    return pl.pallas_call(
        _in_proj_kernel,
        grid=(nb, nt),
        in_specs=[_row_spec("bt", tm, D, nt),
                  pl.BlockSpec((1, mod.shape[1], 3 * D), lambda bi, ti: (bi, 0, 0)),
                  _full((1, D)),
                  _resident((D, IN_COLS))],
        out_specs=_row_spec("bt", tm, IN_COLS, nt),
        out_shape=jax.ShapeDtypeStruct((nb * t, IN_COLS), F32),
        compiler_params=_cparams(("parallel", "parallel")),
        name="even_in_proj",
    )(x, mod, norm_g, w_in_bf)


def _rwkv_rows(pr, prev, vf, p, has_vlora):
    ps = pr + (prev - pr) * p["mu"]
    r = ps[:, :RW]
    k = ps[:, RW:2 * RW]
    v = ps[:, 2 * RW:OFF_W]
    wd = ps[:, OFF_W:OFF_A]
    ad = ps[:, OFF_A:OFF_G]
    gd = ps[:, OFF_G:RCOLS]
    w_log = -_softplus(-(p["w0"] + _dot(jnp.tanh(wd), p["w2"]))) - 0.5
    lw = -jnp.exp(w_log)
    if has_vlora:
        v = v + (vf - v) * _sigmoid(p["v0"] + _dot(_dot(v, p["v1"]), p["v2"]))
    a = _sigmoid(p["a0"] + _dot(ad, p["a2"]))
    g = _dot(_sigmoid(gd), p["g2"])
    kk = k * p["k_k"]
    ss = _dot_hi(kk * kk, p["hsum"])
    kk = kk * lax.rsqrt(jnp.maximum(ss, 1e-24))
    k = k * (1.0 + (a - 1.0) * p["k_a"])
    return r, lw, k, v, -kk, kk * a, g


def _rwkv_post(y, r, k, v, g, p):
    hs = p["hsum"]
    mean = _dot_hi(y, hs) * (1.0 / HEAD_DIM)
    d = y - mean
    var = _dot_hi(d * d, hs) * (1.0 / HEAD_DIM)
    yn = d * lax.rsqrt(var + LNX_EPS) * p["lnx_w"] + p["lnx_b"]
    bonus = _dot_hi(r * k * p["r_k"], hs) * v
    return (yn + bonus) * g


def _lru_rows(xc, p):
    gate_r = _sigmoid(_dot(xc, p["wa"]) + p["ba"])
    gate_i = _sigmoid(_dot(xc, p["wx"]) + p["bx"])
    log_a = -LRU_C * gate_r * _softplus(-p["lam"])
    a_t = jnp.exp(log_a)
    b_t = jnp.sqrt(1.0 - jnp.exp(2.0 * log_a)) * gate_i * xc
    return a_t, b_t


_EVEN_PARAM_NAMES = ("mu", "w0", "w2", "a0", "a2", "g2", "k_k", "k_a", "r_k", "lnx_w", "lnx_b",
                     "v0", "v1", "v2", "conv_w", "conv_b", "wa", "ba", "wx", "bx", "lam", "hsum")


def _load_params(refs):
    return {n: r[...] for n, r in zip(_EVEN_PARAM_NAMES, refs)}


def _wkv_chunk(r, lw, k, v, a, b, s, cst):
    L = r.shape[0]
    g = _dot_hi(cst["tri"], lw)
    gl = g[L - 1:L]
    eg = jnp.exp(g)
    ieg = jnp.exp(-g)
    egl = jnp.exp(gl - g)
    rt = r * eg
    at = a * jnp.exp(g - lw)
    bt = b * ieg
    kt = k * ieg
    m0 = cst["m0"]

    def stk(x):
        return jnp.concatenate([jnp.where(m0, x, 0.0), jnp.where(m0, 0.0, x)], axis=0)

    a4 = jnp.concatenate([stk(at), stk(rt)], axis=0)
    b4 = jnp.concatenate([stk(bt), stk(kt)], axis=0)
    pm = _dot_nt(a4, b4)
    L2 = 2 * L
    n_ab = jnp.where(cst["smask"], pm[:L2, :L2], 0.0)
    a_ak = jnp.where(cst["smask"], pm[:L2, L2:], 0.0)
    a_rb = jnp.where(cst["imask"], pm[L2:, :L2], 0.0)
    a_rk = jnp.where(cst["imask"], pm[L2:, L2:], 0.0)
    ah = _dot_nt(a4, s)
    v2 = stk(v)
    rhs = ah[:L2] + _dot(a_ak, v2)
    t_inv = cst["eye"] + n_ab
    pk = n_ab
    for _ in range(int(math.log2(L)) - 1):
        pk = _dot_hi(pk, pk)
        t_inv = t_inv + _dot_hi(t_inv, pk)
    u2 = _dot_hi(t_inv, rhs)
    uv = jnp.concatenate([u2, v2], axis=0)
    y2 = ah[L2:] + _dot(jnp.concatenate([a_rb, a_rk], axis=1), uv)
    y = y2[:L] + y2[L:]
    bk2 = jnp.concatenate([stk(b * egl), stk(k * egl)], axis=0)
    s_new = s * jnp.exp(gl) + _dot_tn(uv, bk2)
    return y, s_new


def _wkv_consts(L):
    L2 = 2 * L
    ri = lax.broadcasted_iota(jnp.int32, (L2, L2), 0)
    ci = lax.broadcasted_iota(jnp.int32, (L2, L2), 1)
    same = (ri >= L) == (ci >= L)
    rl = lax.broadcasted_iota(jnp.int32, (L, L), 0)
    cl = lax.broadcasted_iota(jnp.int32, (L, L), 1)
    lane = lax.broadcasted_iota(jnp.int32, (1, 2 * HEAD_DIM), 1)
    return {
        "smask": same & (ri > ci),
        "imask": same & (ri >= ci),
        "eye": (ri == ci).astype(F32),
        "tri": (rl >= cl).astype(F32),
        "m0": lane < HEAD_DIM,
    }


def _shift_rows(x, d, fill_rows, row):
    out = pltpu.roll(x, d, 0)
    for t in range(d):
        out = jnp.where(row == t, fill_rows[t], out)
    return out


def _linear_scan_rows(a, b, h0, row):
    n = a.shape[0]
    d = 1
    while d < n:
        keep = row >= d
        a_sh = jnp.where(keep, pltpu.roll(a, d, 0), 1.0)
        b_sh = jnp.where(keep, pltpu.roll(b, d, 0), 0.0)
        b = a * b_sh + b
        a = a * a_sh
        d *= 2
    return a * h0 + b


def _even_mix_kernel(has_vlora, tb, *refs):
    n_par = len(_EVEN_PARAM_NAMES)
    x_ref, proj_ref, mod_ref, vf_ref = refs[:4]
    p_refs = refs[4:4 + n_par]
    wout_ref = refs[4 + n_par]
    (xo_ref, vfo_ref, wkv_ref, shift_ref, lru_ref, conv_ref) = refs[5 + n_par:11 + n_par]
    (s_scr, prev_scr, hist_scr, h_scr, r_s, lw_s, k_s, v_s, a_s, b_s, y_s) = refs[11 + n_par:]
    ti = pl.program_id(1)
    L = WKV_CHUNK

    @pl.when(ti == 0)
    def _():
        s_scr[...] = jnp.zeros_like(s_scr)
        prev_scr[...] = jnp.zeros_like(prev_scr)
        hist_scr[...] = jnp.zeros_like(hist_scr)
        h_scr[...] = jnp.zeros_like(h_scr)

    p = _load_params(p_refs)
    row = lax.broadcasted_iota(jnp.int32, (tb, 1), 0)
    proj = proj_ref[...]
    pr = proj[:, :RCOLS]
    prev = _shift_rows(pr, 1, [prev_scr[...]], row)
    prev_scr[...] = pr[tb - 1:tb]
    r, lw, k, v, a, b, g = _rwkv_rows(pr, prev, vf_ref[...], p, has_vlora)
    vfo_ref[...] = v
    r_s[...] = r
    lw_s[...] = lw
    k_s[...] = k
    v_s[...] = v
    a_s[...] = a
    b_s[...] = b

    cst = _wkv_consts(L)

    def chunk(c, carry):
        rows = pl.ds(pl.multiple_of(c * L, L), L)
        for pi in range(HEADS // 2):
            ln = slice(pi * 128, (pi + 1) * 128)
            y, s_new = _wkv_chunk(r_s[rows, ln], lw_s[rows, ln], k_s[rows, ln], v_s[rows, ln],
                                  a_s[rows, ln], b_s[rows, ln], s_scr[pi], cst)
            y_s[rows, ln] = y
            s_scr[pi] = s_new
        return carry

    lax.fori_loop(0, tb // L, chunk, 0)
    y_rwkv = _rwkv_post(y_s[...], r, k, v, g, p)

    lx = proj[:, RCOLS:RCOLS + LRU]
    lg = proj[:, RCOLS + LRU:]
    hist = hist_scr[...]
    cw = p["conv_w"]
    xc = p["conv_b"] + lx * cw[3:4]
    for d in range(1, CONV_W):
        fill = [hist[CONV_W - 1 - d + t:CONV_W - d + t] for t in range(d)]
        xc = xc + _shift_rows(lx, d, fill, row) * cw[CONV_W - 1 - d:CONV_W - d]
    hist_scr[0:CONV_W - 1] = lx[tb - (CONV_W - 1):tb]
    a_t, b_t = _lru_rows(xc, p)
    hs = _linear_scan_rows(a_t, b_t, h_scr[...], row)
    h_scr[...] = hs[tb - 1:tb]
    y_lru = hs * _gelu(lg)

    wout = wout_ref[...]
    out = _dot(y_rwkv, wout[:RW]) + _dot(y_lru, wout[RW:])
    xo_ref[...] = x_ref[...] + mod_ref[0][:, 2 * D:] * out

    @pl.when(ti == pl.num_programs(1) - 1)
    def _():
        wkv_ref[0] = s_scr[...]
        shift_ref[0] = prev_scr[...]
        lru_ref[0] = h_scr[...]
        conv_ref[0] = hist_scr[0:CONV_W - 1]


def _even_params(e, P):
    eye8 = jnp.eye(HEADS, dtype=F32)

    def bdiag(w):
        return jnp.einsum("ncd,nm->ncmd", w, eye8).reshape(LRU, LRU)

    hsum = jnp.kron(eye8, jnp.ones((HEAD_DIM, HEAD_DIM), F32))
    row = lambda v: v.reshape(1, -1)
    has_vlora = e > 0
    ev = max(e - 1, 0)
    return {
        "mu": row(P["rwkv_mu"][e]), "w0": row(P["rwkv_w0"][e]), "w2": P["rwkv_w2"][e],
        "a0": row(P["rwkv_a0"][e]), "a2": P["rwkv_a2"][e], "g2": P["rwkv_g2"][e],
        "k_k": row(P["rwkv_k_k"][e]), "k_a": row(P["rwkv_k_a"][e]), "r_k": row(P["rwkv_r_k"][e]),
        "lnx_w": row(P["rwkv_lnx_w"][e]), "lnx_b": row(P["rwkv_lnx_b"][e]),
        "v0": row(P["rwkv_v0"][ev]), "v1": P["rwkv_v1"][ev], "v2": P["rwkv_v2"][ev],
        "conv_w": P["lru_conv_w"][e], "conv_b": row(P["lru_conv_b"][e]),
        "wa": bdiag(P["lru_wa"][e]).astype(BF16), "ba": row(P["lru_ba"][e]),
        "wx": bdiag(P["lru_wx"][e]).astype(BF16), "bx": row(P["lru_bx"][e]),
        "lam": row(P["lru_lam"][e]), "hsum": hsum,
    }, has_vlora


def _even_mix_prompt(x, proj, mod, vf, ep, has_vlora, w_out_bf, nb, t):
    tb = min(MIX_ROWS, t)
    nt = t // tb
    pvals = [ep[n] for n in _EVEN_PARAM_NAMES]
    rows = lambda w: _row_spec("bt", tb, w, nt)
    per_b = lambda shape: pl.BlockSpec((1,) + shape, lambda bi, ti: (bi,) + (0,) * len(shape))
    out_shapes = (
        jax.ShapeDtypeStruct((nb * t, D), F32),
        jax.ShapeDtypeStruct((nb * t, RW), F32),
        jax.ShapeDtypeStruct((nb, HEADS // 2, 128, 128), F32),
        jax.ShapeDtypeStruct((nb, 1, RCOLS), F32),
        jax.ShapeDtypeStruct((nb, 1, LRU), F32),
        jax.ShapeDtypeStruct((nb, CONV_W - 1, LRU), F32),
    )
    scratch = [pltpu.VMEM((HEADS // 2, 128, 128), F32), pltpu.VMEM((1, RCOLS), F32),
               pltpu.VMEM((8, LRU), F32), pltpu.VMEM((1, LRU), F32)]
    scratch += [pltpu.VMEM((tb, RW), F32) for _ in range(7)]
    return pl.pallas_call(
        functools.partial(_even_mix_kernel, has_vlora, tb),
        grid=(nb, nt),
        in_specs=[rows(D), rows(IN_COLS), per_b((1, 3 * D)), rows(RW)]
                 + [_full(v.shape) for v in pvals] + [_resident((D, D))],
        out_specs=(rows(D), rows(RW), per_b((HEADS // 2, 128, 128)), per_b((1, RCOLS)),
                   per_b((1, LRU)), per_b((CONV_W - 1, LRU))),
        out_shape=out_shapes,
        scratch_shapes=scratch,
        compiler_params=_cparams(("parallel", "arbitrary")),
        name="even_mix_prompt",
    )(x, proj, mod, vf, *pvals, w_out_bf)


def _even_pre_sample_kernel(has_vlora, *refs):
    n_par = len(_EVEN_PARAM_NAMES)
    proj_ref, shift_ref, c0_ref, c1_ref, c2_ref, h_ref, vf_ref = refs[:7]
    p = _load_params(refs[7:7 + n_par])
    (r_o, w_o, k_o, v_o, a_o, b_o, g_o, ylru_o, h_o) = refs[7 + n_par:]
    proj = proj_ref[...]
    pr = proj[:, :RCOLS]
    r, lw, k, v, a, b, g = _rwkv_rows(pr, shift_ref[...], vf_ref[...], p, has_vlora)
    r_o[...] = r
    w_o[...] = jnp.exp(lw)
    k_o[...] = k
    v_o[...] = v
    a_o[...] = a
    b_o[...] = b
    g_o[...] = g
    lx = proj[:, RCOLS:RCOLS + LRU]
    lg = proj[:, RCOLS + LRU:]
    cw = p["conv_w"]
    xc = (p["conv_b"] + c0_ref[...] * cw[0:1] + c1_ref[...] * cw[1:2] + c2_ref[...] * cw[2:3]
          + lx * cw[3:4])
    a_t, b_t = _lru_rows(xc, p)
    hs = a_t * h_ref[...] + b_t
    h_o[...] = hs
    ylru_o[...] = hs * _gelu(lg)


def _wkv_step_kernel(s_ref, r_ref, w_ref, k_ref, v_ref, a_ref, b_ref, so_ref, y_ref):
    G = s_ref.shape[0]
    N = HEAD_DIM
    ones = jnp.ones((N, N), F32)
    eye = (lax.broadcasted_iota(jnp.int32, (N, N), 0)
           == lax.broadcasted_iota(jnp.int32, (N, N), 1)).astype(F32)
    bc = lambda ref: jnp.broadcast_to(ref[...], (G, N, N)).reshape(G * N, N)
    s = s_ref[...].reshape(G * N, N)
    sa = _dot_hi(s * bc(a_ref), ones)
    v_col = _dot_hi((eye[None] * v_ref[...]).reshape(G * N, N), ones)
    s_new = s * bc(w_ref) + sa * bc(b_ref) + v_col * bc(k_ref)
    so_ref[...] = s_new.reshape(G, N, N)
    y_bc = _dot_hi(s_new * bc(r_ref), ones).reshape(G, N, N)
    y_ref[...] = jnp.sum(y_bc * eye[None], axis=1, keepdims=True)


def _wkv_step(state, ops):
    bh = state.shape[0]
    gb = 64
    s_spec = pl.BlockSpec((gb, HEAD_DIM, HEAD_DIM), lambda i: (i, 0, 0))
    o_spec = pl.BlockSpec((gb, 1, HEAD_DIM), lambda i: (i, 0, 0))
    return pl.pallas_call(
        _wkv_step_kernel,
        grid=(bh // gb,),
        in_specs=[s_spec] + [o_spec] * 6,
        out_specs=(s_spec, o_spec),
        out_shape=(jax.ShapeDtypeStruct(state.shape, F32),
                   jax.ShapeDtypeStruct((bh, 1, HEAD_DIM), F32)),
        compiler_params=_cparams(("parallel",)),
        name="wkv_step_sample",
    )(state, *ops)


def _even_post_sample_kernel(x_ref, mod_ref, y_ref, r_ref, k_ref, v_ref, g_ref, ylru_ref,
                             rk_ref, lw_ref, lb_ref, hsum_ref, wout_ref, xo_ref):
    p = {"r_k": rk_ref[...], "lnx_w": lw_ref[...], "lnx_b": lb_ref[...], "hsum": hsum_ref[...]}
    y_rwkv = _rwkv_post(y_ref[...], r_ref[...], k_ref[...], v_ref[...], g_ref[...], p)
    wout = wout_ref[...]
    out = _dot(y_rwkv, wout[:RW]) + _dot(ylru_ref[...], wout[RW:])
    xo_ref[...] = x_ref[...] + mod_ref[0][:, 2 * D:] * out


def _even_mix_sample(x, proj, mod, vf, s_wkv, s_shift, s_lru, s_conv, ep, has_vlora, w_out_bf):
    nb = x.shape[0]
    pvals = [ep[n] for n in _EVEN_PARAM_NAMES]
    rw = jax.ShapeDtypeStruct((nb, RW), F32)
    ins = [proj, s_shift, s_conv[:, 0], s_conv[:, 1], s_conv[:, 2], s_lru, vf]
    r, w, k, v, a, b, g, y_lru, h_new = pl.pallas_call(
        functools.partial(_even_pre_sample_kernel, has_vlora),
        in_specs=[_full(t.shape) for t in ins] + [_full(t.shape) for t in pvals],
        out_specs=tuple(_full((nb, RW)) for _ in range(9)),
        out_shape=(rw,) * 9,
        grid=(1,),
        compiler_params=_cparams(("arbitrary",)),
        name="even_pre_sample",
    )(*ins, *pvals)
    heads = lambda t: t.reshape(nb * HEADS, 1, HEAD_DIM)
    s_new, y = _wkv_step(s_wkv.reshape(nb * HEADS, HEAD_DIM, HEAD_DIM),
                         [heads(t) for t in (r, w, k, v, a, b)])
    y = y.reshape(nb, RW)
    post_in = [x, mod, y, r, k, v, g, y_lru, ep["r_k"], ep["lnx_w"], ep["lnx_b"], ep["hsum"], w_out_bf]
    x_new = pl.pallas_call(
        _even_post_sample_kernel,
        in_specs=[_full(t.shape) for t in post_in],
        out_specs=_full((nb, D)),
        out_shape=jax.ShapeDtypeStruct((nb, D), F32),
        grid=(1,),
        compiler_params=_cparams(("arbitrary",)),
        name="even_post_sample",
    )(*post_in)
    lx = proj[:, RCOLS:RCOLS + LRU]
    conv_new = jnp.concatenate([s_conv[:, 1:], lx[:, None]], axis=1)
    v_first = v
    return (x_new, v_first, s_new.reshape(nb, HEADS, HEAD_DIM, HEAD_DIM), proj[:, :RCOLS], h_new,
            conv_new)


def _s5_param_kernel(are_ref, aim_ref, ldt_ref, bre_ref, bim_ref, abr_ref, abi_ref, bbr_ref, bbi_ref):
    a_re = are_ref[...]
    a_im = aim_ref[...]
    dt = jnp.exp(ldt_ref[...])
    mag = jnp.exp(a_re * dt)
    abr = mag * jnp.cos(a_im * dt)
    abi = mag * jnp.sin(a_im * dt)
    den = a_re * a_re + a_im * a_im
    nr = abr - 1.0
    cr = (nr * a_re + abi * a_im) / den
    ci = (abi * a_re - nr * a_im) / den
    b_re = bre_ref[...]
    b_im = bim_ref[...]
    abr_ref[...] = abr
    abi_ref[...] = abi
    bbr_ref[...] = cr * b_re - ci * b_im
    bbi_ref[...] = cr * b_im + ci * b_re


def _s5_params(a_re, a_im, log_dt, b_re, b_im, c_re, c_im):
    gp = S5_GROUPS * S5_STATE
    col = lambda t: t.reshape(gp, 1)
    ldt = jnp.broadcast_to(log_dt[:, None], (S5_GROUPS, S5_STATE)).reshape(gp, 1)
    ins = [col(a_re), col(a_im), ldt, b_re.reshape(gp, S5_GROUP), b_im.reshape(gp, S5_GROUP)]
    abr, abi, bbr, bbi = pl.pallas_call(
        _s5_param_kernel,
        in_specs=[_full(t.shape) for t in ins],
        out_specs=(_full((gp, 1)), _full((gp, 1)), _full((gp, S5_GROUP)), _full((gp, S5_GROUP))),
        out_shape=(jax.ShapeDtypeStruct((gp, 1), F32),) * 2
                  + (jax.ShapeDtypeStruct((gp, S5_GROUP), F32),) * 2,
        grid=(1,),
        compiler_params=_cparams(("arbitrary",)),
        name="s5_discretise",
    )(*ins)
    gl = S5_GROUPS // S5_SUPER
    eye = jnp.eye(gl, dtype=F32)

    def b_bd(bb):
        t = bb.reshape(S5_SUPER, gl, S5_STATE, S5_GROUP)
        return jnp.einsum("sgpc,gh->sgchp", t, eye).reshape(S5_SUPER, gl * S5_GROUP, gl * S5_STATE)

    def c_bd(cc):
        t = cc.reshape(S5_SUPER, gl, S5_GROUP, S5_STATE)
        return jnp.einsum("sgcp,gh->sgphc", t, eye).reshape(S5_SUPER, gl * S5_STATE, gl * S5_GROUP)

    b_mat = jnp.concatenate([b_bd(bbr), b_bd(bbi)], axis=2).astype(BF16)
    c_mat = jnp.concatenate([c_bd(c_re), -c_bd(c_im)], axis=1).astype(BF16)
    return abr.reshape(1, gp), abi.reshape(1, gp), b_mat, c_mat


def _s5_kernel(bb, tc, x_ref, mod_ref, g_ref, abr_ref, abi_ref, bmat_ref, cmat_ref, dsk_ref,
               wglu_ref, bglu_ref, s0r_ref, s0i_ref, xo_ref, sr_ref, si_ref,
               xr_scr, xi_scr, str_scr, sti_scr):
    ti = pl.program_id(0)
    rows = tc * bb
    sw = S5_LANES // S5_SUPER

    @pl.when(ti == 0)
    def _():
        str_scr[...] = s0r_ref[...]
        sti_scr[...] = s0i_ref[...]

    x = x_ref[...]
    mod = mod_ref[...]
    y = _rms(x, g_ref[...]).reshape(tc, bb, D)
    u = (y * (1.0 + mod[:, D:2 * D])[None] + mod[:, :D][None]).reshape(rows, D)
    cw = S5_GROUPS // S5_SUPER * S5_GROUP
    for sg in range(S5_SUPER):
        bu = _dot(u[:, sg * cw:(sg + 1) * cw], bmat_ref[sg])
        xr_scr[:, sg * sw:(sg + 1) * sw] = bu[:, :sw]
        xi_scr[:, sg * sw:(sg + 1) * sw] = bu[:, sw:]

    lw = 1024
    for rg in range(bb // 8):
        for lc in range(S5_LANES // lw):
            ln = slice(lc * lw, (lc + 1) * lw)
            ar = jnp.broadcast_to(abr_ref[:, ln], (8, lw))
            ai = jnp.broadcast_to(abi_ref[:, ln], (8, lw))
            sr0 = str_scr[rg * 8:(rg + 1) * 8, ln]
            si0 = sti_scr[rg * 8:(rg + 1) * 8, ln]

            def step(t, carry):
                sr, si = carry
                rr = pl.ds(pl.multiple_of(t * bb + rg * 8, 8), 8)
                nr = ar * sr - ai * si + xr_scr[rr, ln]
                ni = ar * si + ai * sr + xi_scr[rr, ln]
                xr_scr[rr, ln] = nr
                xi_scr[rr, ln] = ni
                return nr, ni

            sr1, si1 = lax.fori_loop(0, tc, step, (sr0, si0))
            str_scr[rg * 8:(rg + 1) * 8, ln] = sr1
            sti_scr[rg * 8:(rg + 1) * 8, ln] = si1

    ys = []
    for sg in range(S5_SUPER):
        ys.append(_dot(xr_scr[:, sg * sw:(sg + 1) * sw], cmat_ref[sg, 0:sw, :])
                  + _dot(xi_scr[:, sg * sw:(sg + 1) * sw], cmat_ref[sg, sw:2 * sw, :]))
    yy = jnp.concatenate(ys, axis=1) + dsk_ref[...] * u
    gl = _dot(_gelu(yy), wglu_ref[...]) + bglu_ref[...]
    out = gl[:, :D] * _sigmoid(gl[:, D:])
    gate = mod[:, 2 * D:]
    xo_ref[...] = x + (out.reshape(tc, bb, D) * gate[None]).reshape(rows, D)

    @pl.when(ti == pl.num_programs(0) - 1)
    def _():
        sr_ref[...] = str_scr[...]
        si_ref[...] = sti_scr[...]


def _s5_layer(x_tb, mod, norm_g, sp, d_skip, w_glu_bf, b_glu, s0r, s0i, bb, t):
    abr, abi, b_mat, c_mat = sp
    tc = min(S5_STEPS, t)
    rows = tc * bb
    ins = [x_tb, mod, norm_g, abr, abi, b_mat, c_mat, d_skip, w_glu_bf, b_glu, s0r, s0i]
    in_specs = [pl.BlockSpec((rows, D), lambda i: (i, 0))] + [_full(v.shape) for v in ins[1:]]
    st = jax.ShapeDtypeStruct((bb, S5_LANES), F32)
    return pl.pallas_call(
        functools.partial(_s5_kernel, bb, tc),
        grid=(t // tc,),
        in_specs=in_specs,
        out_specs=(pl.BlockSpec((rows, D), lambda i: (i, 0)), _full((bb, S5_LANES)),
                   _full((bb, S5_LANES))),
        out_shape=(jax.ShapeDtypeStruct((t * bb, D), F32), st, st),
        scratch_shapes=[pltpu.VMEM((rows, S5_LANES), F32), pltpu.VMEM((rows, S5_LANES), F32),
                        pltpu.VMEM((bb, S5_LANES), F32), pltpu.VMEM((bb, S5_LANES), F32)],
        compiler_params=_cparams(("arbitrary",)),
        name="s5_layer",
    )(*ins)


def _route(logits_t, rb):
    s = _sigmoid(logits_t)
    sel = s + rb
    rows = [sel[e:e + 1] for e in range(N_EXPERTS)]
    scores = []
    for gi in range(N_EGROUPS):
        m = rows[gi * EGROUP:(gi + 1) * EGROUP]
        best = None
        for i in range(EGROUP):
            for j in range(i + 1, EGROUP):
                pair = m[i] + m[j]
                best = pair if best is None else jnp.maximum(best, pair)
        scores.append(best)
    top = scores[0]
    grp = jnp.zeros_like(top, dtype=jnp.int32)
    for gi in range(1, N_EGROUPS):
        better = scores[gi] > top
        grp = jnp.where(better, gi, grp)
        top = jnp.where(better, scores[gi], top)
    picked = []
    for e in range(N_EXPERTS):
        gi = e // EGROUP
        rank = jnp.zeros_like(grp)
        for m in range(gi * EGROUP, (gi + 1) * EGROUP):
            if m == e:
                continue
            ahead = (rows[m] > rows[e]) if m > e else (rows[m] >= rows[e])
            rank = rank + ahead.astype(jnp.int32)
        chosen = (grp == gi) & (rank < 2)
        picked.append(jnp.where(chosen, s[e:e + 1], 0.0))
    w = jnp.concatenate(picked, axis=0)
    return w / jnp.sum(w, axis=0, keepdims=True)


ROUTER_ROWS = 128


def _moe_kernel(final, x_ref, mod_ref, g_ref, rw_ref, rb_ref, wg_ref, wu_ref, wd_ref, fg_ref, o_ref):
    x = x_ref[...]
    mod = mod_ref[0]
    h = _modulate(_rms(x, g_ref[...]), mod)
    logits_t = lax.dot_general(rw_ref[...], h, (((1,), (1,)), ((), ())),
                               precision=lax.Precision.HIGHEST, preferred_element_type=F32)
    gates_t = _route(logits_t[:N_EXPERTS], rb_ref[...])
    pad = jnp.zeros((ROUTER_ROWS - N_EXPERTS, gates_t.shape[1]), F32)
    gates = jnp.transpose(jnp.concatenate([gates_t, pad], axis=0))
    hb = h.astype(BF16)
    acc = jnp.zeros_like(x)
    for e in range(N_EXPERTS):
        hg = jnp.dot(hb, wg_ref[e], preferred_element_type=F32)
        hu = jnp.dot(hb, wu_ref[e], preferred_element_type=F32)
        act = _silu(hg) * hu * gates[:, e:e + 1]
        acc = acc + jnp.dot(act.astype(BF16), wd_ref[e], preferred_element_type=F32)
    xn = x + mod[:, 2 * D:] * acc
    if final:
        xn = _rms(xn, fg_ref[...])
    o_ref[...] = xn


def _moe_layer(x, mod, norm_g, router_wt, router_b, wg, wu, wd, final_g, final,
               nb, t, tm, in_layout, out_layout):
    nt = t // tm
    xin = _as_layout(x, in_layout, nb, t)
    out_shape = (nb * t, D) if out_layout == "bt" else (t, nb * D)
    out = pl.pallas_call(
        functools.partial(_moe_kernel, final),
        grid=(nb, nt),
        in_specs=[_row_spec(in_layout, tm, D, nt),
                  pl.BlockSpec((1, mod.shape[1], 3 * D), lambda bi, ti: (bi, 0, 0)),
                  _full((1, D)), _full((ROUTER_ROWS, D)), _full((N_EXPERTS, 1)),
                  _resident(wg.shape), _resident(wu.shape), _resident(wd.shape), _full((1, D))],
        out_specs=_row_spec(out_layout, tm, D, nt),
        out_shape=jax.ShapeDtypeStruct(out_shape, F32),
        compiler_params=_cparams(("parallel", "parallel")),
        name="moe",
    )(xin, mod, norm_g, router_wt, router_b, wg, wu, wd, final_g)
    return out.reshape(nb * t, D)


def _trunk(x3, c_mods, states, P, W, is_prompt):
    B, T, _ = x3.shape
    depth = P["norm_g"].shape[0]
    x = x3.reshape(B * T, D)
    if is_prompt:
        nb, t = B, T
        mods = c_mods.reshape(2 * depth, B, 1, 3 * D)
        tm_in, tm_moe = min(512, T), min(512, T)
    else:
        nb, t = 1, B
        mods = c_mods.reshape(2 * depth, 1, B, 3 * D)
        tm_in, tm_moe = B, B
    bb = B
    outs = {k: [] for k in ("wkv", "shift", "lru", "conv", "s5r", "s5i")}
    v_first = jnp.zeros((B * T, RW), F32)
    layout = "bt"
    for layer in range(depth):
        e = layer // 2
        ng = P["norm_g"][layer]
        mod_a, mod_m = mods[2 * layer], mods[2 * layer + 1]
        if layer % 2 == 0:
            ep, has_vlora = W["even"][e]
            proj = _in_proj(x, mod_a, ng[0:1], W["w_in"][e], nb, t, tm_in)
            if is_prompt:
                x, vf_new, wkv, sh, lr, cv = _even_mix_prompt(
                    x, proj, mod_a, v_first, ep, has_vlora, W["w_out"][e], nb, t)
                idx = jnp.arange(HEADS)
                wkv = wkv.reshape(B, HEADS // 2, 2, HEAD_DIM, 2, HEAD_DIM)
                wkv = wkv[:, idx // 2, idx % 2, :, idx % 2, :]
                wkv = jnp.moveaxis(wkv, 0, 1)
                sh, lr = sh[:, 0], lr[:, 0]
            else:
                x, vf_new, wkv, sh, lr, cv = _even_mix_sample(
                    x, proj, mod_a, v_first, states["wkv"][e], states["shift"][e], states["lru"][e],
                    states["conv"][e], ep, has_vlora, W["w_out"][e])
            if not has_vlora:
                v_first = vf_new
            outs["wkv"].append(wkv); outs["shift"].append(sh); outs["lru"].append(lr)
            outs["conv"].append(cv)
            next_layout = "tb" if is_prompt else "bt"
        else:
            if is_prompt:
                s0r = jnp.zeros((B, S5_LANES), F32)
                s0i = s0r
                mod_s5 = mod_a[:, 0]
                t_s5 = T
            else:
                s0r = states["s5r"][e].reshape(B, S5_LANES)
                s0i = states["s5i"][e].reshape(B, S5_LANES)
                mod_s5 = mod_a[0]
                t_s5 = 1
            x, sr, si = _s5_layer(x, mod_s5, ng[0:1], W["s5"][e], P["s5_d"][e].reshape(1, D),
                                  W["w_glu"][e], P["s5_b_glu"][e].reshape(1, 2 * D), s0r, s0i, bb, t_s5)
            outs["s5r"].append(sr.reshape(B, S5_GROUPS, S5_STATE))
            outs["s5i"].append(si.reshape(B, S5_GROUPS, S5_STATE))
            next_layout = "bt"
        final = layer == depth - 1
        in_l = "tb" if (is_prompt and layer % 2 == 1) else "bt"
        out_l = next_layout if (is_prompt and layer % 2 == 0) else "bt"
        x = _moe_layer(x, mod_m, ng[1:2], W["router_wt"], W["router_b"], W["wg"][layer], W["wu"][layer],
                       W["wd"][layer], P["final_norm_g"].reshape(1, D), final, nb, t, tm_moe, in_l, out_l)
        layout = out_l
    y = x.reshape(B, T, D)
    return (y, jnp.stack(outs["wkv"]), jnp.stack(outs["shift"]), jnp.stack(outs["lru"]),
            jnp.stack(outs["conv"]), jnp.stack(outs["s5r"]), jnp.stack(outs["s5i"]))


def kernel(x_prompt, x_sample, state_wkv, state_shift, state_lru, state_conv, state_s5_re, state_s5_im, c_prompt, c_sample, norm_g, ada_w, ada_b, final_norm_g, even_w_in, rwkv_mu, rwkv_w0, rwkv_w2, rwkv_a0, rwkv_a2, rwkv_g2, rwkv_k_k, rwkv_k_a, rwkv_r_k, rwkv_lnx_w, rwkv_lnx_b, rwkv_v0, rwkv_v1, rwkv_v2, lru_conv_w, lru_conv_b, lru_wa, lru_ba, lru_wx, lru_bx, lru_lam, even_w_out, s5_a_re, s5_a_im, s5_log_dt, s5_b_re, s5_b_im, s5_c_re, s5_c_im, s5_d, s5_w_glu, s5_b_glu, router_w, router_b, moe_w_gate, moe_w_up, moe_w_down):
    P = dict(norm_g=norm_g, final_norm_g=final_norm_g, rwkv_mu=rwkv_mu, rwkv_w0=rwkv_w0,
             rwkv_w2=rwkv_w2, rwkv_a0=rwkv_a0, rwkv_a2=rwkv_a2, rwkv_g2=rwkv_g2, rwkv_k_k=rwkv_k_k,
             rwkv_k_a=rwkv_k_a, rwkv_r_k=rwkv_r_k.reshape(rwkv_r_k.shape[0], RW),
             rwkv_lnx_w=rwkv_lnx_w, rwkv_lnx_b=rwkv_lnx_b, rwkv_v0=rwkv_v0, rwkv_v1=rwkv_v1,
             rwkv_v2=rwkv_v2, lru_conv_w=lru_conv_w, lru_conv_b=lru_conv_b, lru_wa=lru_wa,
             lru_ba=lru_ba.reshape(lru_ba.shape[0], LRU), lru_wx=lru_wx,
             lru_bx=lru_bx.reshape(lru_bx.shape[0], LRU), lru_lam=lru_lam.reshape(lru_lam.shape[0], LRU),
             s5_d=s5_d, s5_b_glu=s5_b_glu)
    n_even, n_odd = even_w_in.shape[0], s5_w_glu.shape[0]
    W = dict(
        even=[_even_params(e, P) for e in range(n_even)],
        w_in=even_w_in.astype(BF16), w_out=even_w_out.astype(BF16), w_glu=s5_w_glu.astype(BF16),
        s5=[_s5_params(s5_a_re[e], s5_a_im[e], s5_log_dt[e], s5_b_re[e], s5_b_im[e], s5_c_re[e],
                       s5_c_im[e]) for e in range(n_odd)],
        router_wt=jnp.pad(router_w.T, ((0, ROUTER_ROWS - N_EXPERTS), (0, 0))),
        router_b=router_b.reshape(N_EXPERTS, 1),
        wg=moe_w_gate.astype(BF16), wu=moe_w_up.astype(BF16), wd=moe_w_down.astype(BF16),
    )
    bp, bs = x_prompt.shape[0], x_sample.shape[0]
    mods = _ada_all(jnp.concatenate([c_prompt, c_sample], axis=0), ada_w, ada_b)
    out_p = _trunk(x_prompt, mods[:, :bp], None, P, W, True)
    st = dict(wkv=state_wkv, shift=state_shift, lru=state_lru, conv=state_conv,
              s5r=state_s5_re, s5i=state_s5_im)
    out_s = _trunk(x_sample, mods[:, bp:], st, P, W, False)
    return (out_p[0], out_s[0]) + tuple(out_p[1:]) + tuple(out_s[1:])
```

```python
import functools
import math

import jax
import jax.numpy as jnp
from jax import lax
from jax.experimental import pallas as pl
from jax.experimental.pallas import tpu as pltpu

F32 = jnp.float32
BF16 = jnp.bfloat16

D = 1024
HEADS = 8
HEAD_DIM = 64
PAIRS = HEADS // 2
RW = HEADS * HEAD_DIM
OFF_W = 3 * RW
OFF_A = OFF_W + 64
OFF_G = OFF_A + 64
RCOLS = OFF_G + 128
LRU = 512
CONV_W = 4
IN_COLS = RCOLS + 2 * LRU
S5_GROUP = 16
S5_GROUPS = 64
S5_STATE = 64
S5_LANES = S5_GROUPS * S5_STATE
S5_SUPER = 4
N_EXPERTS = 16
N_EGROUPS = 4
EGROUP = 4
D_EXPERT = 256
ROUTER_ROWS = 128
RMS_EPS = 1e-6
LNX_EPS = 64e-5
LRU_C = 8.0

WKV_CHUNK = 64
WKV_BATCH = 4
S5_STEPS = 32
MOE_ROWS = 512
VMEM_LIMIT = 56 * 1024 * 1024


def _cparams(sem):
    return pltpu.CompilerParams(dimension_semantics=sem, vmem_limit_bytes=VMEM_LIMIT)


def _dot(a, b):
    return jnp.dot(a.astype(BF16), b.astype(BF16), preferred_element_type=F32)


def _dot_hi(a, b):
    return jnp.dot(a, b, precision=lax.Precision.HIGHEST, preferred_element_type=F32)


def _dot_nt(a, b):
    return lax.dot_general(a.astype(BF16), b.astype(BF16), (((1,), (1,)), ((), ())),
                           preferred_element_type=F32)


def _dot_tn(a, b):
    return lax.dot_general(a.astype(BF16), b.astype(BF16), (((0,), (0,)), ((), ())),
                           preferred_element_type=F32)


_dot_solve = _dot


def _dot_ones(x, ones_bf):
    hi = x.astype(BF16)
    lo = (x - hi.astype(F32)).astype(BF16)
    return (jnp.dot(hi, ones_bf, preferred_element_type=F32)
            + jnp.dot(lo, ones_bf, preferred_element_type=F32))


def _sigmoid(x):
    return 1.0 / (1.0 + jnp.exp(-x))


def _silu(x):
    return x * _sigmoid(x)


def _softplus(x):
    return jnp.maximum(x, 0.0) + jnp.log1p(jnp.exp(-jnp.abs(x)))


def _gelu(x):
    c = math.sqrt(2.0 / math.pi)
    return 0.5 * x * (1.0 + jnp.tanh(c * (x + 0.044715 * (x * x * x))))


def _rms(x, g):
    ms = jnp.mean(x * x, axis=-1, keepdims=True)
    return x * lax.rsqrt(ms + RMS_EPS) * g


def _per_batch(fn, y, *ms):
    nb = ms[0].shape[0]
    rows, w = y.shape
    return fn(y.reshape(rows // nb, nb, w), *[m[None] for m in ms]).reshape(rows, w)


def _modulate(y, mod):
    return _per_batch(lambda y3, sc, sh: y3 * (1.0 + sc) + sh, y, mod[:, D:2 * D], mod[:, :D])


def _gated_residual(x, out, mod):
    return x + _per_batch(lambda o3, gt: o3 * gt, out, mod[:, 2 * D:])


def _full(shape):
    n = len(shape)
    return pl.BlockSpec(shape, lambda *_: (0,) * n)


def _resident(shape):
    n = len(shape)
    return pl.BlockSpec(shape, lambda *_: (0,) * n, pipeline_mode=pl.Buffered(1))


def _rows(tm, width):
    return pl.BlockSpec((tm, width), lambda i: (i, 0))


def _ada_kernel(c_ref, w_ref, b_ref, o_ref):
    o_ref[0] = _dot(_silu(c_ref[...]), w_ref[0]) + b_ref[0]


def _ada_all(c_all, ada_w, ada_b):
    n_sub = ada_w.shape[0] * ada_w.shape[1]
    w = ada_w.reshape(n_sub, D, 3 * D)
    b = ada_b.reshape(n_sub, 1, 3 * D)
    rows = c_all.shape[0]
    tn = 768
    return pl.pallas_call(
        _ada_kernel,
        grid=(n_sub, 3 * D // tn),
        in_specs=[pl.BlockSpec((rows, D), lambda s, j: (0, 0)),
                  pl.BlockSpec((1, D, tn), lambda s, j: (s, 0, j)),
                  pl.BlockSpec((1, 1, tn), lambda s, j: (s, 0, j))],
        out_specs=pl.BlockSpec((1, rows, tn), lambda s, j: (s, 0, j)),
        out_shape=jax.ShapeDtypeStruct((n_sub, rows, 3 * D), F32),
        compiler_params=_cparams(("parallel", "parallel")),
        name="ada_ln",
    )(c_all, w, b)


def _rwkv_rows(pr, prev, vf, p, has_vlora):
    ps = pr + (prev - pr) * p["mu"]
    r = ps[:, :RW]
    k = ps[:, RW:2 * RW]
    v = ps[:, 2 * RW:OFF_W]
    wd = ps[:, OFF_W:OFF_A]
    ad = ps[:, OFF_A:OFF_G]
    gd = ps[:, OFF_G:RCOLS]
    w_log = -_softplus(-(p["w0"] + _dot(jnp.tanh(wd), p["w2"]))) - 0.5
    lw = -jnp.exp(w_log)
    if has_vlora:
        v = v + (vf - v) * _sigmoid(p["v0"] + _dot(_dot(v, p["v1"]), p["v2"]))
    a = _sigmoid(p["a0"] + _dot(ad, p["a2"]))
    g = _dot(_sigmoid(gd), p["g2"])
    kk = k * p["k_k"]
    ss = _dot_ones(kk * kk, p["hsum"])
    kk = kk * lax.rsqrt(jnp.maximum(ss, 1e-24))
    k = k * (1.0 + (a - 1.0) * p["k_a"])
    return r, lw, k, v, -kk, kk * a, g


def _rwkv_post(y, r, k, v, g, p):
    hs = p["hsum"]
    mean = _dot_ones(y, hs) * (1.0 / HEAD_DIM)
    d = y - mean
    var = _dot_ones(d * d, hs) * (1.0 / HEAD_DIM)
    yn = d * lax.rsqrt(var + LNX_EPS) * p["lnx_w"] + p["lnx_b"]
    bonus = _dot_ones(r * k * p["r_k"], hs) * v
    return (yn + bonus) * g


def _lru_rows(xc, p):
    gate_r = _sigmoid(_dot(xc, p["wa"]) + p["ba"])
    gate_i = _sigmoid(_dot(xc, p["wx"]) + p["bx"])
    log_a = -LRU_C * gate_r * _softplus(-p["lam"])
    a_t = jnp.exp(log_a)
    b_t = jnp.sqrt(1.0 - jnp.exp(2.0 * log_a)) * gate_i * xc
    return a_t, b_t


_EVEN_PARAM_NAMES = ("mu", "w0", "w2", "a0", "a2", "g2", "k_k", "k_a", "r_k", "lnx_w", "lnx_b",
                     "v0", "v1", "v2", "conv_w", "conv_b", "wa", "ba", "wx", "bx", "lam", "hsum")


def _load_params(refs):
    return {n: r[...] for n, r in zip(_EVEN_PARAM_NAMES, refs)}


def _even_params(e, P):
    eye8 = jnp.eye(HEADS, dtype=F32)

    def bdiag(w):
        return jnp.einsum("ncd,nm->ncmd", w, eye8).reshape(LRU, LRU)

    hsum = jnp.kron(eye8, jnp.ones((HEAD_DIM, HEAD_DIM), F32)).astype(BF16)
    row = lambda v: v.reshape(1, -1)
    has_vlora = e > 0
    ev = max(e - 1, 0)
    return {
        "mu": row(P["rwkv_mu"][e]), "w0": row(P["rwkv_w0"][e]), "w2": P["rwkv_w2"][e],
        "a0": row(P["rwkv_a0"][e]), "a2": P["rwkv_a2"][e], "g2": P["rwkv_g2"][e],
        "k_k": row(P["rwkv_k_k"][e]), "k_a": row(P["rwkv_k_a"][e]), "r_k": row(P["rwkv_r_k"][e]),
        "lnx_w": row(P["rwkv_lnx_w"][e]), "lnx_b": row(P["rwkv_lnx_b"][e]),
        "v0": row(P["rwkv_v0"][ev]), "v1": P["rwkv_v1"][ev], "v2": P["rwkv_v2"][ev],
        "conv_w": P["lru_conv_w"][e], "conv_b": row(P["lru_conv_b"][e]),
        "wa": bdiag(P["lru_wa"][e]).astype(BF16), "ba": row(P["lru_ba"][e]),
        "wx": bdiag(P["lru_wx"][e]).astype(BF16), "bx": row(P["lru_bx"][e]),
        "lam": row(P["lru_lam"][e]), "hsum": hsum,
    }, has_vlora


def _wkv_consts(L):
    L2 = 2 * L
    ri = lax.broadcasted_iota(jnp.int32, (L2, L2), 0)
    ci = lax.broadcasted_iota(jnp.int32, (L2, L2), 1)
    same = (ri >= L) == (ci >= L)
    lane = lax.broadcasted_iota(jnp.int32, (1, 2 * HEAD_DIM), 1)
    return {
        "smask": same & (ri > ci),
        "imask": same & (ri >= ci),
        "eye": (ri == ci).astype(F32),
        "m0": lane < HEAD_DIM,
    }


def _wkv_chunks(chains, cst):
    L = chains[0][0].shape[0]
    L2 = 2 * L
    m0 = cst["m0"]
    each = lambda fn, *cols: [fn(*xs) for xs in zip(*cols)]

    def stk(x):
        return jnp.concatenate([jnp.where(m0, x, 0.0), jnp.where(m0, 0.0, x)], axis=0)

    rt, at, bt, kt, b2, k2, v, s, egl = [list(c) for c in zip(*chains)]
    a4 = each(lambda a, r: jnp.concatenate([stk(a), stk(r)], axis=0), at, rt)
    b4 = each(lambda b, k: jnp.concatenate([stk(b), stk(k)], axis=0), bt, kt)
    pm = each(_dot_nt, a4, b4)
    ah = each(_dot_nt, a4, s)
    n_ab = [jnp.where(cst["smask"], x[:L2, :L2], 0.0) for x in pm]
    a_ak = [jnp.where(cst["smask"], x[:L2, L2:], 0.0) for x in pm]
    a_r = [jnp.concatenate([jnp.where(cst["imask"], x[L2:, :L2], 0.0),
                            jnp.where(cst["imask"], x[L2:, L2:], 0.0)], axis=1) for x in pm]
    v2 = [stk(x) for x in v]
    rhs = each(lambda h, m, w: h[:L2] + _dot(m, w), ah, a_ak, v2)
    t_inv = [cst["eye"] + n for n in n_ab]
    pk = n_ab
    for _ in range(int(math.log2(L)) - 1):
        pk = each(_dot_solve, pk, pk)
        t_inv = each(lambda t, q: t + _dot_solve(t, q), t_inv, pk)
    u2 = each(_dot_solve, t_inv, rhs)
    uv = each(lambda u, w: jnp.concatenate([u, w], axis=0), u2, v2)
    y2 = each(lambda h, m, w: h[L2:] + _dot(m, w), ah, a_r, uv)
    ys = [x[:L] + x[L:] for x in y2]
    bk = each(lambda b, k: jnp.concatenate([stk(b), stk(k)], axis=0), b2, k2)
    s_new = each(lambda s0, e, w, q: s0 * e + _dot_tn(w, q), s, egl, uv, bk)
    return list(zip(ys, s_new))


def _shift_time(x, fill, nb, d):
    return jnp.concatenate([fill, x[:x.shape[0] - d * nb]], axis=0)


def _time_cumsum(x, nb):
    d = nb
    while d < x.shape[0]:
        x = x + _shift_time(x, jnp.zeros((d, x.shape[1]), F32), 1, d)
        d *= 2
    return x


def _time_linear_scan(a, b, nb):
    d = nb
    while d < a.shape[0]:
        a_sh = _shift_time(a, jnp.ones((d, a.shape[1]), F32), 1, d)
        b_sh = _shift_time(b, jnp.zeros((d, b.shape[1]), F32), 1, d)
        b = a * b_sh + b
        a = a * a_sh
        d *= 2
    return a, b


def _even_layer_kernel(has_vlora, nb, *refs):
    n_par = len(_EVEN_PARAM_NAMES)
    x_ref, mod_ref, ng_ref, win_ref, vf_ref = refs[:5]
    p_refs = refs[5:5 + n_par]
    wout_ref = refs[5 + n_par]
    (xo_ref, vfo_ref, wkv_ref, shift_ref, lru_ref, conv_ref) = refs[6 + n_par:12 + n_par]
    (s_scr, prev_scr, hist_scr, h_scr, egl_scr,
     rt_s, at_s, bt_s, kt_s, b2_s, k2_s, v_s, y_s) = refs[12 + n_par:]
    ti = pl.program_id(0)
    L = WKV_CHUNK
    R = L * nb

    @pl.when(ti == 0)
    def _():
        s_scr[...] = jnp.zeros_like(s_scr)
        prev_scr[...] = jnp.zeros_like(prev_scr)
        hist_scr[...] = jnp.zeros_like(hist_scr)
        h_scr[...] = jnp.zeros_like(h_scr)

    p = _load_params(p_refs)
    x = x_ref[...]
    mod = mod_ref[...]
    proj = _dot(_modulate(_rms(x, ng_ref[...]), mod), win_ref[...])
    pr = proj[:, :RCOLS]
    prev = _shift_time(pr, prev_scr[...], nb, 1)
    prev_scr[...] = pr[R - nb:]
    r, lw, k, v, a, b, g = _rwkv_rows(pr, prev, vf_ref[...], p, has_vlora)
    vfo_ref[...] = jnp.zeros_like(vfo_ref) if has_vlora else v

    gc = _time_cumsum(lw, nb)
    g3 = gc.reshape(L, nb, RW)
    gl = g3[L - 1:L]
    egl = jnp.broadcast_to(jnp.exp(gl), (8, nb, RW)).reshape(8 * nb, RW)
    to_l = jnp.exp(gl - g3).reshape(R, RW)
    ieg = jnp.exp(-gc)
    ops = (r * jnp.exp(gc), a * jnp.exp(gc - lw), b * ieg, k * ieg, b * to_l, k * to_l, v)
    op_refs = (rt_s, at_s, bt_s, kt_s, b2_s, k2_s, v_s)
    for pi in range(PAIRS):
        for ref, val in zip(op_refs, ops):
            ref[pi] = val[:, pi * 128:(pi + 1) * 128]
        egl_scr[pi] = egl[:, pi * 128:(pi + 1) * 128]
    cst = _wkv_consts(L)

    def per_batch(it, carry):
        ids = [(it * WKV_BATCH + j, pi) for j in range(WKV_BATCH) for pi in range(PAIRS)]
        rows = lambda bi: pl.ds(bi, L, stride=nb)
        chains = [[ref[pi, rows(bi), :] for ref in op_refs]
                  + [s_scr[bi * PAIRS + pi], egl_scr[pi, pl.ds(bi, 8, stride=nb), :][0:1]]
                  for bi, pi in ids]
        for (bi, pi), (y, s_new) in zip(ids, _wkv_chunks(chains, cst)):
            y_s[pi, rows(bi), :] = y
            s_scr[bi * PAIRS + pi] = s_new
        return carry

    lax.fori_loop(0, nb // WKV_BATCH, per_batch, 0)
    y_all = jnp.concatenate([y_s[pi] for pi in range(PAIRS)], axis=1)
    y_rwkv = _rwkv_post(y_all, r, k, v, g, p)

    lx = proj[:, RCOLS:RCOLS + LRU]
    lg = proj[:, RCOLS + LRU:]
    xpad = jnp.concatenate([hist_scr[...], lx], axis=0)
    cw = p["conv_w"]
    xc = p["conv_b"] + xpad[0:R] * cw[0:1]
    for j in range(1, CONV_W):
        xc = xc + xpad[j * nb:j * nb + R] * cw[j:j + 1]
    hist_scr[...] = xpad[R:]
    a_t, b_t = _lru_rows(xc, p)
    a_cum, b_cum = _time_linear_scan(a_t, b_t, nb)
    hs = _per_batch(lambda a3, h0: a3 * h0, a_cum, h_scr[...]) + b_cum
    h_scr[...] = hs[R - nb:]
    y_lru = hs * _gelu(lg)

    out = _dot(y_rwkv, wout_ref[0:RW, :]) + _dot(y_lru, wout_ref[RW:, :])
    xo_ref[...] = _gated_residual(x, out, mod)

    @pl.when(ti == pl.num_programs(0) - 1)
    def _():
        wkv_ref[...] = s_scr[...]
        shift_ref[...] = prev_scr[...]
        lru_ref[...] = h_scr[...]
        conv_ref[...] = hist_scr[...]


def _even_layer_prompt(x, mod, norm_g, w_in_bf, vf, ep, has_vlora, w_out_bf, nb, t):
    L = WKV_CHUNK
    R = L * nb
    pvals = [ep[n] for n in _EVEN_PARAM_NAMES]
    out_shapes = [
        jax.ShapeDtypeStruct((t * nb, D), F32),
        jax.ShapeDtypeStruct((t * nb, RW) if not has_vlora else (8, 128), F32),
        jax.ShapeDtypeStruct((nb * PAIRS, 128, 128), F32),
        jax.ShapeDtypeStruct((nb, RCOLS), F32),
        jax.ShapeDtypeStruct((nb, LRU), F32),
        jax.ShapeDtypeStruct(((CONV_W - 1) * nb, LRU), F32),
    ]
    out_specs = [_rows(R, D), _rows(R, RW) if not has_vlora else _full((8, 128)),
                 _full((nb * PAIRS, 128, 128)), _full((nb, RCOLS)), _full((nb, LRU)),
                 _full(((CONV_W - 1) * nb, LRU))]
    scratch = [pltpu.VMEM((nb * PAIRS, 128, 128), F32), pltpu.VMEM((nb, RCOLS), F32),
               pltpu.VMEM(((CONV_W - 1) * nb, LRU), F32), pltpu.VMEM((nb, LRU), F32),
               pltpu.VMEM((PAIRS, 8 * nb, 128), F32)]
    scratch +=[pltpu.VMEM((PAIRS, R, 128), F32) for _ in range(8)]
    vf_spec = _rows(R, RW) if has_vlora else _full(vf.shape)
    return pl.pallas_call(
        functools.partial(_even_layer_kernel, has_vlora, nb),
        grid=(t // L,),
        in_specs=[_rows(R, D), _full(mod.shape), _full((1, D)), _resident((D, IN_COLS)), vf_spec]
                 + [_full(v.shape) for v in pvals] + [_resident((D, D))],
        out_specs=out_specs,
        out_shape=out_shapes,
        scratch_shapes=scratch,
        compiler_params=_cparams(("arbitrary",)),
        name="even_layer_prompt",
    )(x, mod, norm_g, w_in_bf, vf, *pvals, w_out_bf)


def _in_proj_kernel(x_ref, mod_ref, g_ref, w_ref, o_ref):
    o_ref[...] = _dot(_modulate(_rms(x_ref[...], g_ref[...]), mod_ref[...]), w_ref[...])


def _in_proj(x, mod, norm_g, w_in_bf):
    rows = x.shape[0]
    return pl.pallas_call(
        _in_proj_kernel,
        grid=(1,),
        in_specs=[_full((rows, D)), _full(mod.shape), _full((1, D)), _resident((D, IN_COLS))],
        out_specs=_full((rows, IN_COLS)),
        out_shape=jax.ShapeDtypeStruct((rows, IN_COLS), F32),
        compiler_params=_cparams(("arbitrary",)),
        name="even_in_proj",
    )(x, mod, norm_g, w_in_bf)


def _even_pre_sample_kernel(has_vlora, *refs):
    n_par = len(_EVEN_PARAM_NAMES)
    proj_ref, shift_ref, c0_ref, c1_ref, c2_ref, h_ref, vf_ref = refs[:7]
    p = _load_params(refs[7:7 + n_par])
    (r_o, w_o, k_o, v_o, a_o, b_o, g_o, ylru_o, h_o) = refs[7 + n_par:]
    proj = proj_ref[...]
    pr = proj[:, :RCOLS]
    r, lw, k, v, a, b, g = _rwkv_rows(pr, shift_ref[...], vf_ref[...], p, has_vlora)
    r_o[...] = r
    w_o[...] = jnp.exp(lw)
    k_o[...] = k
    v_o[...] = v
    a_o[...] = a
    b_o[...] = b
    g_o[...] = g
    lx = proj[:, RCOLS:RCOLS + LRU]
    lg = proj[:, RCOLS + LRU:]
    cw = p["conv_w"]
    xc = (p["conv_b"] + c0_ref[...] * cw[0:1] + c1_ref[...] * cw[1:2] + c2_ref[...] * cw[2:3]
          + lx * cw[3:4])
    a_t, b_t = _lru_rows(xc, p)
    hs = a_t * h_ref[...] + b_t
    h_o[...] = hs
    ylru_o[...] = hs * _gelu(lg)


def _wkv_step_kernel(s_ref, r_ref, w_ref, k_ref, v_ref, a_ref, b_ref, so_ref, y_ref):
    G = s_ref.shape[0]
    N = HEAD_DIM
    ones = jnp.ones((N, N), F32)
    eye = (lax.broadcasted_iota(jnp.int32, (N, N), 0)
           == lax.broadcasted_iota(jnp.int32, (N, N), 1)).astype(F32)
    bc = lambda ref: jnp.broadcast_to(ref[...], (G, N, N)).reshape(G * N, N)
    s = s_ref[...].reshape(G * N, N)
    sa = _dot_hi(s * bc(a_ref), ones)
    v_col = _dot_hi((eye[None] * v_ref[...]).reshape(G * N, N), ones)
    s_new = s * bc(w_ref) + sa * bc(b_ref) + v_col * bc(k_ref)
    so_ref[...] = s_new.reshape(G, N, N)
    y_bc = _dot_hi(s_new * bc(r_ref), ones).reshape(G, N, N)
    y_ref[...] = jnp.sum(y_bc * eye[None], axis=1, keepdims=True)


def _wkv_step(state, ops):
    bh = state.shape[0]
    gb = 64
    s_spec = pl.BlockSpec((gb, HEAD_DIM, HEAD_DIM), lambda i: (i, 0, 0))
    o_spec = pl.BlockSpec((gb, 1, HEAD_DIM), lambda i: (i, 0, 0))
    return pl.pallas_call(
        _wkv_step_kernel,
        grid=(bh // gb,),
        in_specs=[s_spec] + [o_spec] * 6,
        out_specs=(s_spec, o_spec),
        out_shape=(jax.ShapeDtypeStruct(state.shape, F32),
                   jax.ShapeDtypeStruct((bh, 1, HEAD_DIM), F32)),
        compiler_params=_cparams(("parallel",)),
        name="wkv_step_sample",
    )(state, *ops)


def _even_post_sample_kernel(x_ref, mod_ref, y_ref, r_ref, k_ref, v_ref, g_ref, ylru_ref,
                             rk_ref, lw_ref, lb_ref, hsum_ref, wout_ref, xo_ref):
    p = {"r_k": rk_ref[...], "lnx_w": lw_ref[...], "lnx_b": lb_ref[...], "hsum": hsum_ref[...]}
    y_rwkv = _rwkv_post(y_ref[...], r_ref[...], k_ref[...], v_ref[...], g_ref[...], p)
    out = _dot(y_rwkv, wout_ref[0:RW, :]) + _dot(ylru_ref[...], wout_ref[RW:, :])
    xo_ref[...] = _gated_residual(x_ref[...], out, mod_ref[...])


def _even_layer_sample(x, mod, norm_g, w_in_bf, vf, s_wkv, s_shift, s_lru, s_conv, ep, has_vlora,
                       w_out_bf):
    nb = x.shape[0]
    proj = _in_proj(x, mod, norm_g, w_in_bf)
    pvals = [ep[n] for n in _EVEN_PARAM_NAMES]
    rw = jax.ShapeDtypeStruct((nb, RW), F32)
    ins = [proj, s_shift, s_conv[:, 0], s_conv[:, 1], s_conv[:, 2], s_lru, vf]
    r, w, k, v, a, b, g, y_lru, h_new = pl.pallas_call(
        functools.partial(_even_pre_sample_kernel, has_vlora),
        in_specs=[_full(t.shape) for t in ins] + [_full(t.shape) for t in pvals],
        out_specs=tuple(_full((nb, RW)) for _ in range(9)),
        out_shape=(rw,) * 9,
        grid=(1,),
        compiler_params=_cparams(("arbitrary",)),
        name="even_pre_sample",
    )(*ins, *pvals)
    heads = lambda t: t.reshape(nb * HEADS, 1, HEAD_DIM)
    s_new, y = _wkv_step(s_wkv.reshape(nb * HEADS, HEAD_DIM, HEAD_DIM),
                         [heads(t) for t in (r, w, k, v, a, b)])
    y = y.reshape(nb, RW)
    post_in = [x, mod, y, r, k, v, g, y_lru, ep["r_k"], ep["lnx_w"], ep["lnx_b"], ep["hsum"], w_out_bf]
    x_new = pl.pallas_call(
        _even_post_sample_kernel,
        in_specs=[_full(t.shape) for t in post_in],
        out_specs=_full((nb, D)),
        out_shape=jax.ShapeDtypeStruct((nb, D), F32),
        grid=(1,),
        compiler_params=_cparams(("arbitrary",)),
        name="even_post_sample",
    )(*post_in)
    lx = proj[:, RCOLS:RCOLS + LRU]
    conv_new = jnp.concatenate([s_conv[:, 1:], lx[:, None]], axis=1)
    return (x_new, v, s_new.reshape(nb, HEADS, HEAD_DIM, HEAD_DIM), proj[:, :RCOLS], h_new,
            conv_new)


def _s5_param_kernel(are_ref, aim_ref, ldt_ref, bre_ref, bim_ref, abr_ref, abi_ref, bbr_ref, bbi_ref):
    a_re = are_ref[...]
    a_im = aim_ref[...]
    dt = jnp.exp(ldt_ref[...])
    mag = jnp.exp(a_re * dt)
    abr = mag * jnp.cos(a_im * dt)
    abi = mag * jnp.sin(a_im * dt)
    den = a_re * a_re + a_im * a_im
    nr = abr - 1.0
    cr = (nr * a_re + abi * a_im) / den
    ci = (abi * a_re - nr * a_im) / den
    b_re = bre_ref[...]
    b_im = bim_ref[...]
    abr_ref[...] = abr
    abi_ref[...] = abi
    bbr_ref[...] = cr * b_re - ci * b_im
    bbi_ref[...] = cr * b_im + ci * b_re


def _s5_params(a_re, a_im, log_dt, b_re, b_im, c_re, c_im):
    gp = S5_GROUPS * S5_STATE
    col = lambda t: t.reshape(gp, 1)
    ldt = jnp.broadcast_to(log_dt[:, None], (S5_GROUPS, S5_STATE)).reshape(gp, 1)
    ins = [col(a_re), col(a_im), ldt, b_re.reshape(gp, S5_GROUP), b_im.reshape(gp, S5_GROUP)]
    abr, abi, bbr, bbi = pl.pallas_call(
        _s5_param_kernel,
        in_specs=[_full(t.shape) for t in ins],
        out_specs=(_full((gp, 1)), _full((gp, 1)), _full((gp, S5_GROUP)), _full((gp, S5_GROUP))),
        out_shape=(jax.ShapeDtypeStruct((gp, 1), F32),) * 2
                  + (jax.ShapeDtypeStruct((gp, S5_GROUP), F32),) * 2,
        grid=(1,),
        compiler_params=_cparams(("arbitrary",)),
        name="s5_discretise",
    )(*ins)
    gl = S5_GROUPS // S5_SUPER
    eye = jnp.eye(gl, dtype=F32)

    def b_bd(bb):
        t = bb.reshape(S5_SUPER, gl, S5_STATE, S5_GROUP)
        return jnp.einsum("sgpc,gh->sgchp", t, eye).reshape(S5_SUPER, gl * S5_GROUP, gl * S5_STATE)

    def c_bd(cc):
        t = cc.reshape(S5_SUPER, gl, S5_GROUP, S5_STATE)
        return jnp.einsum("sgcp,gh->sgphc", t, eye).reshape(S5_SUPER, gl * S5_STATE, gl * S5_GROUP)

    b_mat = jnp.concatenate([b_bd(bbr), b_bd(bbi)], axis=2).astype(BF16)
    c_mat = jnp.concatenate([c_bd(c_re), -c_bd(c_im)], axis=1).astype(BF16)
    return abr.reshape(1, gp), abi.reshape(1, gp), b_mat, c_mat


def _s5_kernel(bb, tc, x_ref, mod_ref, g_ref, abr_ref, abi_ref, bmat_ref, cmat_ref, dsk_ref,
               wglu_ref, bglu_ref, s0r_ref, s0i_ref, xo_ref, sr_ref, si_ref,
               xr_scr, xi_scr, str_scr, sti_scr):
    ti = pl.program_id(0)
    sw = S5_LANES // S5_SUPER

    @pl.when(ti == 0)
    def _():
        str_scr[...] = s0r_ref[...]
        sti_scr[...] = s0i_ref[...]

    x = x_ref[...]
    mod = mod_ref[...]
    u = _modulate(_rms(x, g_ref[...]), mod)
    cw = S5_GROUPS // S5_SUPER * S5_GROUP
    for sg in range(S5_SUPER):
        bu = _dot(u[:, sg * cw:(sg + 1) * cw], bmat_ref[sg])
        xr_scr[:, sg * sw:(sg + 1) * sw] = bu[:, :sw]
        xi_scr[:, sg * sw:(sg + 1) * sw] = bu[:, sw:]

    lw = 1024
    for rg in range(bb // 8):
        for lc in range(S5_LANES // lw):
            ln = slice(lc * lw, (lc + 1) * lw)
            ar = jnp.broadcast_to(abr_ref[:, ln], (8, lw))
            ai = jnp.broadcast_to(abi_ref[:, ln], (8, lw))
            sr0 = str_scr[rg * 8:(rg + 1) * 8, ln]
            si0 = sti_scr[rg * 8:(rg + 1) * 8, ln]

            def step(t, carry):
                sr, si = carry
                rr = pl.ds(pl.multiple_of(t * bb + rg * 8, 8), 8)
                nr = ar * sr - ai * si + xr_scr[rr, ln]
                ni = ar * si + ai * sr + xi_scr[rr, ln]
                xr_scr[rr, ln] = nr
                xi_scr[rr, ln] = ni
                return nr, ni

            sr1, si1 = lax.fori_loop(0, tc, step, (sr0, si0))
            str_scr[rg * 8:(rg + 1) * 8, ln] = sr1
            sti_scr[rg * 8:(rg + 1) * 8, ln] = si1

    ys = []
    for sg in range(S5_SUPER):
        ys.append(_dot(xr_scr[:, sg * sw:(sg + 1) * sw], cmat_ref[sg, 0:sw, :])
                  + _dot(xi_scr[:, sg * sw:(sg + 1) * sw], cmat_ref[sg, sw:2 * sw, :]))
    yy = jnp.concatenate(ys, axis=1) + dsk_ref[...] * u
    gl = _dot(_gelu(yy), wglu_ref[...]) + bglu_ref[...]
    out = gl[:, :D] * _sigmoid(gl[:, D:])
    xo_ref[...] = _gated_residual(x, out, mod)

    @pl.when(ti == pl.num_programs(0) - 1)
    def _():
        sr_ref[...] = str_scr[...]
        si_ref[...] = sti_scr[...]


def _s5_layer(x, mod, norm_g, sp, d_skip, w_glu_bf, b_glu, s0r, s0i, bb, t):
    abr, abi, b_mat, c_mat = sp
    tc = min(S5_STEPS, t)
    rows = tc * bb
    ins = [x, mod, norm_g, abr, abi, b_mat, c_mat, d_skip, w_glu_bf, b_glu, s0r, s0i]
    in_specs = [_rows(rows, D)] + [_full(v.shape) for v in ins[1:]]
    st = jax.ShapeDtypeStruct((bb, S5_LANES), F32)
    return pl.pallas_call(
        functools.partial(_s5_kernel, bb, tc),
        grid=(t // tc,),
        in_specs=in_specs,
        out_specs=(_rows(rows, D), _full((bb, S5_LANES)), _full((bb, S5_LANES))),
        out_shape=(jax.ShapeDtypeStruct((t * bb, D), F32), st, st),
        scratch_shapes=[pltpu.VMEM((rows, S5_LANES), F32), pltpu.VMEM((rows, S5_LANES), F32),
                        pltpu.VMEM((bb, S5_LANES), F32), pltpu.VMEM((bb, S5_LANES), F32)],
        compiler_params=_cparams(("arbitrary",)),
        name="s5_layer",
    )(*ins)


def _route(logits_t, rb):
    s = _sigmoid(logits_t)
    sel = s + rb
    rows = [sel[e:e + 1] for e in range(N_EXPERTS)]
    scores = []
    for gi in range(N_EGROUPS):
        m = rows[gi * EGROUP:(gi + 1) * EGROUP]
        best = None
        for i in range(EGROUP):
            for j in range(i + 1, EGROUP):
                pair = m[i] + m[j]
                best = pair if best is None else jnp.maximum(best, pair)
        scores.append(best)
    top = scores[0]
    grp = jnp.zeros_like(top, dtype=jnp.int32)
    for gi in range(1, N_EGROUPS):
        better = scores[gi] > top
        grp = jnp.where(better, gi, grp)
        top = jnp.where(better, scores[gi], top)
    picked = []
    for e in range(N_EXPERTS):
        gi = e // EGROUP
        rank = jnp.zeros_like(grp)
        for m in range(gi * EGROUP, (gi + 1) * EGROUP):
            if m == e:
                continue
            ahead = (rows[m] > rows[e]) if m > e else (rows[m] >= rows[e])
            rank = rank + ahead.astype(jnp.int32)
        chosen = (grp == gi) & (rank < 2)
        picked.append(jnp.where(chosen, s[e:e + 1], 0.0))
    w = jnp.concatenate(picked, axis=0)
    return w / jnp.sum(w, axis=0, keepdims=True)


def _moe_kernel(final, x_ref, mod_ref, g_ref, rw_ref, rb_ref, wg_ref, wu_ref, wd_ref, fg_ref, o_ref):
    x = x_ref[...]
    mod = mod_ref[...]
    h = _modulate(_rms(x, g_ref[...]), mod)
    logits_t = lax.dot_general(rw_ref[...], h, (((1,), (1,)), ((), ())),
                               precision=lax.Precision.HIGHEST, preferred_element_type=F32)
    gates_t = _route(logits_t[:N_EXPERTS], rb_ref[...])
    pad = jnp.zeros((ROUTER_ROWS - N_EXPERTS, gates_t.shape[1]), F32)
    gates = jnp.transpose(jnp.concatenate([gates_t, pad], axis=0))
    hb = h.astype(BF16)
    acc = jnp.zeros_like(x)
    for e in range(N_EXPERTS):
        hg = jnp.dot(hb, wg_ref[e], preferred_element_type=F32)
        hu = jnp.dot(hb, wu_ref[e], preferred_element_type=F32)
        act = _silu(hg) * hu * gates[:, e:e + 1]
        acc = acc + jnp.dot(act.astype(BF16), wd_ref[e], preferred_element_type=F32)
    xn = _gated_residual(x, acc, mod)
    if final:
        xn = _rms(xn, fg_ref[...])
    o_ref[...] = xn


def _moe_layer(x, mod, norm_g, router_wt, router_b, wg, wu, wd, final_g, final):
    rows = x.shape[0]
    tm = min(MOE_ROWS, rows)
    return pl.pallas_call(
        functools.partial(_moe_kernel, final),
        grid=(rows // tm,),
        in_specs=[_rows(tm, D), _full(mod.shape),
                  _full((1, D)), _full((ROUTER_ROWS, D)), _full((N_EXPERTS, 1)),
                  _resident(wg.shape), _resident(wu.shape), _resident(wd.shape), _full((1, D))],
        out_specs=_rows(tm, D),
        out_shape=jax.ShapeDtypeStruct((rows, D), F32),
        compiler_params=_cparams(("parallel",)),
        name="moe",
    )(x, mod, norm_g, router_wt, router_b, wg, wu, wd, final_g)


def _trunk(x3, mods, states, P, W):
    B, T, _ = x3.shape
    fresh = states is None
    depth = P["norm_g"].shape[0]
    x = jnp.swapaxes(x3, 0, 1).reshape(T * B, D)
    outs = {k: [] for k in ("wkv", "shift", "lru", "conv", "s5r", "s5i")}
    v_first = jnp.zeros((8, 128), F32)
    for layer in range(depth):
        e = layer // 2
        ng = P["norm_g"][layer]
        mod_a, mod_m = mods[2 * layer], mods[2 * layer + 1]
        if layer % 2 == 0:
            ep, has_vlora = W["even"][e]
            if fresh:
                x, vf_new, wkv, sh, lr, cv = _even_layer_prompt(
                    x, mod_a, ng[0:1], W["w_in"][e], v_first, ep, has_vlora, W["w_out"][e], B, T)
                idx = jnp.arange(HEADS)
                wkv = wkv.reshape(B, PAIRS, 2, HEAD_DIM, 2, HEAD_DIM)
                wkv = wkv[:, idx // 2, idx % 2, :, idx % 2, :]
                wkv = jnp.moveaxis(wkv, 0, 1)
                cv = jnp.swapaxes(cv.reshape(CONV_W - 1, B, LRU), 0, 1)
            else:
                x, vf_new, wkv, sh, lr, cv = _even_layer_sample(
                    x, mod_a, ng[0:1], W["w_in"][e], v_first, states["wkv"][e], states["shift"][e],
                    states["lru"][e], states["conv"][e], ep, has_vlora, W["w_out"][e])
            if not has_vlora:
                v_first = vf_new
            outs["wkv"].append(wkv); outs["shift"].append(sh); outs["lru"].append(lr)
            outs["conv"].append(cv)
        else:
            if fresh:
                s0r = jnp.zeros((B, S5_LANES), F32)
                s0i = s0r
            else:
                s0r = states["s5r"][e].reshape(B, S5_LANES)
                s0i = states["s5i"][e].reshape(B, S5_LANES)
            x, sr, si = _s5_layer(x, mod_a, ng[0:1], W["s5"][e], P["s5_d"][e].reshape(1, D),
                                  W["w_glu"][e], P["s5_b_glu"][e].reshape(1, 2 * D), s0r, s0i, B, T)
            outs["s5r"].append(sr.reshape(B, S5_GROUPS, S5_STATE))
            outs["s5i"].append(si.reshape(B, S5_GROUPS, S5_STATE))
        x = _moe_layer(x, mod_m, ng[1:2], W["router_wt"], W["router_b"], W["wg"][layer], W["wu"][layer],
                       W["wd"][layer], P["final_norm_g"].reshape(1, D), layer == depth - 1)
    y = jnp.swapaxes(x.reshape(T, B, D), 0, 1)
    return (y, jnp.stack(outs["wkv"]), jnp.stack(outs["shift"]), jnp.stack(outs["lru"]),
            jnp.stack(outs["conv"]), jnp.stack(outs["s5r"]), jnp.stack(outs["s5i"]))


def kernel(x_prompt, x_sample, state_wkv, state_shift, state_lru, state_conv, state_s5_re, state_s5_im, c_prompt, c_sample, norm_g, ada_w, ada_b, final_norm_g, even_w_in, rwkv_mu, rwkv_w0, rwkv_w2, rwkv_a0, rwkv_a2, rwkv_g2, rwkv_k_k, rwkv_k_a, rwkv_r_k, rwkv_lnx_w, rwkv_lnx_b, rwkv_v0, rwkv_v1, rwkv_v2, lru_conv_w, lru_conv_b, lru_wa, lru_ba, lru_wx, lru_bx, lru_lam, even_w_out, s5_a_re, s5_a_im, s5_log_dt, s5_b_re, s5_b_im, s5_c_re, s5_c_im, s5_d, s5_w_glu, s5_b_glu, router_w, router_b, moe_w_gate, moe_w_up, moe_w_down):
    P = dict(norm_g=norm_g, final_norm_g=final_norm_g, rwkv_mu=rwkv_mu, rwkv_w0=rwkv_w0,
             rwkv_w2=rwkv_w2, rwkv_a0=rwkv_a0, rwkv_a2=rwkv_a2, rwkv_g2=rwkv_g2, rwkv_k_k=rwkv_k_k,
             rwkv_k_a=rwkv_k_a, rwkv_r_k=rwkv_r_k.reshape(rwkv_r_k.shape[0], RW),
             rwkv_lnx_w=rwkv_lnx_w, rwkv_lnx_b=rwkv_lnx_b, rwkv_v0=rwkv_v0, rwkv_v1=rwkv_v1,
             rwkv_v2=rwkv_v2, lru_conv_w=lru_conv_w, lru_conv_b=lru_conv_b, lru_wa=lru_wa,
             lru_ba=lru_ba.reshape(lru_ba.shape[0], LRU), lru_wx=lru_wx,
             lru_bx=lru_bx.reshape(lru_bx.shape[0], LRU), lru_lam=lru_lam.reshape(lru_lam.shape[0], LRU),
             s5_d=s5_d, s5_b_glu=s5_b_glu)
    n_even, n_odd = even_w_in.shape[0], s5_w_glu.shape[0]
    W = dict(
        even=[_even_params(e, P) for e in range(n_even)],
        w_in=even_w_in.astype(BF16), w_out=even_w_out.astype(BF16), w_glu=s5_w_glu.astype(BF16),
        s5=[_s5_params(s5_a_re[e], s5_a_im[e], s5_log_dt[e], s5_b_re[e], s5_b_im[e], s5_c_re[e],
                       s5_c_im[e]) for e in range(n_odd)],
        router_wt=jnp.pad(router_w.T, ((0, ROUTER_ROWS - N_EXPERTS), (0, 0))),
        router_b=router_b.reshape(N_EXPERTS, 1),
        wg=moe_w_gate.astype(BF16), wu=moe_w_up.astype(BF16), wd=moe_w_down.astype(BF16),
    )
    bp = x_prompt.shape[0]
    mods = _ada_all(jnp.concatenate([c_prompt, c_sample], axis=0), ada_w, ada_b)
    out_p = _trunk(x_prompt, mods[:, :bp], None, P, W)
    st = dict(wkv=state_wkv, shift=state_shift, lru=state_lru, conv=state_conv,
              s5r=state_s5_re, s5i=state_s5_im)
    out_s = _trunk(x_sample, mods[:, bp:], st, P, W)
    return (out_p[0], out_s[0]) + tuple(out_p[1:]) + tuple(out_s[1:])
```

```python
import functools
import math

import jax
import jax.numpy as jnp
from jax import lax
from jax.experimental import pallas as pl
from jax.experimental.pallas import tpu as pltpu

F32 = jnp.float32
BF16 = jnp.bfloat16

D = 1024
HEADS = 8
HEAD_DIM = 64
PAIRS = HEADS // 2
RW = HEADS * HEAD_DIM
OFF_W = 3 * RW
OFF_A = OFF_W + 64
OFF_G = OFF_A + 64
RCOLS = OFF_G + 128
LRU = 512
CONV_W = 4
IN_COLS = RCOLS + 2 * LRU
S5_GROUP = 16
S5_GROUPS = 64
S5_STATE = 64
S5_LANES = S5_GROUPS * S5_STATE
S5_SUPER = 4
N_EXPERTS = 16
N_EGROUPS = 4
EGROUP = 4
D_EXPERT = 256
ROUTER_ROWS = 128
RMS_EPS = 1e-6
LNX_EPS = 64e-5
LRU_C = 8.0

WKV_CHUNK = 64
WKV_BATCH = 4
S5_STEPS = 32
MOE_ROWS = 512
VMEM_LIMIT = 56 * 1024 * 1024


def _cparams(sem):
    return pltpu.CompilerParams(dimension_semantics=sem, vmem_limit_bytes=VMEM_LIMIT)


def _dot(a, b):
    return jnp.dot(a.astype(BF16), b.astype(BF16), preferred_element_type=F32)


def _dot_hi(a, b):
    return jnp.dot(a, b, precision=lax.Precision.HIGHEST, preferred_element_type=F32)


def _dot_nt(a, b):
    return lax.dot_general(a.astype(BF16), b.astype(BF16), (((1,), (1,)), ((), ())),
                           preferred_element_type=F32)


def _dot_tn(a, b):
    return lax.dot_general(a.astype(BF16), b.astype(BF16), (((0,), (0,)), ((), ())),
                           preferred_element_type=F32)


_dot_solve = _dot


def _dot_ones(x, ones_bf):
    hi = x.astype(BF16)
    lo = (x - hi.astype(F32)).astype(BF16)
    return (jnp.dot(hi, ones_bf, preferred_element_type=F32)
            + jnp.dot(lo, ones_bf, preferred_element_type=F32))


def _sigmoid(x):
    return 1.0 / (1.0 + jnp.exp(-x))


def _silu(x):
    return x * _sigmoid(x)


def _softplus(x):
    return jnp.maximum(x, 0.0) + jnp.log1p(jnp.exp(-jnp.abs(x)))


def _gelu(x):
    c = math.sqrt(2.0 / math.pi)
    return 0.5 * x * (1.0 + jnp.tanh(c * (x + 0.044715 * (x * x * x))))


def _rms(x, g):
    ms = jnp.mean(x * x, axis=-1, keepdims=True)
    return x * lax.rsqrt(ms + RMS_EPS) * g


def _per_batch(fn, y, *ms):
    nb = ms[0].shape[0]
    rows, w = y.shape
    return fn(y.reshape(rows // nb, nb, w), *[m[None] for m in ms]).reshape(rows, w)


def _modulate(y, mod):
    return _per_batch(lambda y3, sc, sh: y3 * (1.0 + sc) + sh, y, mod[:, D:2 * D], mod[:, :D])


def _gated_residual(x, out, mod):
    return x + _per_batch(lambda o3, gt: o3 * gt, out, mod[:, 2 * D:])


def _full(shape):
    n = len(shape)
    return pl.BlockSpec(shape, lambda *_: (0,) * n)


def _resident(shape):
    n = len(shape)
    return pl.BlockSpec(shape, lambda *_: (0,) * n, pipeline_mode=pl.Buffered(1))


def _rows(tm, width):
    return pl.BlockSpec((tm, width), lambda i: (i, 0))


def _mod_spec(mod):
    mods, sub = mod
    return pl.BlockSpec((None,) + mods.shape[1:], lambda *_: (sub, 0, 0))


def _ada_kernel(n_first, c_ref, w_ref, b_ref, o1_ref, o2_ref):
    m = _dot(_silu(c_ref[...]), w_ref[0]) + b_ref[0]
    o1_ref[0] = m[:n_first]
    o2_ref[0] = m[n_first:]


def _ada_all(c_all, n_first, ada_w, ada_b):
    n_sub = ada_w.shape[0] * ada_w.shape[1]
    w = ada_w.reshape(n_sub, D, 3 * D)
    b = ada_b.reshape(n_sub, 1, 3 * D)
    rows = c_all.shape[0]
    tn = 768
    out = lambda r: (pl.BlockSpec((1, r, tn), lambda s, j: (s, 0, j)),
                     jax.ShapeDtypeStruct((n_sub, r, 3 * D), F32))
    (spec1, shape1), (spec2, shape2) = out(n_first), out(rows - n_first)
    return pl.pallas_call(
        functools.partial(_ada_kernel, n_first),
        grid=(n_sub, 3 * D // tn),
        in_specs=[pl.BlockSpec((rows, D), lambda s, j: (0, 0)),
                  pl.BlockSpec((1, D, tn), lambda s, j: (s, 0, j)),
                  pl.BlockSpec((1, 1, tn), lambda s, j: (s, 0, j))],
        out_specs=(spec1, spec2),
        out_shape=(shape1, shape2),
        compiler_params=_cparams(("parallel", "parallel")),
        name="ada_ln",
    )(c_all, w, b)


def _rwkv_rows(pr, prev, vf, p, has_vlora):
    ps = pr + (prev - pr) * p["mu"]
    r = ps[:, :RW]
    k = ps[:, RW:2 * RW]
    v = ps[:, 2 * RW:OFF_W]
    wd = ps[:, OFF_W:OFF_A]
    ad = ps[:, OFF_A:OFF_G]
    gd = ps[:, OFF_G:RCOLS]
    w_log = -_softplus(-(p["w0"] + _dot(jnp.tanh(wd), p["w2"]))) - 0.5
    lw = -jnp.exp(w_log)
    if has_vlora:
        v = v + (vf - v) * _sigmoid(p["v0"] + _dot(_dot(v, p["v1"]), p["v2"]))
    a = _sigmoid(p["a0"] + _dot(ad, p["a2"]))
    g = _dot(_sigmoid(gd), p["g2"])
    kk = k * p["k_k"]
    ss = _dot(kk * kk, p["hsum"])
    kk = kk * lax.rsqrt(jnp.maximum(ss, 1e-24))
    k = k * (1.0 + (a - 1.0) * p["k_a"])
    return r, lw, k, v, -kk, kk * a, g


def _rwkv_post(y, r, k, v, g, p):
    hs = p["hsum"]
    mean = _dot(y, hs) * (1.0 / HEAD_DIM)
    d = y - mean
    var = _dot(d * d, hs) * (1.0 / HEAD_DIM)
    yn = d * lax.rsqrt(var + LNX_EPS) * p["lnx_w"] + p["lnx_b"]
    bonus = _dot(r * k * p["r_k"], hs) * v
    return (yn + bonus) * g


def _lru_rows(xc, p):
    gate_r = _sigmoid(_dot(xc, p["wa"]) + p["ba"])
    gate_i = _sigmoid(_dot(xc, p["wx"]) + p["bx"])
    log_a = -LRU_C * gate_r * _softplus(-p["lam"])
    a_t = jnp.exp(log_a)
    b_t = jnp.sqrt(1.0 - jnp.exp(2.0 * log_a)) * gate_i * xc
    return a_t, b_t


_EVEN_PARAM_NAMES = ("mu", "w0", "w2", "a0", "a2", "g2", "k_k", "k_a", "r_k", "lnx_w", "lnx_b",
                     "v0", "v1", "v2", "conv_w", "conv_b", "wa", "ba", "wx", "bx", "lam", "hsum")


def _load_params(refs):
    return {n: r[...] for n, r in zip(_EVEN_PARAM_NAMES, refs)}


def _even_params(e, P):
    eye8 = jnp.eye(HEADS, dtype=F32)

    def bdiag(w):
        return jnp.einsum("ncd,nm->ncmd", w, eye8).reshape(LRU, LRU)

    hsum = jnp.kron(eye8, jnp.ones((HEAD_DIM, HEAD_DIM), F32)).astype(BF16)
    row = lambda v: v.reshape(1, -1)
    has_vlora = e > 0
    ev = max(e - 1, 0)
    return {
        "mu": row(P["rwkv_mu"][e]), "w0": row(P["rwkv_w0"][e]), "w2": P["rwkv_w2"][e],
        "a0": row(P["rwkv_a0"][e]), "a2": P["rwkv_a2"][e], "g2": P["rwkv_g2"][e],
        "k_k": row(P["rwkv_k_k"][e]), "k_a": row(P["rwkv_k_a"][e]), "r_k": row(P["rwkv_r_k"][e]),
        "lnx_w": row(P["rwkv_lnx_w"][e]), "lnx_b": row(P["rwkv_lnx_b"][e]),
        "v0": row(P["rwkv_v0"][ev]), "v1": P["rwkv_v1"][ev], "v2": P["rwkv_v2"][ev],
        "conv_w": P["lru_conv_w"][e], "conv_b": row(P["lru_conv_b"][e]),
        "wa": bdiag(P["lru_wa"][e]).astype(BF16), "ba": row(P["lru_ba"][e]),
        "wx": bdiag(P["lru_wx"][e]).astype(BF16), "bx": row(P["lru_bx"][e]),
        "lam": row(P["lru_lam"][e]), "hsum": hsum,
    }, has_vlora


def _wkv_consts(L):
    L2 = 2 * L
    ri = lax.broadcasted_iota(jnp.int32, (L2, L2), 0)
    ci = lax.broadcasted_iota(jnp.int32, (L2, L2), 1)
    same = (ri >= L) == (ci >= L)
    lane = lax.broadcasted_iota(jnp.int32, (1, 2 * HEAD_DIM), 1)
    return {
        "smask": same & (ri > ci),
        "imask": same & (ri >= ci),
        "eye": (ri == ci).astype(F32),
        "m0": lane < HEAD_DIM,
    }


def _wkv_chunks(chains, cst):
    L = chains[0][0].shape[0]
    L2 = 2 * L
    m0 = cst["m0"]
    each = lambda fn, *cols: [fn(*xs) for xs in zip(*cols)]

    def stk(x):
        return jnp.concatenate([jnp.where(m0, x, 0.0), jnp.where(m0, 0.0, x)], axis=0)

    rt, at, bt, kt, b2, k2, v, s, egl = [list(c) for c in zip(*chains)]
    a4 = each(lambda a, r: jnp.concatenate([stk(a), stk(r)], axis=0), at, rt)
    b4 = each(lambda b, k: jnp.concatenate([stk(b), stk(k)], axis=0), bt, kt)
    pm = each(_dot_nt, a4, b4)
    ah = each(_dot_nt, a4, s)
    n_ab = [jnp.where(cst["smask"], x[:L2, :L2], 0.0) for x in pm]
    a_ak = [jnp.where(cst["smask"], x[:L2, L2:], 0.0) for x in pm]
    a_r = [jnp.concatenate([jnp.where(cst["imask"], x[L2:, :L2], 0.0),
                            jnp.where(cst["imask"], x[L2:, L2:], 0.0)], axis=1) for x in pm]
    v2 = [stk(x) for x in v]
    rhs = each(lambda h, m, w: h[:L2] + _dot(m, w), ah, a_ak, v2)
    t_inv = [cst["eye"] + n for n in n_ab]
    pk = n_ab
    for _ in range(int(math.log2(L)) - 1):
        pk = each(_dot_solve, pk, pk)
        t_inv = each(lambda t, q: t + _dot_solve(t, q), t_inv, pk)
    u2 = each(_dot_solve, t_inv, rhs)
    uv = each(lambda u, w: jnp.concatenate([u, w], axis=0), u2, v2)
    y2 = each(lambda h, m, w: h[L2:] + _dot(m, w), ah, a_r, uv)
    ys = [x[:L] + x[L:] for x in y2]
    bk = each(lambda b, k: jnp.concatenate([stk(b), stk(k)], axis=0), b2, k2)
    s_new = each(lambda s0, e, w, q: s0 * e + _dot_tn(w, q), s, egl, uv, bk)
    return list(zip(ys, s_new))


def _shift_time(x, fill, nb, d):
    return jnp.concatenate([fill, x[:x.shape[0] - d * nb]], axis=0)


def _time_cumsum(x, nb):
    d = nb
    while d < x.shape[0]:
        x = x + _shift_time(x, jnp.zeros((d, x.shape[1]), F32), 1, d)
        d *= 2
    return x


def _time_linear_scan(a, b, nb):
    d = nb
    while d < a.shape[0]:
        a_sh = _shift_time(a, jnp.ones((d, a.shape[1]), F32), 1, d)
        b_sh = _shift_time(b, jnp.zeros((d, b.shape[1]), F32), 1, d)
        b = a * b_sh + b
        a = a * a_sh
        d *= 2
    return a, b


def _to_time_major(x_ref, scr, nb, steps):
    planes = x_ref.shape[2] // 128
    for c in range(planes):
        for bi in range(nb):
            scr[c, pl.ds(bi, steps, stride=nb), :] = x_ref[bi, :, c * 128:(c + 1) * 128]
    return jnp.concatenate([scr[c] for c in range(planes)], axis=1)


def _from_time_major(x, o_ref, scr, nb, steps):
    planes = x.shape[1] // 128
    for c in range(planes):
        scr[c] = x[:, c * 128:(c + 1) * 128]
    for c in range(planes):
        for bi in range(nb):
            o_ref[bi, :, c * 128:(c + 1) * 128] = scr[c, pl.ds(bi, steps, stride=nb), :]


def _even_layer_kernel(has_vlora, x_bt, nb, *refs):
    n_par = len(_EVEN_PARAM_NAMES)
    x_ref, mod_ref, ng_ref, win_ref, vf_ref = refs[:5]
    p_refs = refs[5:5 + n_par]
    wout_ref = refs[5 + n_par]
    (xo_ref, vfo_ref, wkv_ref, shift_ref, lru_ref, conv_ref) = refs[6 + n_par:12 + n_par]
    (s_scr, prev_scr, hist_scr, h_scr, egl_scr,
     rt_s, at_s, bt_s, kt_s, b2_s, k2_s, v_s, y_s) = refs[12 + n_par:25 + n_par]
    ti = pl.program_id(0)
    L = WKV_CHUNK
    R = L * nb

    @pl.when(ti == 0)
    def _():
        s_scr[...] = jnp.zeros_like(s_scr)
        prev_scr[...] = jnp.zeros_like(prev_scr)
        hist_scr[...] = jnp.zeros_like(hist_scr)
        h_scr[...] = jnp.zeros_like(h_scr)

    p = _load_params(p_refs)
    x = _to_time_major(x_ref, refs[25 + n_par], nb, L) if x_bt else x_ref[...]
    mod = mod_ref[...]
    proj = _dot(_modulate(_rms(x, ng_ref[...]), mod), win_ref[...])
    pr = proj[:, :RCOLS]
    prev = _shift_time(pr, prev_scr[...], nb, 1)
    prev_scr[...] = pr[R - nb:]
    r, lw, k, v, a, b, g = _rwkv_rows(pr, prev, vf_ref[...], p, has_vlora)
    vfo_ref[...] = jnp.zeros_like(vfo_ref) if has_vlora else v

    gc = _time_cumsum(lw, nb)
    g3 = gc.reshape(L, nb, RW)
    gl = g3[L - 1:L]
    egl = jnp.broadcast_to(jnp.exp(gl), (8, nb, RW)).reshape(8 * nb, RW)
    to_l = jnp.exp(gl - g3).reshape(R, RW)
    ieg = jnp.exp(-gc)
    ops = (r * jnp.exp(gc), a * jnp.exp(gc - lw), b * ieg, k * ieg, b * to_l, k * to_l, v)
    op_refs = (rt_s, at_s, bt_s, kt_s, b2_s, k2_s, v_s)
    for pi in range(PAIRS):
        for ref, val in zip(op_refs, ops):
            ref[pi] = val[:, pi * 128:(pi + 1) * 128]
        egl_scr[pi] = egl[:, pi * 128:(pi + 1) * 128]
    cst = _wkv_consts(L)

    def per_batch(it, carry):
        ids = [(it * WKV_BATCH + j, pi) for j in range(WKV_BATCH) for pi in range(PAIRS)]
        rows = lambda bi: pl.ds(bi, L, stride=nb)
        chains = [[ref[pi, rows(bi), :] for ref in op_refs]
                  + [s_scr[bi * PAIRS + pi], egl_scr[pi, pl.ds(bi, 8, stride=nb), :][0:1]]
                  for bi, pi in ids]
        for (bi, pi), (y, s_new) in zip(ids, _wkv_chunks(chains, cst)):
            y_s[pi, rows(bi), :] = y
            s_scr[bi * PAIRS + pi] = s_new
        return carry

    lax.fori_loop(0, nb // WKV_BATCH, per_batch, 0)
    y_all = jnp.concatenate([y_s[pi] for pi in range(PAIRS)], axis=1)
    y_rwkv = _rwkv_post(y_all, r, k, v, g, p)

    lx = proj[:, RCOLS:RCOLS + LRU]
    lg = proj[:, RCOLS + LRU:]
    xpad = jnp.concatenate([hist_scr[...], lx], axis=0)
    cw = p["conv_w"]
    xc = p["conv_b"] + xpad[0:R] * cw[0:1]
    for j in range(1, CONV_W):
        xc = xc + xpad[j * nb:j * nb + R] * cw[j:j + 1]
    hist_scr[...] = xpad[R:]
    a_t, b_t = _lru_rows(xc, p)
    a_cum, b_cum = _time_linear_scan(a_t, b_t, nb)
    hs = _per_batch(lambda a3, h0: a3 * h0, a_cum, h_scr[...]) + b_cum
    h_scr[...] = hs[R - nb:]
    y_lru = hs * _gelu(lg)

    out = _dot(y_rwkv, wout_ref[0:RW, :]) + _dot(y_lru, wout_ref[RW:, :])
    xo_ref[...] = _gated_residual(x, out, mod)

    @pl.when(ti == pl.num_programs(0) - 1)
    def _():
        wkv_ref[...] = s_scr[...]
        shift_ref[...] = prev_scr[...]
        lru_ref[...] = h_scr[...]
        conv_ref[...] = hist_scr[...]


def _even_layer_prompt(x, mod, norm_g, w_in_bf, vf, ep, has_vlora, w_out_bf, nb, t):
    L = WKV_CHUNK
    R = L * nb
    x_bt = x.ndim == 3
    x_spec = pl.BlockSpec((nb, L, D), lambda i: (0, i, 0)) if x_bt else _rows(R, D)
    pvals = [ep[n] for n in _EVEN_PARAM_NAMES]
    out_shapes = [
        jax.ShapeDtypeStruct((t * nb, D), F32),
        jax.ShapeDtypeStruct((t * nb, RW) if not has_vlora else (8, 128), F32),
        jax.ShapeDtypeStruct((nb * PAIRS, 128, 128), F32),
        jax.ShapeDtypeStruct((nb, RCOLS), F32),
        jax.ShapeDtypeStruct((nb, LRU), F32),
        jax.ShapeDtypeStruct(((CONV_W - 1) * nb, LRU), F32),
    ]
    out_specs = [_rows(R, D), _rows(R, RW) if not has_vlora else _full((8, 128)),
                 _full((nb * PAIRS, 128, 128)), _full((nb, RCOLS)), _full((nb, LRU)),
                 _full(((CONV_W - 1) * nb, LRU))]
    scratch = [pltpu.VMEM((nb * PAIRS, 128, 128), F32), pltpu.VMEM((nb, RCOLS), F32),
               pltpu.VMEM(((CONV_W - 1) * nb, LRU), F32), pltpu.VMEM((nb, LRU), F32),
               pltpu.VMEM((PAIRS, 8 * nb, 128), F32)]
    scratch += [pltpu.VMEM((PAIRS, R, 128), F32) for _ in range(8)]
    scratch += [pltpu.VMEM((D // 128, R, 128), F32)] if x_bt else []
    vf_spec = _rows(R, RW) if has_vlora else _full(vf.shape)
    return pl.pallas_call(
        functools.partial(_even_layer_kernel, has_vlora, x_bt, nb),
        grid=(t // L,),
        in_specs=[x_spec, _mod_spec(mod), _full((1, D)), _resident((D, IN_COLS)), vf_spec]
                 + [_full(v.shape) for v in pvals] + [_resident((D, D))],
        out_specs=out_specs,
        out_shape=out_shapes,
        scratch_shapes=scratch,
        compiler_params=_cparams(("arbitrary",)),
        name="even_layer_prompt",
    )(x, mod[0], norm_g, w_in_bf, vf, *pvals, w_out_bf)


def _in_proj_kernel(x_ref, mod_ref, g_ref, w_ref, o_ref):
    o_ref[...] = _dot(_modulate(_rms(x_ref[...], g_ref[...]), mod_ref[...]), w_ref[...])


def _in_proj(x, mod, norm_g, w_in_bf):
    rows = x.shape[0]
    return pl.pallas_call(
        _in_proj_kernel,
        grid=(1,),
        in_specs=[_full((rows, D)), _mod_spec(mod), _full((1, D)), _resident((D, IN_COLS))],
        out_specs=_full((rows, IN_COLS)),
        out_shape=jax.ShapeDtypeStruct((rows, IN_COLS), F32),
        compiler_params=_cparams(("arbitrary",)),
        name="even_in_proj",
    )(x, mod[0], norm_g, w_in_bf)


def _even_pre_sample_kernel(has_vlora, *refs):
    n_par = len(_EVEN_PARAM_NAMES)
    proj_ref, shift_ref, c0_ref, c1_ref, c2_ref, h_ref, vf_ref = refs[:7]
    p = _load_params(refs[7:7 + n_par])
    (r_o, w_o, k_o, v_o, a_o, b_o, g_o, ylru_o, h_o) = refs[7 + n_par:]
    proj = proj_ref[...]
    pr = proj[:, :RCOLS]
    r, lw, k, v, a, b, g = _rwkv_rows(pr, shift_ref[...], vf_ref[...], p, has_vlora)
    r_o[...] = r
    w_o[...] = jnp.exp(lw)
    k_o[...] = k
    v_o[...] = v
    a_o[...] = a
    b_o[...] = b
    g_o[...] = g
    lx = proj[:, RCOLS:RCOLS + LRU]
    lg = proj[:, RCOLS + LRU:]
    cw = p["conv_w"]
    xc = (p["conv_b"] + c0_ref[...] * cw[0:1] + c1_ref[...] * cw[1:2] + c2_ref[...] * cw[2:3]
          + lx * cw[3:4])
    a_t, b_t = _lru_rows(xc, p)
    hs = a_t * h_ref[...] + b_t
    h_o[...] = hs
    ylru_o[...] = hs * _gelu(lg)


def _wkv_step_kernel(s_ref, r_ref, w_ref, k_ref, v_ref, a_ref, b_ref, so_ref, y_ref):
    G = s_ref.shape[0]
    N = HEAD_DIM
    ones = jnp.ones((N, N), BF16)
    eye = (lax.broadcasted_iota(jnp.int32, (N, N), 0)
           == lax.broadcasted_iota(jnp.int32, (N, N), 1)).astype(F32)
    bc = lambda ref: jnp.broadcast_to(ref[...], (G, N, N)).reshape(G * N, N)
    s = s_ref[...].reshape(G * N, N)
    sa = _dot_ones(s * bc(a_ref), ones)
    v_col = _dot_ones((eye[None] * v_ref[...]).reshape(G * N, N), ones)
    s_new = s * bc(w_ref) + sa * bc(b_ref) + v_col * bc(k_ref)
    so_ref[...] = s_new.reshape(G, N, N)
    y_bc = _dot_ones(s_new * bc(r_ref), ones).reshape(G, N, N)
    y_ref[...] = jnp.sum(y_bc * eye[None], axis=1, keepdims=True)


def _wkv_step(state, ops):
    bh = state.shape[0]
    gb = 64
    s_spec = pl.BlockSpec((gb, HEAD_DIM, HEAD_DIM), lambda i: (i, 0, 0))
    o_spec = pl.BlockSpec((gb, 1, HEAD_DIM), lambda i: (i, 0, 0))
    return pl.pallas_call(
        _wkv_step_kernel,
        grid=(bh // gb,),
        in_specs=[s_spec] + [o_spec] * 6,
        out_specs=(s_spec, o_spec),
        out_shape=(jax.ShapeDtypeStruct(state.shape, F32),
                   jax.ShapeDtypeStruct((bh, 1, HEAD_DIM), F32)),
        compiler_params=_cparams(("parallel",)),
        name="wkv_step_sample",
    )(state, *ops)


def _even_post_sample_kernel(x_ref, mod_ref, y_ref, r_ref, k_ref, v_ref, g_ref, ylru_ref,
                             rk_ref, lw_ref, lb_ref, hsum_ref, wout_ref, xo_ref):
    p = {"r_k": rk_ref[...], "lnx_w": lw_ref[...], "lnx_b": lb_ref[...], "hsum": hsum_ref[...]}
    y_rwkv = _rwkv_post(y_ref[...], r_ref[...], k_ref[...], v_ref[...], g_ref[...], p)
    out = _dot(y_rwkv, wout_ref[0:RW, :]) + _dot(ylru_ref[...], wout_ref[RW:, :])
    xo_ref[...] = _gated_residual(x_ref[...], out, mod_ref[...])


def _even_layer_sample(x, mod, norm_g, w_in_bf, vf, s_wkv, s_shift, s_lru, s_conv, ep, has_vlora,
                       w_out_bf):
    nb = x.shape[0]
    proj = _in_proj(x, mod, norm_g, w_in_bf)
    pvals = [ep[n] for n in _EVEN_PARAM_NAMES]
    rw = jax.ShapeDtypeStruct((nb, RW), F32)
    ins = [proj, s_shift, s_conv[:, 0], s_conv[:, 1], s_conv[:, 2], s_lru, vf]
    r, w, k, v, a, b, g, y_lru, h_new = pl.pallas_call(
        functools.partial(_even_pre_sample_kernel, has_vlora),
        in_specs=[_full(t.shape) for t in ins] + [_full(t.shape) for t in pvals],
        out_specs=tuple(_full((nb, RW)) for _ in range(9)),
        out_shape=(rw,) * 9,
        grid=(1,),
        compiler_params=_cparams(("arbitrary",)),
        name="even_pre_sample",
    )(*ins, *pvals)
    heads = lambda t: t.reshape(nb * HEADS, 1, HEAD_DIM)
    s_new, y = _wkv_step(s_wkv.reshape(nb * HEADS, HEAD_DIM, HEAD_DIM),
                         [heads(t) for t in (r, w, k, v, a, b)])
    y = y.reshape(nb, RW)
    post_in = [x, mod[0], y, r, k, v, g, y_lru, ep["r_k"], ep["lnx_w"], ep["lnx_b"], ep["hsum"],
               w_out_bf]
    x_new = pl.pallas_call(
        _even_post_sample_kernel,
        in_specs=[_full(x.shape), _mod_spec(mod)] + [_full(t.shape) for t in post_in[2:]],
        out_specs=_full((nb, D)),
        out_shape=jax.ShapeDtypeStruct((nb, D), F32),
        grid=(1,),
        compiler_params=_cparams(("arbitrary",)),
        name="even_post_sample",
    )(*post_in)
    lx = proj[:, RCOLS:RCOLS + LRU]
    conv_new = jnp.concatenate([s_conv[:, 1:], lx[:, None]], axis=1)
    return (x_new, v, s_new.reshape(nb, HEADS, HEAD_DIM, HEAD_DIM), proj[:, :RCOLS], h_new,
            conv_new)


def _s5_param_kernel(are_ref, aim_ref, ldt_ref, bre_ref, bim_ref, abr_ref, abi_ref, bbr_ref, bbi_ref):
    a_re = are_ref[...]
    a_im = aim_ref[...]
    dt = jnp.exp(ldt_ref[...])
    mag = jnp.exp(a_re * dt)
    abr = mag * jnp.cos(a_im * dt)
    abi = mag * jnp.sin(a_im * dt)
    den = a_re * a_re + a_im * a_im
    nr = abr - 1.0
    cr = (nr * a_re + abi * a_im) / den
    ci = (abi * a_re - nr * a_im) / den
    b_re = bre_ref[...]
    b_im = bim_ref[...]
    abr_ref[...] = abr
    abi_ref[...] = abi
    bbr_ref[...] = cr * b_re - ci * b_im
    bbi_ref[...] = cr * b_im + ci * b_re


def _s5_params(a_re, a_im, log_dt, b_re, b_im, c_re, c_im):
    gp = S5_GROUPS * S5_STATE
    col = lambda t: t.reshape(gp, 1)
    ldt = jnp.broadcast_to(log_dt[:, None], (S5_GROUPS, S5_STATE)).reshape(gp, 1)
    ins = [col(a_re), col(a_im), ldt, b_re.reshape(gp, S5_GROUP), b_im.reshape(gp, S5_GROUP)]
    abr, abi, bbr, bbi = pl.pallas_call(
        _s5_param_kernel,
        in_specs=[_full(t.shape) for t in ins],
        out_specs=(_full((gp, 1)), _full((gp, 1)), _full((gp, S5_GROUP)), _full((gp, S5_GROUP))),
        out_shape=(jax.ShapeDtypeStruct((gp, 1), F32),) * 2
                  + (jax.ShapeDtypeStruct((gp, S5_GROUP), F32),) * 2,
        grid=(1,),
        compiler_params=_cparams(("arbitrary",)),
        name="s5_discretise",
    )(*ins)
    gl = S5_GROUPS // S5_SUPER
    eye = jnp.eye(gl, dtype=F32)

    def b_bd(bb):
        t = bb.reshape(S5_SUPER, gl, S5_STATE, S5_GROUP)
        return jnp.einsum("sgpc,gh->sgchp", t, eye).reshape(S5_SUPER, gl * S5_GROUP, gl * S5_STATE)

    def c_bd(cc):
        t = cc.reshape(S5_SUPER, gl, S5_GROUP, S5_STATE)
        return jnp.einsum("sgcp,gh->sgphc", t, eye).reshape(S5_SUPER, gl * S5_STATE, gl * S5_GROUP)

    b_mat = jnp.concatenate([b_bd(bbr), b_bd(bbi)], axis=2).astype(BF16)
    c_mat = jnp.concatenate([c_bd(c_re), -c_bd(c_im)], axis=1).astype(BF16)
    return abr.reshape(1, gp), abi.reshape(1, gp), b_mat, c_mat


def _s5_kernel(bb, tc, x_ref, mod_ref, g_ref, abr_ref, abi_ref, bmat_ref, cmat_ref, dsk_ref,
               wglu_ref, bglu_ref, s0r_ref, s0i_ref, xo_ref, sr_ref, si_ref,
               xr_scr, xi_scr, str_scr, sti_scr):
    ti = pl.program_id(0)
    sw = S5_LANES // S5_SUPER

    @pl.when(ti == 0)
    def _():
        str_scr[...] = s0r_ref[...]
        sti_scr[...] = s0i_ref[...]

    x = x_ref[...]
    mod = mod_ref[...]
    u = _modulate(_rms(x, g_ref[...]), mod)
    cw = S5_GROUPS // S5_SUPER * S5_GROUP
    for sg in range(S5_SUPER):
        bu = _dot(u[:, sg * cw:(sg + 1) * cw], bmat_ref[sg])
        xr_scr[:, sg * sw:(sg + 1) * sw] = bu[:, :sw]
        xi_scr[:, sg * sw:(sg + 1) * sw] = bu[:, sw:]

    lw = 1024
    for rg in range(bb // 8):
        for lc in range(S5_LANES // lw):
            ln = slice(lc * lw, (lc + 1) * lw)
            ar = jnp.broadcast_to(abr_ref[:, ln], (8, lw))
            ai = jnp.broadcast_to(abi_ref[:, ln], (8, lw))
            sr = str_scr[rg * 8:(rg + 1) * 8, ln]
            si = sti_scr[rg * 8:(rg + 1) * 8, ln]
            for t in range(tc):
                rr = slice(t * bb + rg * 8, t * bb + rg * 8 + 8)
                sr, si = (ar * sr - ai * si + xr_scr[rr, ln], ar * si + ai * sr + xi_scr[rr, ln])
                xr_scr[rr, ln] = sr
                xi_scr[rr, ln] = si
            str_scr[rg * 8:(rg + 1) * 8, ln] = sr
            sti_scr[rg * 8:(rg + 1) * 8, ln] = si

    ys = []
    for sg in range(S5_SUPER):
        ys.append(_dot(xr_scr[:, sg * sw:(sg + 1) * sw], cmat_ref[sg, 0:sw, :])
                  + _dot(xi_scr[:, sg * sw:(sg + 1) * sw], cmat_ref[sg, sw:2 * sw, :]))
    yy = jnp.concatenate(ys, axis=1) + dsk_ref[...] * u
    gl = _dot(_gelu(yy), wglu_ref[...]) + bglu_ref[...]
    out = gl[:, :D] * _sigmoid(gl[:, D:])
    xo_ref[...] = _gated_residual(x, out, mod)

    @pl.when(ti == pl.num_programs(0) - 1)
    def _():
        sr_ref[...] = str_scr[...]
        si_ref[...] = sti_scr[...]


def _s5_layer(x, mod, norm_g, sp, d_skip, w_glu_bf, b_glu, s0r, s0i, bb, t):
    abr, abi, b_mat, c_mat = sp
    tc = min(S5_STEPS, t)
    rows = tc * bb
    ins = [x, mod[0], norm_g, abr, abi, b_mat, c_mat, d_skip, w_glu_bf, b_glu, s0r, s0i]
    in_specs = [_rows(rows, D), _mod_spec(mod)] + [_full(v.shape) for v in ins[2:]]
    st = jax.ShapeDtypeStruct((bb, S5_LANES), F32)
    return pl.pallas_call(
        functools.partial(_s5_kernel, bb, tc),
        grid=(t // tc,),
        in_specs=in_specs,
        out_specs=(_rows(rows, D), _full((bb, S5_LANES)), _full((bb, S5_LANES))),
        out_shape=(jax.ShapeDtypeStruct((t * bb, D), F32), st, st),
        scratch_shapes=[pltpu.VMEM((rows, S5_LANES), F32), pltpu.VMEM((rows, S5_LANES), F32),
                        pltpu.VMEM((bb, S5_LANES), F32), pltpu.VMEM((bb, S5_LANES), F32)],
        compiler_params=_cparams(("arbitrary",)),
        name="s5_layer",
    )(*ins)


def _route(logits_t, rb):
    s = _sigmoid(logits_t)
    sel = s + rb
    rows = [sel[e:e + 1] for e in range(N_EXPERTS)]
    scores = []
    for gi in range(N_EGROUPS):
        m = rows[gi * EGROUP:(gi + 1) * EGROUP]
        best = None
        for i in range(EGROUP):
            for j in range(i + 1, EGROUP):
                pair = m[i] + m[j]
                best = pair if best is None else jnp.maximum(best, pair)
        scores.append(best)
    top = scores[0]
    grp = jnp.zeros_like(top, dtype=jnp.int32)
    for gi in range(1, N_EGROUPS):
        better = scores[gi] > top
        grp = jnp.where(better, gi, grp)
        top = jnp.where(better, scores[gi], top)
    picked = []
    for e in range(N_EXPERTS):
        gi = e // EGROUP
        rank = jnp.zeros_like(grp)
        for m in range(gi * EGROUP, (gi + 1) * EGROUP):
            if m == e:
                continue
            ahead = (rows[m] > rows[e]) if m > e else (rows[m] >= rows[e])
            rank = rank + ahead.astype(jnp.int32)
        chosen = (grp == gi) & (rank < 2)
        picked.append(jnp.where(chosen, s[e:e + 1], 0.0))
    w = jnp.concatenate(picked, axis=0)
    return w / jnp.sum(w, axis=0, keepdims=True)


def _moe_kernel(final, out_bt, x_ref, mod_ref, g_ref, rw_ref, rb_ref, wg_ref, wu_ref, wd_ref, fg_ref,
                o_ref, *scratch):
    x = x_ref[...]
    mod = mod_ref[...]
    h = _modulate(_rms(x, g_ref[...]), mod)
    rw = rw_ref[...]
    rw_hi = rw.astype(BF16)
    rw_lo = (rw - rw_hi.astype(F32)).astype(BF16)
    h_hi = h.astype(BF16)
    h_lo = (h - h_hi.astype(F32)).astype(BF16)
    logits_t = _dot_nt(rw_hi, h_hi) + (_dot_nt(rw_hi, h_lo) + _dot_nt(rw_lo, h_hi))
    gates_t = _route(logits_t[:N_EXPERTS], rb_ref[...])
    pad = jnp.zeros((ROUTER_ROWS - N_EXPERTS, gates_t.shape[1]), F32)
    gates = jnp.transpose(jnp.concatenate([gates_t, pad], axis=0))
    hb = h.astype(BF16)
    acc = jnp.zeros_like(x)
    for e in range(N_EXPERTS):
        hg = jnp.dot(hb, wg_ref[e], preferred_element_type=F32)
        hu = jnp.dot(hb, wu_ref[e], preferred_element_type=F32)
        act = _silu(hg) * hu * gates[:, e:e + 1]
        acc = acc + jnp.dot(act.astype(BF16), wd_ref[e], preferred_element_type=F32)
    xn = _gated_residual(x, acc, mod)
    if final:
        xn = _rms(xn, fg_ref[...])
    if out_bt:
        nb = mod.shape[0]
        _from_time_major(xn, o_ref, scratch[0], nb, x.shape[0] // nb)
    else:
        o_ref[...] = xn


def _moe_layer(x, mod, norm_g, router_wt, router_b, wg, wu, wd, final_g, final, out_bt):
    rows = x.shape[0]
    nb = mod[0].shape[1]
    tm = min(MOE_ROWS, rows)
    if out_bt:
        out_spec = pl.BlockSpec((nb, tm // nb, D), lambda i: (0, i, 0))
        out_shape = jax.ShapeDtypeStruct((nb, rows // nb, D), F32)
        scratch = [pltpu.VMEM((D // 128, tm, 128), F32)]
    else:
        out_spec, out_shape, scratch = _rows(tm, D), jax.ShapeDtypeStruct((rows, D), F32), []
    return pl.pallas_call(
        functools.partial(_moe_kernel, final, out_bt),
        grid=(rows // tm,),
        in_specs=[_rows(tm, D), _mod_spec(mod),
                  _full((1, D)), _full((ROUTER_ROWS, D)), _full((N_EXPERTS, 1)),
                  _resident(wg.shape), _resident(wu.shape), _resident(wd.shape), _full((1, D))],
        out_specs=out_spec,
        out_shape=out_shape,
        scratch_shapes=scratch,
        compiler_params=_cparams(("parallel",)),
        name="moe",
    )(x, mod[0], norm_g, router_wt, router_b, wg, wu, wd, final_g)


def _trunk(x3, mods, states, P, W):
    B, T, _ = x3.shape
    fresh = states is None
    assert fresh or T == 1, "a group that carries state is stepped one token at a time"
    depth = P["norm_g"].shape[0]
    x = x3 if (fresh and T > 1) else x3.reshape(B, D)
    outs = {k: [] for k in ("wkv", "shift", "lru", "conv", "s5r", "s5i")}
    v_first = jnp.zeros((8, 128), F32)
    for layer in range(depth):
        e = layer // 2
        ng = P["norm_g"][layer]
        mod_a, mod_m = (mods, 2 * layer), (mods, 2 * layer + 1)
        if layer % 2 == 0:
            ep, has_vlora = W["even"][e]
            if fresh:
                x, vf_new, wkv, sh, lr, cv = _even_layer_prompt(
                    x, mod_a, ng[0:1], W["w_in"][e], v_first, ep, has_vlora, W["w_out"][e], B, T)
                idx = jnp.arange(HEADS)
                wkv = wkv.reshape(B, PAIRS, 2, HEAD_DIM, 2, HEAD_DIM)
                wkv = wkv[:, idx // 2, idx % 2, :, idx % 2, :]
                wkv = jnp.moveaxis(wkv, 0, 1)
                cv = jnp.swapaxes(cv.reshape(CONV_W - 1, B, LRU), 0, 1)
            else:
                x, vf_new, wkv, sh, lr, cv = _even_layer_sample(
                    x, mod_a, ng[0:1], W["w_in"][e], v_first, states["wkv"][e], states["shift"][e],
                    states["lru"][e], states["conv"][e], ep, has_vlora, W["w_out"][e])
            if not has_vlora:
                v_first = vf_new
            outs["wkv"].append(wkv); outs["shift"].append(sh); outs["lru"].append(lr)
            outs["conv"].append(cv)
        else:
            if fresh:
                s0r = jnp.zeros((B, S5_LANES), F32)
                s0i = s0r
            else:
                s0r = states["s5r"][e].reshape(B, S5_LANES)
                s0i = states["s5i"][e].reshape(B, S5_LANES)
            x, sr, si = _s5_layer(x, mod_a, ng[0:1], W["s5"][e], P["s5_d"][e].reshape(1, D),
                                  W["w_glu"][e], P["s5_b_glu"][e].reshape(1, 2 * D), s0r, s0i, B, T)
            outs["s5r"].append(sr.reshape(B, S5_GROUPS, S5_STATE))
            outs["s5i"].append(si.reshape(B, S5_GROUPS, S5_STATE))
        last = layer == depth - 1
        x = _moe_layer(x, mod_m, ng[1:2], W["router_wt"], W["router_b"], W["wg"][layer], W["wu"][layer],
                       W["wd"][layer], P["final_norm_g"].reshape(1, D), last, last and x3.shape[1] > 1)
    y = x.reshape(B, T, D)
    return (y, jnp.stack(outs["wkv"]), jnp.stack(outs["shift"]), jnp.stack(outs["lru"]),
            jnp.stack(outs["conv"]), jnp.stack(outs["s5r"]), jnp.stack(outs["s5i"]))


def kernel(x_prompt, x_sample, state_wkv, state_shift, state_lru, state_conv, state_s5_re, state_s5_im, c_prompt, c_sample, norm_g, ada_w, ada_b, final_norm_g, even_w_in, rwkv_mu, rwkv_w0, rwkv_w2, rwkv_a0, rwkv_a2, rwkv_g2, rwkv_k_k, rwkv_k_a, rwkv_r_k, rwkv_lnx_w, rwkv_lnx_b, rwkv_v0, rwkv_v1, rwkv_v2, lru_conv_w, lru_conv_b, lru_wa, lru_ba, lru_wx, lru_bx, lru_lam, even_w_out, s5_a_re, s5_a_im, s5_log_dt, s5_b_re, s5_b_im, s5_c_re, s5_c_im, s5_d, s5_w_glu, s5_b_glu, router_w, router_b, moe_w_gate, moe_w_up, moe_w_down):
    P = dict(norm_g=norm_g, final_norm_g=final_norm_g, rwkv_mu=rwkv_mu, rwkv_w0=rwkv_w0,
             rwkv_w2=rwkv_w2, rwkv_a0=rwkv_a0, rwkv_a2=rwkv_a2, rwkv_g2=rwkv_g2, rwkv_k_k=rwkv_k_k,
             rwkv_k_a=rwkv_k_a, rwkv_r_k=rwkv_r_k.reshape(rwkv_r_k.shape[0], RW),
             rwkv_lnx_w=rwkv_lnx_w, rwkv_lnx_b=rwkv_lnx_b, rwkv_v0=rwkv_v0, rwkv_v1=rwkv_v1,
             rwkv_v2=rwkv_v2, lru_conv_w=lru_conv_w, lru_conv_b=lru_conv_b, lru_wa=lru_wa,
             lru_ba=lru_ba.reshape(lru_ba.shape[0], LRU), lru_wx=lru_wx,
             lru_bx=lru_bx.reshape(lru_bx.shape[0], LRU), lru_lam=lru_lam.reshape(lru_lam.shape[0], LRU),
             s5_d=s5_d, s5_b_glu=s5_b_glu)
    n_even, n_odd = even_w_in.shape[0], s5_w_glu.shape[0]
    W = dict(
        even=[_even_params(e, P) for e in range(n_even)],
        w_in=even_w_in.astype(BF16), w_out=even_w_out.astype(BF16), w_glu=s5_w_glu.astype(BF16),
        s5=[_s5_params(s5_a_re[e], s5_a_im[e], s5_log_dt[e], s5_b_re[e], s5_b_im[e], s5_c_re[e],
                       s5_c_im[e]) for e in range(n_odd)],
        router_wt=jnp.pad(router_w.T, ((0, ROUTER_ROWS - N_EXPERTS), (0, 0))),
        router_b=router_b.reshape(N_EXPERTS, 1),
        wg=moe_w_gate.astype(BF16), wu=moe_w_up.astype(BF16), wd=moe_w_down.astype(BF16),
    )
    bp = x_prompt.shape[0]
    mods_p, mods_s = _ada_all(jnp.concatenate([c_prompt, c_sample], axis=0), bp, ada_w, ada_b)
    out_p = _trunk(x_prompt, mods_p, None, P, W)
    st = dict(wkv=state_wkv, shift=state_shift, lru=state_lru, conv=state_conv,
              s5r=state_s5_re, s5i=state_s5_im)
    out_s = _trunk(x_sample, mods_s, st, P, W)
    return (out_p[0], out_s[0]) + tuple(out_p[1:]) + tuple(out_s[1:])
```

```python
import functools
import math

import jax
import jax.numpy as jnp
from jax import lax
from jax.experimental import pallas as pl
from jax.experimental.pallas import tpu as pltpu

F32 = jnp.float32
BF16 = jnp.bfloat16

D = 1024
HEADS = 8
HEAD_DIM = 64
PAIRS = HEADS // 2
RW = HEADS * HEAD_DIM
OFF_W = 3 * RW
OFF_A = OFF_W + 64
OFF_G = OFF_A + 64
RCOLS = OFF_G + 128
LRU = 512
CONV_W = 4
IN_COLS = RCOLS + 2 * LRU
S5_GROUP = 16
S5_GROUPS = 64
S5_STATE = 64
S5_LANES = S5_GROUPS * S5_STATE
S5_SUPER = 4
N_EXPERTS = 16
N_EGROUPS = 4
EGROUP = 4
D_EXPERT = 256
ROUTER_ROWS = 128
RMS_EPS = 1e-6
LNX_EPS = 64e-5
LRU_C = 8.0

WKV_CHUNK = 64
WKV_BATCH = 4
S5_STEPS = 32
MOE_ROWS = 512
VMEM_LIMIT = 56 * 1024 * 1024


def _cparams(sem):
    return pltpu.CompilerParams(dimension_semantics=sem, vmem_limit_bytes=VMEM_LIMIT)


def _dot(a, b):
    return jnp.dot(a.astype(BF16), b.astype(BF16), preferred_element_type=F32)


def _dot_hi(a, b):
    return jnp.dot(a, b, precision=lax.Precision.HIGHEST, preferred_element_type=F32)


def _dot_nt(a, b):
    return lax.dot_general(a.astype(BF16), b.astype(BF16), (((1,), (1,)), ((), ())),
                           preferred_element_type=F32)


def _dot_tn(a, b):
    return lax.dot_general(a.astype(BF16), b.astype(BF16), (((0,), (0,)), ((), ())),
                           preferred_element_type=F32)


_dot_solve = _dot


def _dot_ones(x, ones_bf, terms=2):
    out = None
    for _ in range(terms):
        part = x.astype(BF16)
        x = x - part.astype(F32)
        d = jnp.dot(part, ones_bf, preferred_element_type=F32)
        out = d if out is None else out + d
    return out


def _sigmoid(x):
    return 1.0 / (1.0 + jnp.exp(-x))


def _silu(x):
    return x * _sigmoid(x)


def _softplus(x):
    return jnp.maximum(x, 0.0) + jnp.log1p(jnp.exp(-jnp.abs(x)))


def _gelu(x):
    c = math.sqrt(2.0 / math.pi)
    return 0.5 * x * (1.0 + jnp.tanh(c * (x + 0.044715 * (x * x * x))))


def _rms(x, g):
    ms = jnp.mean(x * x, axis=-1, keepdims=True)
    return x * lax.rsqrt(ms + RMS_EPS) * g


def _per_batch(fn, y, *ms):
    nb = ms[0].shape[0]
    rows, w = y.shape
    return fn(y.reshape(rows // nb, nb, w), *[m[None] for m in ms]).reshape(rows, w)


def _modulate(y, mod):
    return _per_batch(lambda y3, sc, sh: y3 * (1.0 + sc) + sh, y, mod[:, D:2 * D], mod[:, :D])


def _gated_residual(x, out, mod):
    return x + _per_batch(lambda o3, gt: o3 * gt, out, mod[:, 2 * D:])


def _full(shape):
    n = len(shape)
    return pl.BlockSpec(shape, lambda *_: (0,) * n)


def _resident(shape):
    n = len(shape)
    return pl.BlockSpec(shape, lambda *_: (0,) * n, pipeline_mode=pl.Buffered(1))


def _rows(tm, width):
    return pl.BlockSpec((tm, width), lambda i: (i, 0))


def _pick_spec(sel, single_buffer=False):
    stack, i = sel
    tail = (0,) * (stack.ndim - 1)
    kw = dict(pipeline_mode=pl.Buffered(1)) if single_buffer else {}
    return pl.BlockSpec((None,) + stack.shape[1:], lambda *_: (i,) + tail, **kw)


def _mod_spec(mod):
    return _pick_spec(mod)


def _ada_kernel(n_first, c_ref, w_ref, b_ref, o1_ref, o2_ref):
    m = _dot(_silu(c_ref[...]), w_ref[0]) + b_ref[0]
    o1_ref[0] = m[:n_first]
    o2_ref[0] = m[n_first:]


def _ada_all(c_all, n_first, ada_w, ada_b):
    n_sub = ada_w.shape[0] * ada_w.shape[1]
    w = ada_w.reshape(n_sub, D, 3 * D)
    b = ada_b.reshape(n_sub, 1, 3 * D)
    rows = c_all.shape[0]
    tn = 768
    out = lambda r: (pl.BlockSpec((1, r, tn), lambda s, j: (s, 0, j)),
                     jax.ShapeDtypeStruct((n_sub, r, 3 * D), F32))
    (spec1, shape1), (spec2, shape2) = out(n_first), out(rows - n_first)
    return pl.pallas_call(
        functools.partial(_ada_kernel, n_first),
        grid=(n_sub, 3 * D // tn),
        in_specs=[pl.BlockSpec((rows, D), lambda s, j: (0, 0)),
                  pl.BlockSpec((1, D, tn), lambda s, j: (s, 0, j)),
                  pl.BlockSpec((1, 1, tn), lambda s, j: (s, 0, j))],
        out_specs=(spec1, spec2),
        out_shape=(shape1, shape2),
        compiler_params=_cparams(("parallel", "parallel")),
        name="ada_ln",
    )(c_all, w, b)


def _rwkv_rows(pr, prev, vf, p, has_vlora):
    ps = pr + (prev - pr) * p["mu"]
    r = ps[:, :RW]
    k = ps[:, RW:2 * RW]
    v = ps[:, 2 * RW:OFF_W]
    wd = ps[:, OFF_W:OFF_A]
    ad = ps[:, OFF_A:OFF_G]
    gd = ps[:, OFF_G:RCOLS]
    w_log = -_softplus(-(p["w0"] + _dot(jnp.tanh(wd), p["w2"]))) - 0.5
    lw = -jnp.exp(w_log)
    if has_vlora:
        v = v + (vf - v) * _sigmoid(p["v0"] + _dot(_dot(v, p["v1"]), p["v2"]))
    a = _sigmoid(p["a0"] + _dot(ad, p["a2"]))
    g = _dot(_sigmoid(gd), p["g2"])
    kk = k * p["k_k"]
    ss = _dot(kk * kk, p["hsum"])
    kk = kk * lax.rsqrt(jnp.maximum(ss, 1e-24))
    k = k * (1.0 + (a - 1.0) * p["k_a"])
    return r, lw, k, v, -kk, kk * a, g


def _rwkv_post(y, r, k, v, g, p):
    hs = p["hsum"]
    mean = _dot(y, hs) * (1.0 / HEAD_DIM)
    d = y - mean
    var = _dot(d * d, hs) * (1.0 / HEAD_DIM)
    yn = d * lax.rsqrt(var + LNX_EPS) * p["lnx_w"] + p["lnx_b"]
    bonus = _dot(r * k * p["r_k"], hs) * v
    return (yn + bonus) * g


def _lru_rows(xc, p):
    gate_r = _sigmoid(_dot(xc, p["wa"]) + p["ba"])
    gate_i = _sigmoid(_dot(xc, p["wx"]) + p["bx"])
    log_a = -LRU_C * gate_r * _softplus(-p["lam"])
    a_t = jnp.exp(log_a)
    b_t = jnp.sqrt(1.0 - jnp.exp(2.0 * log_a)) * gate_i * xc
    return a_t, b_t


_EVEN_PARAM_NAMES = ("mu", "w0", "w2", "a0", "a2", "g2", "k_k", "k_a", "r_k", "lnx_w", "lnx_b",
                     "v0", "v1", "v2", "conv_w", "conv_b", "wa", "ba", "wx", "bx", "lam", "hsum")


def _load_params(refs):
    return {n: r[...] for n, r in zip(_EVEN_PARAM_NAMES, refs)}


def _even_params(e, P):
    eye8 = jnp.eye(HEADS, dtype=F32)

    def bdiag(w):
        return jnp.einsum("ncd,nm->ncmd", w, eye8).reshape(LRU, LRU)

    hsum = jnp.kron(eye8, jnp.ones((HEAD_DIM, HEAD_DIM), F32)).astype(BF16)
    row = lambda v: v.reshape(1, -1)
    has_vlora = e > 0
    ev = max(e - 1, 0)
    return {
        "mu": row(P["rwkv_mu"][e]), "w0": row(P["rwkv_w0"][e]), "w2": P["rwkv_w2"][e],
        "a0": row(P["rwkv_a0"][e]), "a2": P["rwkv_a2"][e], "g2": P["rwkv_g2"][e],
        "k_k": row(P["rwkv_k_k"][e]), "k_a": row(P["rwkv_k_a"][e]), "r_k": row(P["rwkv_r_k"][e]),
        "lnx_w": row(P["rwkv_lnx_w"][e]), "lnx_b": row(P["rwkv_lnx_b"][e]),
        "v0": row(P["rwkv_v0"][ev]), "v1": P["rwkv_v1"][ev], "v2": P["rwkv_v2"][ev],
        "conv_w": P["lru_conv_w"][e], "conv_b": row(P["lru_conv_b"][e]),
        "wa": bdiag(P["lru_wa"][e]).astype(BF16), "ba": row(P["lru_ba"][e]),
        "wx": bdiag(P["lru_wx"][e]).astype(BF16), "bx": row(P["lru_bx"][e]),
        "lam": row(P["lru_lam"][e]), "hsum": hsum,
    }, has_vlora


def _wkv_consts(L):
    L2 = 2 * L
    ri = lax.broadcasted_iota(jnp.int32, (L2, L2), 0)
    ci = lax.broadcasted_iota(jnp.int32, (L2, L2), 1)
    same = (ri >= L) == (ci >= L)
    lane = lax.broadcasted_iota(jnp.int32, (1, 2 * HEAD_DIM), 1)
    return {
        "smask": same & (ri > ci),
        "imask": same & (ri >= ci),
        "eye": (ri == ci).astype(F32),
        "m0": lane < HEAD_DIM,
    }


def _wkv_chunks(chains, cst):
    L = chains[0][0].shape[0]
    L2 = 2 * L
    m0 = cst["m0"]
    each = lambda fn, *cols: [fn(*xs) for xs in zip(*cols)]

    def stk(x):
        return jnp.concatenate([jnp.where(m0, x, 0.0), jnp.where(m0, 0.0, x)], axis=0)

    rt, at, bt, kt, b2, k2, v, s, egl = [list(c) for c in zip(*chains)]
    a4 = each(lambda a, r: jnp.concatenate([stk(a), stk(r)], axis=0), at, rt)
    b4 = each(lambda b, k: jnp.concatenate([stk(b), stk(k)], axis=0), bt, kt)
    pm = each(_dot_nt, a4, b4)
    ah = each(_dot_nt, a4, s)
    n_ab = [jnp.where(cst["smask"], x[:L2, :L2], 0.0) for x in pm]
    a_ak = [jnp.where(cst["smask"], x[:L2, L2:], 0.0) for x in pm]
    a_r = [jnp.concatenate([jnp.where(cst["imask"], x[L2:, :L2], 0.0),
                            jnp.where(cst["imask"], x[L2:, L2:], 0.0)], axis=1) for x in pm]
    v2 = [stk(x) for x in v]
    rhs = each(lambda h, m, w: h[:L2] + _dot(m, w), ah, a_ak, v2)
    t_inv = [cst["eye"] + n for n in n_ab]
    pk = n_ab
    for _ in range(int(math.log2(L)) - 1):
        pk = each(_dot_solve, pk, pk)
        t_inv = each(lambda t, q: t + _dot_solve(t, q), t_inv, pk)
    u2 = each(_dot_solve, t_inv, rhs)
    uv = each(lambda u, w: jnp.concatenate([u, w], axis=0), u2, v2)
    y2 = each(lambda h, m, w: h[L2:] + _dot(m, w), ah, a_r, uv)
    ys = [x[:L] + x[L:] for x in y2]
    bk = each(lambda b, k: jnp.concatenate([stk(b), stk(k)], axis=0), b2, k2)
    s_new = each(lambda s0, e, w, q: s0 * e + _dot_tn(w, q), s, egl, uv, bk)
    return list(zip(ys, s_new))


def _shift_time(x, fill, nb, d):
    return jnp.concatenate([fill, x[:x.shape[0] - d * nb]], axis=0)


def _time_cumsum(x, nb):
    d = nb
    while d < x.shape[0]:
        x = x + _shift_time(x, jnp.zeros((d, x.shape[1]), F32), 1, d)
        d *= 2
    return x


def _time_linear_scan(a, b, nb):
    d = nb
    while d < a.shape[0]:
        a_sh = _shift_time(a, jnp.ones((d, a.shape[1]), F32), 1, d)
        b_sh = _shift_time(b, jnp.zeros((d, b.shape[1]), F32), 1, d)
        b = a * b_sh + b
        a = a * a_sh
        d *= 2
    return a, b


def _to_time_major(x_ref, scr, nb, steps):
    planes = x_ref.shape[2] // 128
    for c in range(planes):
        for bi in range(nb):
            scr[c, pl.ds(bi, steps, stride=nb), :] = x_ref[bi, :, c * 128:(c + 1) * 128]
    return jnp.concatenate([scr[c] for c in range(planes)], axis=1)


def _from_time_major(x, o_ref, scr, nb, steps):
    planes = x.shape[1] // 128
    for c in range(planes):
        scr[c] = x[:, c * 128:(c + 1) * 128]
    for c in range(planes):
        for bi in range(nb):
            o_ref[bi, :, c * 128:(c + 1) * 128] = scr[c, pl.ds(bi, steps, stride=nb), :]


def _even_layer_kernel(has_vlora, x_bt, nb, *refs):
    n_par = len(_EVEN_PARAM_NAMES)
    x_ref, mod_ref, ng_ref, win_ref, vf_ref = refs[:5]
    p_refs = refs[5:5 + n_par]
    wout_ref = refs[5 + n_par]
    (xo_ref, vfo_ref, wkv_ref, shift_ref, lru_ref, conv_ref) = refs[6 + n_par:12 + n_par]
    (s_scr, prev_scr, hist_scr, h_scr, egl_scr,
     rt_s, at_s, bt_s, kt_s, b2_s, k2_s, v_s, y_s) = refs[12 + n_par:25 + n_par]
    ti = pl.program_id(0)
    L = WKV_CHUNK
    R = L * nb

    @pl.when(ti == 0)
    def _():
        s_scr[...] = jnp.zeros_like(s_scr)
        prev_scr[...] = jnp.zeros_like(prev_scr)
        hist_scr[...] = jnp.zeros_like(hist_scr)
        h_scr[...] = jnp.zeros_like(h_scr)

    p = _load_params(p_refs)
    x = _to_time_major(x_ref, refs[25 + n_par], nb, L) if x_bt else x_ref[...]
    mod = mod_ref[...]
    proj = _dot(_modulate(_rms(x, ng_ref[...]), mod), win_ref[...])
    pr = proj[:, :RCOLS]
    prev = _shift_time(pr, prev_scr[...], nb, 1)
    prev_scr[...] = pr[R - nb:]
    r, lw, k, v, a, b, g = _rwkv_rows(pr, prev, vf_ref[...], p, has_vlora)
    vfo_ref[...] = jnp.zeros_like(vfo_ref) if has_vlora else v

    gc = _time_cumsum(lw, nb)
    g3 = gc.reshape(L, nb, RW)
    gl = g3[L - 1:L]
    egl = jnp.broadcast_to(jnp.exp(gl), (8, nb, RW)).reshape(8 * nb, RW)
    to_l = jnp.exp(gl - g3).reshape(R, RW)
    ieg = jnp.exp(-gc)
    ops = (r * jnp.exp(gc), a * jnp.exp(gc - lw), b * ieg, k * ieg, b * to_l, k * to_l, v)
    op_refs = (rt_s, at_s, bt_s, kt_s, b2_s, k2_s, v_s)
    for pi in range(PAIRS):
        for ref, val in zip(op_refs, ops):
            ref[pi] = val[:, pi * 128:(pi + 1) * 128]
        egl_scr[pi] = egl[:, pi * 128:(pi + 1) * 128]
    cst = _wkv_consts(L)

    def per_batch(it, carry):
        ids = [(it * WKV_BATCH + j, pi) for j in range(WKV_BATCH) for pi in range(PAIRS)]
        rows = lambda bi: pl.ds(bi, L, stride=nb)
        chains = [[ref[pi, rows(bi), :] for ref in op_refs]
                  + [s_scr[bi * PAIRS + pi], egl_scr[pi, pl.ds(bi, 8, stride=nb), :][0:1]]
                  for bi, pi in ids]
        for (bi, pi), (y, s_new) in zip(ids, _wkv_chunks(chains, cst)):
            y_s[pi, rows(bi), :] = y
            s_scr[bi * PAIRS + pi] = s_new
        return carry

    lax.fori_loop(0, nb // WKV_BATCH, per_batch, 0)
    y_all = jnp.concatenate([y_s[pi] for pi in range(PAIRS)], axis=1)
    y_rwkv = _rwkv_post(y_all, r, k, v, g, p)

    lx = proj[:, RCOLS:RCOLS + LRU]
    lg = proj[:, RCOLS + LRU:]
    xpad = jnp.concatenate([hist_scr[...], lx], axis=0)
    cw = p["conv_w"]
    xc = p["conv_b"] + xpad[0:R] * cw[0:1]
    for j in range(1, CONV_W):
        xc = xc + xpad[j * nb:j * nb + R] * cw[j:j + 1]
    hist_scr[...] = xpad[R:]
    a_t, b_t = _lru_rows(xc, p)
    a_cum, b_cum = _time_linear_scan(a_t, b_t, nb)
    hs = _per_batch(lambda a3, h0: a3 * h0, a_cum, h_scr[...]) + b_cum
    h_scr[...] = hs[R - nb:]
    y_lru = hs * _gelu(lg)

    out = _dot(y_rwkv, wout_ref[0:RW, :]) + _dot(y_lru, wout_ref[RW:, :])
    xo_ref[...] = _gated_residual(x, out, mod)

    @pl.when(ti == pl.num_programs(0) - 1)
    def _():
        wkv_ref[...] = s_scr[...]
        shift_ref[...] = prev_scr[...]
        lru_ref[...] = h_scr[...]
        conv_ref[...] = hist_scr[...]


def _even_layer_prompt(x, mod, norm_g, w_in_bf, vf, ep, has_vlora, w_out_bf, nb, t):
    L = WKV_CHUNK
    R = L * nb
    x_bt = x.ndim == 3
    x_spec = pl.BlockSpec((nb, L, D), lambda i: (0, i, 0)) if x_bt else _rows(R, D)
    pvals = [ep[n] for n in _EVEN_PARAM_NAMES]
    out_shapes = [
        jax.ShapeDtypeStruct((t * nb, D), F32),
        jax.ShapeDtypeStruct((t * nb, RW) if not has_vlora else (8, 128), F32),
        jax.ShapeDtypeStruct((nb * PAIRS, 128, 128), F32),
        jax.ShapeDtypeStruct((nb, RCOLS), F32),
        jax.ShapeDtypeStruct((nb, LRU), F32),
        jax.ShapeDtypeStruct(((CONV_W - 1) * nb, LRU), F32),
    ]
    out_specs = [_rows(R, D), _rows(R, RW) if not has_vlora else _full((8, 128)),
                 _full((nb * PAIRS, 128, 128)), _full((nb, RCOLS)), _full((nb, LRU)),
                 _full(((CONV_W - 1) * nb, LRU))]
    scratch = [pltpu.VMEM((nb * PAIRS, 128, 128), F32), pltpu.VMEM((nb, RCOLS), F32),
               pltpu.VMEM(((CONV_W - 1) * nb, LRU), F32), pltpu.VMEM((nb, LRU), F32),
               pltpu.VMEM((PAIRS, 8 * nb, 128), F32)]
    scratch += [pltpu.VMEM((PAIRS, R, 128), F32) for _ in range(8)]
    scratch += [pltpu.VMEM((D // 128, R, 128), F32)] if x_bt else []
    vf_spec = _rows(R, RW) if has_vlora else _full(vf.shape)
    return pl.pallas_call(
        functools.partial(_even_layer_kernel, has_vlora, x_bt, nb),
        grid=(t // L,),
        in_specs=[x_spec, _mod_spec(mod), _full((1, D)), _pick_spec(w_in_bf, True), vf_spec]
                 + [_full(v.shape) for v in pvals] + [_pick_spec(w_out_bf, True)],
        out_specs=out_specs,
        out_shape=out_shapes,
        scratch_shapes=scratch,
        compiler_params=_cparams(("arbitrary",)),
        name="even_layer_prompt",
    )(x, mod[0], norm_g, w_in_bf[0], vf, *pvals, w_out_bf[0])


def _in_proj_kernel(x_ref, mod_ref, g_ref, w_ref, o_ref):
    o_ref[...] = _dot(_modulate(_rms(x_ref[...], g_ref[...]), mod_ref[...]), w_ref[...])


def _in_proj(x, mod, norm_g, w_in_bf):
    rows = x.shape[0]
    return pl.pallas_call(
        _in_proj_kernel,
        grid=(1,),
        in_specs=[_full((rows, D)), _mod_spec(mod), _full((1, D)), _pick_spec(w_in_bf, True)],
        out_specs=_full((rows, IN_COLS)),
        out_shape=jax.ShapeDtypeStruct((rows, IN_COLS), F32),
        compiler_params=_cparams(("arbitrary",)),
        name="even_in_proj",
    )(x, mod[0], norm_g, w_in_bf[0])


def _even_pre_sample_kernel(has_vlora, *refs):
    n_par = len(_EVEN_PARAM_NAMES)
    proj_ref, shift_ref, c0_ref, c1_ref, c2_ref, h_ref, vf_ref = refs[:7]
    p = _load_params(refs[7:7 + n_par])
    (r_o, w_o, k_o, v_o, a_o, b_o, g_o, ylru_o, h_o) = refs[7 + n_par:]
    proj = proj_ref[...]
    pr = proj[:, :RCOLS]
    r, lw, k, v, a, b, g = _rwkv_rows(pr, shift_ref[...], vf_ref[...], p, has_vlora)
    r_o[...] = r
    w_o[...] = jnp.exp(lw)
    k_o[...] = k
    v_o[...] = v
    a_o[...] = a
    b_o[...] = b
    g_o[...] = g
    lx = proj[:, RCOLS:RCOLS + LRU]
    lg = proj[:, RCOLS + LRU:]
    cw = p["conv_w"]
    xc = (p["conv_b"] + c0_ref[...] * cw[0:1] + c1_ref[...] * cw[1:2] + c2_ref[...] * cw[2:3]
          + lx * cw[3:4])
    a_t, b_t = _lru_rows(xc, p)
    hs = a_t * h_ref[...] + b_t
    h_o[...] = hs
    ylru_o[...] = hs * _gelu(lg)


def _wkv_step_kernel(s_ref, r_ref, w_ref, k_ref, v_ref, a_ref, b_ref, so_ref, y_ref):
    G = s_ref.shape[0]
    N = HEAD_DIM
    ones = jnp.ones((N, N), BF16)
    eye = (lax.broadcasted_iota(jnp.int32, (N, N), 0)
           == lax.broadcasted_iota(jnp.int32, (N, N), 1)).astype(F32)
    bc = lambda ref: jnp.broadcast_to(ref[...], (G, N, N)).reshape(G * N, N)
    s = s_ref[...].reshape(G * N, N)
    sa = _dot_ones(s * bc(a_ref), ones)
    v_col = _dot_ones((eye[None] * v_ref[...]).reshape(G * N, N), ones)
    s_new = s * bc(w_ref) + sa * bc(b_ref) + v_col * bc(k_ref)
    so_ref[...] = s_new.reshape(G, N, N)
    y_bc = _dot_ones(s_new * bc(r_ref), ones).reshape(G, N, N)
    y_ref[...] = jnp.sum(y_bc * eye[None], axis=1, keepdims=True)


def _wkv_step(state, ops):
    stack, e = state
    bh = stack.shape[1]
    gb = 64
    s_in = pl.BlockSpec((None, gb, HEAD_DIM, HEAD_DIM), lambda i: (e, i, 0, 0))
    s_out = pl.BlockSpec((gb, HEAD_DIM, HEAD_DIM), lambda i: (i, 0, 0))
    o_spec = pl.BlockSpec((gb, 1, HEAD_DIM), lambda i: (i, 0, 0))
    return pl.pallas_call(
        _wkv_step_kernel,
        grid=(bh // gb,),
        in_specs=[s_in] + [o_spec] * 6,
        out_specs=(s_out, o_spec),
        out_shape=(jax.ShapeDtypeStruct(stack.shape[1:], F32),
                   jax.ShapeDtypeStruct((bh, 1, HEAD_DIM), F32)),
        compiler_params=_cparams(("parallel",)),
        name="wkv_step_sample",
    )(stack, *ops)


def _even_post_sample_kernel(x_ref, mod_ref, y_ref, r_ref, k_ref, v_ref, g_ref, ylru_ref,
                             rk_ref, lw_ref, lb_ref, hsum_ref, wout_ref, xo_ref):
    p = {"r_k": rk_ref[...], "lnx_w": lw_ref[...], "lnx_b": lb_ref[...], "hsum": hsum_ref[...]}
    y_rwkv = _rwkv_post(y_ref[...], r_ref[...], k_ref[...], v_ref[...], g_ref[...], p)
    out = _dot(y_rwkv, wout_ref[0:RW, :]) + _dot(ylru_ref[...], wout_ref[RW:, :])
    xo_ref[...] = _gated_residual(x_ref[...], out, mod_ref[...])


def _even_layer_sample(x, mod, norm_g, w_in_bf, vf, s_wkv, s_shift, s_lru, s_conv, ep, has_vlora,
                       w_out_bf):
    nb = x.shape[0]
    proj = _in_proj(x, mod, norm_g, w_in_bf)
    pvals = [ep[n] for n in _EVEN_PARAM_NAMES]
    rw = jax.ShapeDtypeStruct((nb, RW), F32)
    ins = [proj, s_shift, s_conv[:, 0], s_conv[:, 1], s_conv[:, 2], s_lru, vf]
    r, w, k, v, a, b, g, y_lru, h_new = pl.pallas_call(
        functools.partial(_even_pre_sample_kernel, has_vlora),
        in_specs=[_full(t.shape) for t in ins] + [_full(t.shape) for t in pvals],
        out_specs=tuple(_full((nb, RW)) for _ in range(9)),
        out_shape=(rw,) * 9,
        grid=(1,),
        compiler_params=_cparams(("arbitrary",)),
        name="even_pre_sample",
    )(*ins, *pvals)
    heads = lambda t: t.reshape(nb * HEADS, 1, HEAD_DIM)
    s_all = s_wkv[0].reshape(s_wkv[0].shape[0], nb * HEADS, HEAD_DIM, HEAD_DIM)
    s_new, y = _wkv_step((s_all, s_wkv[1]), [heads(t) for t in (r, w, k, v, a, b)])
    y = y.reshape(nb, RW)
    post_in = [x, mod[0], y, r, k, v, g, y_lru, ep["r_k"], ep["lnx_w"], ep["lnx_b"], ep["hsum"],
               w_out_bf[0]]
    x_new = pl.pallas_call(
        _even_post_sample_kernel,
        in_specs=[_full(x.shape), _mod_spec(mod)] + [_full(t.shape) for t in post_in[2:-1]]
                 + [_pick_spec(w_out_bf, True)],
        out_specs=_full((nb, D)),
        out_shape=jax.ShapeDtypeStruct((nb, D), F32),
        grid=(1,),
        compiler_params=_cparams(("arbitrary",)),
        name="even_post_sample",
    )(*post_in)
    lx = proj[:, RCOLS:RCOLS + LRU]
    conv_new = jnp.concatenate([s_conv[:, 1:], lx[:, None]], axis=1)
    return (x_new, v, s_new.reshape(nb, HEADS, HEAD_DIM, HEAD_DIM), proj[:, :RCOLS], h_new,
            conv_new)


def _s5_param_kernel(are_ref, aim_ref, ldt_ref, bre_ref, bim_ref, rep_ref,
                     abr_ref, abi_ref, bbr_ref, bbi_ref):
    a_re = are_ref[...]
    a_im = aim_ref[...]
    dt = jnp.exp(ldt_ref[...])
    mag = jnp.exp(a_re * dt)
    abr = mag * jnp.cos(a_im * dt)
    abi = mag * jnp.sin(a_im * dt)
    den = a_re * a_re + a_im * a_im
    nr = abr - 1.0
    cr = _dot_ones((nr * a_re + abi * a_im) / den, rep_ref[...], terms=3)
    ci = _dot_ones((abi * a_re - nr * a_im) / den, rep_ref[...], terms=3)
    b_re = bre_ref[...]
    b_im = bim_ref[...]
    abr_ref[...] = abr
    abi_ref[...] = abi
    bbr_ref[...] = cr * b_re - ci * b_im
    bbi_ref[...] = cr * b_im + ci * b_re


def _s5_params(a_re, a_im, log_dt, b_re, b_im, c_re, c_im):
    gp = S5_GROUPS * S5_STATE
    pc = S5_STATE * S5_GROUP
    rep = jnp.repeat(jnp.eye(S5_STATE, dtype=BF16), S5_GROUP, axis=1)
    ins = [a_re, a_im, log_dt.reshape(S5_GROUPS, 1), b_re.reshape(S5_GROUPS, pc),
           b_im.reshape(S5_GROUPS, pc), rep]
    gs = (S5_GROUPS, S5_STATE)
    abr, abi, bbr, bbi = pl.pallas_call(
        _s5_param_kernel,
        in_specs=[_full(t.shape) for t in ins],
        out_specs=(_full(gs), _full(gs), _full((S5_GROUPS, pc)), _full((S5_GROUPS, pc))),
        out_shape=(jax.ShapeDtypeStruct(gs, F32),) * 2
                  + (jax.ShapeDtypeStruct((S5_GROUPS, pc), F32),) * 2,
        grid=(1,),
        compiler_params=_cparams(("arbitrary",)),
        name="s5_discretise",
    )(*ins)
    gl = S5_GROUPS // S5_SUPER
    eye = jnp.eye(gl, dtype=F32)

    def b_bd(bb):
        t = bb.reshape(S5_SUPER, gl, S5_STATE, S5_GROUP)
        return jnp.einsum("sgpc,gh->sgchp", t, eye).reshape(S5_SUPER, gl * S5_GROUP, gl * S5_STATE)

    def c_bd(cc):
        t = cc.reshape(S5_SUPER, gl, S5_GROUP, S5_STATE)
        return jnp.einsum("sgcp,gh->sgphc", t, eye).reshape(S5_SUPER, gl * S5_STATE, gl * S5_GROUP)

    b_mat = jnp.concatenate([b_bd(bbr), b_bd(bbi)], axis=2).astype(BF16)
    c_mat = jnp.concatenate([c_bd(c_re), -c_bd(c_im)], axis=1).astype(BF16)
    return abr.reshape(1, gp), abi.reshape(1, gp), b_mat, c_mat


def _s5_kernel(bb, tc, x_ref, mod_ref, g_ref, abr_ref, abi_ref, bmat_ref, cmat_ref, dsk_ref,
               wglu_ref, bglu_ref, s0r_ref, s0i_ref, xo_ref, sr_ref, si_ref,
               xr_scr, xi_scr, str_scr, sti_scr):
    ti = pl.program_id(0)
    sw = S5_LANES // S5_SUPER

    @pl.when(ti == 0)
    def _():
        str_scr[...] = s0r_ref[...]
        sti_scr[...] = s0i_ref[...]

    x = x_ref[...]
    mod = mod_ref[...]
    u = _modulate(_rms(x, g_ref[...]), mod)
    cw = S5_GROUPS // S5_SUPER * S5_GROUP
    for sg in range(S5_SUPER):
        bu = _dot(u[:, sg * cw:(sg + 1) * cw], bmat_ref[sg])
        xr_scr[:, sg * sw:(sg + 1) * sw] = bu[:, :sw]
        xi_scr[:, sg * sw:(sg + 1) * sw] = bu[:, sw:]

    lw = 1024
    for rg in range(bb // 8):
        for lc in range(S5_LANES // lw):
            ln = slice(lc * lw, (lc + 1) * lw)
            ar = jnp.broadcast_to(abr_ref[:, ln], (8, lw))
            ai = jnp.broadcast_to(abi_ref[:, ln], (8, lw))
            sr = str_scr[rg * 8:(rg + 1) * 8, ln]
            si = sti_scr[rg * 8:(rg + 1) * 8, ln]
            for t in range(tc):
                rr = slice(t * bb + rg * 8, t * bb + rg * 8 + 8)
                sr, si = (ar * sr - ai * si + xr_scr[rr, ln], ar * si + ai * sr + xi_scr[rr, ln])
                xr_scr[rr, ln] = sr
                xi_scr[rr, ln] = si
            str_scr[rg * 8:(rg + 1) * 8, ln] = sr
            sti_scr[rg * 8:(rg + 1) * 8, ln] = si

    ys = []
    for sg in range(S5_SUPER):
        ys.append(_dot(xr_scr[:, sg * sw:(sg + 1) * sw], cmat_ref[sg, 0:sw, :])
                  + _dot(xi_scr[:, sg * sw:(sg + 1) * sw], cmat_ref[sg, sw:2 * sw, :]))
    yy = jnp.concatenate(ys, axis=1) + dsk_ref[...] * u
    gl = _dot(_gelu(yy), wglu_ref[...]) + bglu_ref[...]
    out = gl[:, :D] * _sigmoid(gl[:, D:])
    xo_ref[...] = _gated_residual(x, out, mod)

    @pl.when(ti == pl.num_programs(0) - 1)
    def _():
        sr_ref[...] = str_scr[...]
        si_ref[...] = sti_scr[...]


def _s5_layer(x, mod, norm_g, sp, d_skip, w_glu_bf, b_glu, s0r, s0i, bb, t):
    abr, abi, b_mat, c_mat = sp
    tc = min(S5_STEPS, t)
    rows = tc * bb
    ins = [x, mod[0], norm_g, abr, abi, b_mat, c_mat, d_skip, w_glu_bf[0], b_glu, s0r, s0i]
    in_specs = [_rows(rows, D), _mod_spec(mod)] + [_full(v.shape) for v in ins[2:]]
    in_specs[8] = _pick_spec(w_glu_bf, True)
    st = jax.ShapeDtypeStruct((bb, S5_LANES), F32)
    return pl.pallas_call(
        functools.partial(_s5_kernel, bb, tc),
        grid=(t // tc,),
        in_specs=in_specs,
        out_specs=(_rows(rows, D), _full((bb, S5_LANES)), _full((bb, S5_LANES))),
        out_shape=(jax.ShapeDtypeStruct((t * bb, D), F32), st, st),
        scratch_shapes=[pltpu.VMEM((rows, S5_LANES), F32), pltpu.VMEM((rows, S5_LANES), F32),
                        pltpu.VMEM((bb, S5_LANES), F32), pltpu.VMEM((bb, S5_LANES), F32)],
        compiler_params=_cparams(("arbitrary",)),
        name="s5_layer",
    )(*ins)


def _route(logits_t, rb):
    s = _sigmoid(logits_t)
    sel = s + rb
    rows = [sel[e:e + 1] for e in range(N_EXPERTS)]
    scores = []
    for gi in range(N_EGROUPS):
        m = rows[gi * EGROUP:(gi + 1) * EGROUP]
        best = None
        for i in range(EGROUP):
            for j in range(i + 1, EGROUP):
                pair = m[i] + m[j]
                best = pair if best is None else jnp.maximum(best, pair)
        scores.append(best)
    top = scores[0]
    grp = jnp.zeros_like(top, dtype=jnp.int32)
    for gi in range(1, N_EGROUPS):
        better = scores[gi] > top
        grp = jnp.where(better, gi, grp)
        top = jnp.where(better, scores[gi], top)
    picked = []
    for e in range(N_EXPERTS):
        gi = e // EGROUP
        rank = jnp.zeros_like(grp)
        for m in range(gi * EGROUP, (gi + 1) * EGROUP):
            if m == e:
                continue
            ahead = (rows[m] > rows[e]) if m > e else (rows[m] >= rows[e])
            rank = rank + ahead.astype(jnp.int32)
        chosen = (grp == gi) & (rank < 2)
        picked.append(jnp.where(chosen, s[e:e + 1], 0.0))
    w = jnp.concatenate(picked, axis=0)
    return w / jnp.sum(w, axis=0, keepdims=True)


def _moe_kernel(final, out_bt, x_ref, mod_ref, g_ref, rw_ref, rb_ref, wg_ref, wu_ref, wd_ref, fg_ref,
                o_ref, *scratch):
    x = x_ref[...]
    mod = mod_ref[...]
    h = _modulate(_rms(x, g_ref[...]), mod)
    rw = rw_ref[...]
    rw_hi = rw.astype(BF16)
    rw_lo = (rw - rw_hi.astype(F32)).astype(BF16)
    h_hi = h.astype(BF16)
    h_lo = (h - h_hi.astype(F32)).astype(BF16)
    logits_t = _dot_nt(rw_hi, h_hi) + (_dot_nt(rw_hi, h_lo) + _dot_nt(rw_lo, h_hi))
    gates_t = _route(logits_t[:N_EXPERTS], rb_ref[...])
    pad = jnp.zeros((ROUTER_ROWS - N_EXPERTS, gates_t.shape[1]), F32)
    gates = jnp.transpose(jnp.concatenate([gates_t, pad], axis=0))
    hb = h.astype(BF16)
    acc = jnp.zeros_like(x)
    for e in range(N_EXPERTS):
        hg = jnp.dot(hb, wg_ref[e], preferred_element_type=F32)
        hu = jnp.dot(hb, wu_ref[e], preferred_element_type=F32)
        act = _silu(hg) * hu * gates[:, e:e + 1]
        acc = acc + jnp.dot(act.astype(BF16), wd_ref[e], preferred_element_type=F32)
    xn = _gated_residual(x, acc, mod)
    if final:
        xn = _rms(xn, fg_ref[...])
    if out_bt:
        nb = mod.shape[0]
        _from_time_major(xn, o_ref, scratch[0], nb, x.shape[0] // nb)
    else:
        o_ref[...] = xn


def _moe_layer(x, mod, norm_g, router_wt, router_b, wg, wu, wd, final_g, final, out_bt):
    rows = x.shape[0]
    nb = mod[0].shape[1]
    tm = min(MOE_ROWS, rows)
    if out_bt:
        out_spec = pl.BlockSpec((nb, tm // nb, D), lambda i: (0, i, 0))
        out_shape = jax.ShapeDtypeStruct((nb, rows // nb, D), F32)
        scratch = [pltpu.VMEM((D // 128, tm, 128), F32)]
    else:
        out_spec, out_shape, scratch = _rows(tm, D), jax.ShapeDtypeStruct((rows, D), F32), []
    return pl.pallas_call(
        functools.partial(_moe_kernel, final, out_bt),
        grid=(rows // tm,),
        in_specs=[_rows(tm, D), _mod_spec(mod),
                  _full((1, D)), _full((ROUTER_ROWS, D)), _full((N_EXPERTS, 1)),
                  _pick_spec(wg, True), _pick_spec(wu, True), _pick_spec(wd, True), _full((1, D))],
        out_specs=out_spec,
        out_shape=out_shape,
        scratch_shapes=scratch,
        compiler_params=_cparams(("parallel",)),
        name="moe",
    )(x, mod[0], norm_g, router_wt, router_b, wg[0], wu[0], wd[0], final_g)


def _trunk(x3, mods, states, P, W):
    B, T, _ = x3.shape
    fresh = states is None
    assert fresh or T == 1, "a group that carries state is stepped one token at a time"
    depth = P["norm_g"].shape[0]
    x = x3 if (fresh and T > 1) else x3.reshape(B, D)
    outs = {k: [] for k in ("wkv", "shift", "lru", "conv", "s5r", "s5i")}
    v_first = jnp.zeros((8, 128), F32)
    for layer in range(depth):
        e = layer // 2
        ng = P["norm_g"][layer]
        mod_a, mod_m = (mods, 2 * layer), (mods, 2 * layer + 1)
        if layer % 2 == 0:
            ep, has_vlora = W["even"][e]
            if fresh:
                x, vf_new, wkv, sh, lr, cv = _even_layer_prompt(
                    x, mod_a, ng[0:1], (W["w_in"], e), v_first, ep, has_vlora, (W["w_out"], e), B, T)
                idx = jnp.arange(HEADS)
                wkv = wkv.reshape(B, PAIRS, 2, HEAD_DIM, 2, HEAD_DIM)
                wkv = wkv[:, idx // 2, idx % 2, :, idx % 2, :]
                wkv = jnp.moveaxis(wkv, 0, 1)
                cv = jnp.swapaxes(cv.reshape(CONV_W - 1, B, LRU), 0, 1)
            else:
                x, vf_new, wkv, sh, lr, cv = _even_layer_sample(
                    x, mod_a, ng[0:1], (W["w_in"], e), v_first, (states["wkv"], e), states["shift"][e],
                    states["lru"][e], states["conv"][e], ep, has_vlora, (W["w_out"], e))
            if not has_vlora:
                v_first = vf_new
            outs["wkv"].append(wkv); outs["shift"].append(sh); outs["lru"].append(lr)
            outs["conv"].append(cv)
        else:
            if fresh:
                s0r = jnp.zeros((B, S5_LANES), F32)
                s0i = s0r
            else:
                s0r = states["s5r"][e].reshape(B, S5_LANES)
                s0i = states["s5i"][e].reshape(B, S5_LANES)
            x, sr, si = _s5_layer(x, mod_a, ng[0:1], W["s5"][e], P["s5_d"][e].reshape(1, D),
                                  (W["w_glu"], e), P["s5_b_glu"][e].reshape(1, 2 * D), s0r, s0i, B, T)
            outs["s5r"].append(sr.reshape(B, S5_GROUPS, S5_STATE))
            outs["s5i"].append(si.reshape(B, S5_GROUPS, S5_STATE))
        last = layer == depth - 1
        x = _moe_layer(x, mod_m, ng[1:2], W["router_wt"], W["router_b"], (W["wg"], layer),
                       (W["wu"], layer), (W["wd"], layer), P["final_norm_g"].reshape(1, D), last,
                       last and x3.shape[1] > 1)
    y = x.reshape(B, T, D)
    return (y, jnp.stack(outs["wkv"]), jnp.stack(outs["shift"]), jnp.stack(outs["lru"]),
            jnp.stack(outs["conv"]), jnp.stack(outs["s5r"]), jnp.stack(outs["s5i"]))


def kernel(x_prompt, x_sample, state_wkv, state_shift, state_lru, state_conv, state_s5_re, state_s5_im, c_prompt, c_sample, norm_g, ada_w, ada_b, final_norm_g, even_w_in, rwkv_mu, rwkv_w0, rwkv_w2, rwkv_a0, rwkv_a2, rwkv_g2, rwkv_k_k, rwkv_k_a, rwkv_r_k, rwkv_lnx_w, rwkv_lnx_b, rwkv_v0, rwkv_v1, rwkv_v2, lru_conv_w, lru_conv_b, lru_wa, lru_ba, lru_wx, lru_bx, lru_lam, even_w_out, s5_a_re, s5_a_im, s5_log_dt, s5_b_re, s5_b_im, s5_c_re, s5_c_im, s5_d, s5_w_glu, s5_b_glu, router_w, router_b, moe_w_gate, moe_w_up, moe_w_down):
    P = dict(norm_g=norm_g, final_norm_g=final_norm_g, rwkv_mu=rwkv_mu, rwkv_w0=rwkv_w0,
             rwkv_w2=rwkv_w2, rwkv_a0=rwkv_a0, rwkv_a2=rwkv_a2, rwkv_g2=rwkv_g2, rwkv_k_k=rwkv_k_k,
             rwkv_k_a=rwkv_k_a, rwkv_r_k=rwkv_r_k.reshape(rwkv_r_k.shape[0], RW),
             rwkv_lnx_w=rwkv_lnx_w, rwkv_lnx_b=rwkv_lnx_b, rwkv_v0=rwkv_v0, rwkv_v1=rwkv_v1,
             rwkv_v2=rwkv_v2, lru_conv_w=lru_conv_w, lru_conv_b=lru_conv_b, lru_wa=lru_wa,
             lru_ba=lru_ba.reshape(lru_ba.shape[0], LRU), lru_wx=lru_wx,
             lru_bx=lru_bx.reshape(lru_bx.shape[0], LRU), lru_lam=lru_lam.reshape(lru_lam.shape[0], LRU),
             s5_d=s5_d, s5_b_glu=s5_b_glu)
    n_even, n_odd = even_w_in.shape[0], s5_w_glu.shape[0]
    W = dict(
        even=[_even_params(e, P) for e in range(n_even)],
        w_in=even_w_in.astype(BF16), w_out=even_w_out.astype(BF16), w_glu=s5_w_glu.astype(BF16),
        s5=[_s5_params(s5_a_re[e], s5_a_im[e], s5_log_dt[e], s5_b_re[e], s5_b_im[e], s5_c_re[e],
                       s5_c_im[e]) for e in range(n_odd)],
        router_wt=jnp.pad(router_w.T, ((0, ROUTER_ROWS - N_EXPERTS), (0, 0))),
        router_b=router_b.reshape(N_EXPERTS, 1),
        wg=moe_w_gate.astype(BF16), wu=moe_w_up.astype(BF16), wd=moe_w_down.astype(BF16),
    )
    bp = x_prompt.shape[0]
    mods_p, mods_s = _ada_all(jnp.concatenate([c_prompt, c_sample], axis=0), bp, ada_w, ada_b)
    out_p = _trunk(x_prompt, mods_p, None, P, W)
    st = dict(wkv=state_wkv, shift=state_shift, lru=state_lru, conv=state_conv,
              s5r=state_s5_re, s5i=state_s5_im)
    out_s = _trunk(x_sample, mods_s, st, P, W)
    return (out_p[0], out_s[0]) + tuple(out_p[1:]) + tuple(out_s[1:])
```

```python
import functools
import math

import jax
import jax.numpy as jnp
from jax import lax
from jax.experimental import pallas as pl
from jax.experimental.pallas import tpu as pltpu

F32 = jnp.float32
BF16 = jnp.bfloat16

D = 1024
HEADS = 8
HEAD_DIM = 64
PAIRS = HEADS // 2
RW = HEADS * HEAD_DIM
OFF_W = 3 * RW
OFF_A = OFF_W + 64
OFF_G = OFF_A + 64
RCOLS = OFF_G + 128
LRU = 512
CONV_W = 4
IN_COLS = RCOLS + 2 * LRU
S5_GROUP = 16
S5_GROUPS = 64
S5_STATE = 64
S5_LANES = S5_GROUPS * S5_STATE
S5_SUPER = 4
N_EXPERTS = 16
N_EGROUPS = 4
EGROUP = 4
D_EXPERT = 256
ROUTER_ROWS = 128
RMS_EPS = 1e-6
LNX_EPS = 64e-5
LRU_C = 8.0

WKV_CHUNK = 64
WKV_BATCH = 4
S5_STEPS = 32
MOE_ROWS = 512
VMEM_LIMIT = 56 * 1024 * 1024


def _cparams(sem):
    return pltpu.CompilerParams(dimension_semantics=sem, vmem_limit_bytes=VMEM_LIMIT)


def _dot(a, b):
    return jnp.dot(a.astype(BF16), b.astype(BF16), preferred_element_type=F32)


def _dot_hi(a, b):
    return jnp.dot(a, b, precision=lax.Precision.HIGHEST, preferred_element_type=F32)


def _dot_nt(a, b):
    return lax.dot_general(a.astype(BF16), b.astype(BF16), (((1,), (1,)), ((), ())),
                           preferred_element_type=F32)


def _dot_tn(a, b):
    return lax.dot_general(a.astype(BF16), b.astype(BF16), (((0,), (0,)), ((), ())),
                           preferred_element_type=F32)


_dot_solve = _dot


def _dot_ones(x, ones_bf, terms=2):
    out = None
    for _ in range(terms):
        part = x.astype(BF16)
        x = x - part.astype(F32)
        d = jnp.dot(part, ones_bf, preferred_element_type=F32)
        out = d if out is None else out + d
    return out


def _sigmoid(x):
    return 1.0 / (1.0 + jnp.exp(-x))


def _silu(x):
    return x * _sigmoid(x)


def _softplus(x):
    return jnp.maximum(x, 0.0) + jnp.log1p(jnp.exp(-jnp.abs(x)))


def _gelu(x):
    c = math.sqrt(2.0 / math.pi)
    return 0.5 * x * (1.0 + jnp.tanh(c * (x + 0.044715 * (x * x * x))))


def _rms(x, g):
    ms = jnp.mean(x * x, axis=-1, keepdims=True)
    return x * lax.rsqrt(ms + RMS_EPS) * g


def _per_batch(fn, y, *ms):
    nb = ms[0].shape[0]
    rows, w = y.shape
    return fn(y.reshape(rows // nb, nb, w), *[m[None] for m in ms]).reshape(rows, w)


def _modulate(y, mod):
    return _per_batch(lambda y3, sc, sh: y3 * (1.0 + sc) + sh, y, mod[:, D:2 * D], mod[:, :D])


def _gated_residual(x, out, mod):
    return x + _per_batch(lambda o3, gt: o3 * gt, out, mod[:, 2 * D:])


def _full(shape):
    n = len(shape)
    return pl.BlockSpec(shape, lambda *_: (0,) * n)


def _resident(shape):
    n = len(shape)
    return pl.BlockSpec(shape, lambda *_: (0,) * n, pipeline_mode=pl.Buffered(1))


def _rows(tm, width):
    return pl.BlockSpec((tm, width), lambda i: (i, 0))


def _pick_spec(sel, single_buffer=False):
    stack, i = sel
    tail = (0,) * (stack.ndim - 1)
    kw = dict(pipeline_mode=pl.Buffered(1)) if single_buffer else {}
    return pl.BlockSpec((None,) + stack.shape[1:], lambda *_: (i,) + tail, **kw)


def _mod_spec(mod):
    return _pick_spec(mod)


def _ada_kernel(n_first, c_ref, w_ref, b_ref, o1_ref, o2_ref):
    m = _dot(_silu(c_ref[...]), w_ref[0]) + b_ref[0]
    o1_ref[0] = m[:n_first]
    o2_ref[0] = m[n_first:]


def _ada_all(c_all, n_first, ada_w, ada_b):
    n_sub = ada_w.shape[0] * ada_w.shape[1]
    w = ada_w.reshape(n_sub, D, 3 * D)
    b = ada_b.reshape(n_sub, 1, 3 * D)
    rows = c_all.shape[0]
    tn = 768
    out = lambda r: (pl.BlockSpec((1, r, tn), lambda s, j: (s, 0, j)),
                     jax.ShapeDtypeStruct((n_sub, r, 3 * D), F32))
    (spec1, shape1), (spec2, shape2) = out(n_first), out(rows - n_first)
    return pl.pallas_call(
        functools.partial(_ada_kernel, n_first),
        grid=(n_sub, 3 * D // tn),
        in_specs=[pl.BlockSpec((rows, D), lambda s, j: (0, 0)),
                  pl.BlockSpec((1, D, tn), lambda s, j: (s, 0, j)),
                  pl.BlockSpec((1, 1, tn), lambda s, j: (s, 0, j))],
        out_specs=(spec1, spec2),
        out_shape=(shape1, shape2),
        compiler_params=_cparams(("parallel", "parallel")),
        name="ada_ln",
    )(c_all, w, b)


def _rwkv_rows(pr, prev, vf, p, has_vlora):
    ps = pr + (prev - pr) * p["mu"]
    r = ps[:, :RW]
    k = ps[:, RW:2 * RW]
    v = ps[:, 2 * RW:OFF_W]
    wd = ps[:, OFF_W:OFF_A]
    ad = ps[:, OFF_A:OFF_G]
    gd = ps[:, OFF_G:RCOLS]
    lw = -math.exp(-0.5) * _sigmoid(p["w0"] + _dot(jnp.tanh(wd), p["w2"]))
    if has_vlora:
        v = v + (vf - v) * _sigmoid(p["v0"] + _dot(_dot(v, p["v1"]), p["v2"]))
    a = _sigmoid(p["a0"] + _dot(ad, p["a2"]))
    g = _dot(_sigmoid(gd), p["g2"])
    kk = k * p["k_k"]
    ss = _dot(kk * kk, p["hsum"])
    kk = kk * lax.rsqrt(jnp.maximum(ss, 1e-24))
    k = k * (1.0 + (a - 1.0) * p["k_a"])
    return r, lw, k, v, -kk, kk * a, g


def _rwkv_post(y, r, k, v, g, p):
    hs = p["hsum"]
    mean = _dot(y, hs) * (1.0 / HEAD_DIM)
    d = y - mean
    var = _dot(d * d, hs) * (1.0 / HEAD_DIM)
    yn = d * lax.rsqrt(var + LNX_EPS) * p["lnx_w"] + p["lnx_b"]
    bonus = _dot(r * k * p["r_k"], hs) * v
    return (yn + bonus) * g


def _lru_rows(xc, p):
    gate_r = _sigmoid(_dot(xc, p["wa"]) + p["ba"])
    gate_i = _sigmoid(_dot(xc, p["wx"]) + p["bx"])
    log_a = -LRU_C * gate_r * _softplus(-p["lam"])
    a_t = jnp.exp(log_a)
    b_t = jnp.sqrt(1.0 - jnp.exp(2.0 * log_a)) * gate_i * xc
    return a_t, b_t


_EVEN_PARAM_NAMES = ("mu", "w0", "w2", "a0", "a2", "g2", "k_k", "k_a", "r_k", "lnx_w", "lnx_b",
                     "v0", "v1", "v2", "conv_w", "conv_b", "wa", "ba", "wx", "bx", "lam", "hsum")


def _load_params(refs):
    return {n: r[...] for n, r in zip(_EVEN_PARAM_NAMES, refs)}


def _even_params(P):
    n_even = P["rwkv_mu"].shape[0]
    eye8 = jnp.eye(HEADS, dtype=F32)

    def bdiag(w):
        return jnp.einsum("lncd,nm->lncmd", w, eye8).reshape(w.shape[0], LRU, LRU).astype(BF16)

    rows = lambda v: v.reshape(v.shape[0], 1, -1)
    stacks = {
        "mu": rows(P["rwkv_mu"]), "w0": rows(P["rwkv_w0"]), "w2": P["rwkv_w2"],
        "a0": rows(P["rwkv_a0"]), "a2": P["rwkv_a2"], "g2": P["rwkv_g2"],
        "k_k": rows(P["rwkv_k_k"]), "k_a": rows(P["rwkv_k_a"]), "r_k": rows(P["rwkv_r_k"]),
        "lnx_w": rows(P["rwkv_lnx_w"]), "lnx_b": rows(P["rwkv_lnx_b"]),
        "v0": rows(P["rwkv_v0"]), "v1": P["rwkv_v1"], "v2": P["rwkv_v2"],
        "conv_w": P["lru_conv_w"], "conv_b": rows(P["lru_conv_b"]),
        "wa": bdiag(P["lru_wa"]), "ba": rows(P["lru_ba"]),
        "wx": bdiag(P["lru_wx"]), "bx": rows(P["lru_bx"]),
        "lam": rows(P["lru_lam"]),
        "hsum": jnp.kron(eye8, jnp.ones((HEAD_DIM, HEAD_DIM), F32)).astype(BF16)[None],
    }
    vlora = ("v0", "v1", "v2")
    shared = ("hsum",)
    out = []
    for e in range(n_even):
        idx = lambda n: 0 if n in shared else (max(e - 1, 0) if n in vlora else e)
        out.append(({n: (s, idx(n)) for n, s in stacks.items()}, e > 0))
    return out


def _wkv_consts(L):
    L2 = 2 * L
    ri = lax.broadcasted_iota(jnp.int32, (L2, L2), 0)
    ci = lax.broadcasted_iota(jnp.int32, (L2, L2), 1)
    same = (ri >= L) == (ci >= L)
    lane = lax.broadcasted_iota(jnp.int32, (1, 2 * HEAD_DIM), 1)
    return {
        "smask": same & (ri > ci),
        "imask": same & (ri >= ci),
        "eye": (ri == ci).astype(F32),
        "m0": lane < HEAD_DIM,
    }


def _wkv_chunks(chains, cst):
    L = chains[0][0].shape[0]
    L2 = 2 * L
    m0 = cst["m0"]
    each = lambda fn, *cols: [fn(*xs) for xs in zip(*cols)]

    def stk(x):
        return jnp.concatenate([jnp.where(m0, x, 0.0), jnp.where(m0, 0.0, x)], axis=0)

    rt, at, bt, kt, b2, k2, v, s, egl = [list(c) for c in zip(*chains)]
    a4 = each(lambda a, r: jnp.concatenate([stk(a), stk(r)], axis=0), at, rt)
    b4 = each(lambda b, k: jnp.concatenate([stk(b), stk(k)], axis=0), bt, kt)
    pm = each(_dot_nt, a4, b4)
    ah = each(_dot_nt, a4, s)
    n_ab = [jnp.where(cst["smask"], x[:L2, :L2], 0.0) for x in pm]
    a_ak = [jnp.where(cst["smask"], x[:L2, L2:], 0.0) for x in pm]
    a_r = [jnp.concatenate([jnp.where(cst["imask"], x[L2:, :L2], 0.0),
                            jnp.where(cst["imask"], x[L2:, L2:], 0.0)], axis=1) for x in pm]
    v2 = [stk(x) for x in v]
    rhs = each(lambda h, m, w: h[:L2] + _dot(m, w), ah, a_ak, v2)
    t_inv = [cst["eye"] + n for n in n_ab]
    pk = n_ab
    for _ in range(int(math.log2(L)) - 1):
        pk = each(_dot_solve, pk, pk)
        t_inv = each(lambda t, q: t + _dot_solve(t, q), t_inv, pk)
    u2 = each(_dot_solve, t_inv, rhs)
    uv = each(lambda u, w: jnp.concatenate([u, w], axis=0), u2, v2)
    y2 = each(lambda h, m, w: h[L2:] + _dot(m, w), ah, a_r, uv)
    ys = [x[:L] + x[L:] for x in y2]
    bk = each(lambda b, k: jnp.concatenate([stk(b), stk(k)], axis=0), b2, k2)
    s_new = each(lambda s0, e, w, q: s0 * e + _dot_tn(w, q), s, egl, uv, bk)
    return list(zip(ys, s_new))


def _shift_time(x, fill, nb, d):
    return jnp.concatenate([fill, x[:x.shape[0] - d * nb]], axis=0)


def _time_cumsum(x, nb):
    d = nb
    while d < x.shape[0]:
        x = x + _shift_time(x, jnp.zeros((d, x.shape[1]), F32), 1, d)
        d *= 2
    return x


def _time_linear_scan(a, b, nb):
    d = nb
    while d < a.shape[0]:
        a_sh = _shift_time(a, jnp.ones((d, a.shape[1]), F32), 1, d)
        b_sh = _shift_time(b, jnp.zeros((d, b.shape[1]), F32), 1, d)
        b = a * b_sh + b
        a = a * a_sh
        d *= 2
    return a, b


def _to_time_major(x_ref, scr, nb, steps):
    planes = x_ref.shape[2] // 128
    for c in range(planes):
        for bi in range(nb):
            scr[c, pl.ds(bi, steps, stride=nb), :] = x_ref[bi, :, c * 128:(c + 1) * 128]
    return jnp.concatenate([scr[c] for c in range(planes)], axis=1)


def _from_time_major(x, o_ref, scr, nb, steps):
    planes = x.shape[1] // 128
    for c in range(planes):
        scr[c] = x[:, c * 128:(c + 1) * 128]
    for c in range(planes):
        for bi in range(nb):
            o_ref[bi, :, c * 128:(c + 1) * 128] = scr[c, pl.ds(bi, steps, stride=nb), :]


def _even_layer_kernel(has_vlora, x_bt, nb, *refs):
    n_par = len(_EVEN_PARAM_NAMES)
    x_ref, mod_ref, ng_ref, win_ref, vf_ref = refs[:5]
    p_refs = refs[5:5 + n_par]
    wout_ref = refs[5 + n_par]
    (xo_ref, vfo_ref, wkv_ref, shift_ref, lru_ref, conv_ref) = refs[6 + n_par:12 + n_par]
    (s_scr, prev_scr, hist_scr, h_scr, egl_scr,
     rt_s, at_s, bt_s, kt_s, b2_s, k2_s, v_s, y_s) = refs[12 + n_par:25 + n_par]
    ti = pl.program_id(0)
    L = WKV_CHUNK
    R = L * nb

    @pl.when(ti == 0)
    def _():
        s_scr[...] = jnp.zeros_like(s_scr)
        prev_scr[...] = jnp.zeros_like(prev_scr)
        hist_scr[...] = jnp.zeros_like(hist_scr)
        h_scr[...] = jnp.zeros_like(h_scr)

    p = _load_params(p_refs)
    x = _to_time_major(x_ref, refs[25 + n_par], nb, L) if x_bt else x_ref[...]
    mod = mod_ref[...]
    proj = _dot(_modulate(_rms(x, ng_ref[...]), mod), win_ref[...])
    pr = proj[:, :RCOLS]
    prev = _shift_time(pr, prev_scr[...], nb, 1)
    prev_scr[...] = pr[R - nb:]
    r, lw, k, v, a, b, g = _rwkv_rows(pr, prev, vf_ref[...], p, has_vlora)
    vfo_ref[...] = jnp.zeros_like(vfo_ref) if has_vlora else v

    gc = _time_cumsum(lw, nb)
    g3 = gc.reshape(L, nb, RW)
    gl = g3[L - 1:L]
    egl = jnp.broadcast_to(jnp.exp(gl), (8, nb, RW)).reshape(8 * nb, RW)
    to_l = jnp.exp(gl - g3).reshape(R, RW)
    ieg = jnp.exp(-gc)
    ops = (r * jnp.exp(gc), a * jnp.exp(gc - lw), b * ieg, k * ieg, b * to_l, k * to_l, v)
    op_refs = (rt_s, at_s, bt_s, kt_s, b2_s, k2_s, v_s)
    for pi in range(PAIRS):
        for ref, val in zip(op_refs, ops):
            ref[pi] = val[:, pi * 128:(pi + 1) * 128]
        egl_scr[pi] = egl[:, pi * 128:(pi + 1) * 128]
    cst = _wkv_consts(L)

    def per_batch(it, carry):
        ids = [(it * WKV_BATCH + j, pi) for j in range(WKV_BATCH) for pi in range(PAIRS)]
        rows = lambda bi: pl.ds(bi, L, stride=nb)
        chains = [[ref[pi, rows(bi), :] for ref in op_refs]
                  + [s_scr[bi * PAIRS + pi], egl_scr[pi, pl.ds(bi, 8, stride=nb), :][0:1]]
                  for bi, pi in ids]
        for (bi, pi), (y, s_new) in zip(ids, _wkv_chunks(chains, cst)):
            y_s[pi, rows(bi), :] = y
            s_scr[bi * PAIRS + pi] = s_new
        return carry

    lax.fori_loop(0, nb // WKV_BATCH, per_batch, 0)
    y_all = jnp.concatenate([y_s[pi] for pi in range(PAIRS)], axis=1)
    y_rwkv = _rwkv_post(y_all, r, k, v, g, p)

    lx = proj[:, RCOLS:RCOLS + LRU]
    lg = proj[:, RCOLS + LRU:]
    xpad = jnp.concatenate([hist_scr[...], lx], axis=0)
    cw = p["conv_w"]
    xc = p["conv_b"] + xpad[0:R] * cw[0:1]
    for j in range(1, CONV_W):
        xc = xc + xpad[j * nb:j * nb + R] * cw[j:j + 1]
    hist_scr[...] = xpad[R:]
    a_t, b_t = _lru_rows(xc, p)
    a_cum, b_cum = _time_linear_scan(a_t, b_t, nb)
    hs = _per_batch(lambda a3, h0: a3 * h0, a_cum, h_scr[...]) + b_cum
    h_scr[...] = hs[R - nb:]
    y_lru = hs * _gelu(lg)

    out = _dot(y_rwkv, wout_ref[0:RW, :]) + _dot(y_lru, wout_ref[RW:, :])
    xo_ref[...] = _gated_residual(x, out, mod)

    @pl.when(ti == pl.num_programs(0) - 1)
    def _():
        wkv_ref[...] = s_scr[...]
        shift_ref[...] = prev_scr[...]
        lru_ref[...] = h_scr[...]
        conv_ref[...] = hist_scr[...]


def _even_layer_prompt(x, mod, norm_g, w_in_bf, vf, ep, has_vlora, w_out_bf, nb, t):
    L = WKV_CHUNK
    R = L * nb
    x_bt = x.ndim == 3
    x_spec = pl.BlockSpec((nb, L, D), lambda i: (0, i, 0)) if x_bt else _rows(R, D)
    pvals = [ep[n] for n in _EVEN_PARAM_NAMES]
    out_shapes = [
        jax.ShapeDtypeStruct((t * nb, D), F32),
        jax.ShapeDtypeStruct((t * nb, RW) if not has_vlora else (8, 128), F32),
        jax.ShapeDtypeStruct((nb * PAIRS, 128, 128), F32),
        jax.ShapeDtypeStruct((nb, RCOLS), F32),
        jax.ShapeDtypeStruct((nb, LRU), F32),
        jax.ShapeDtypeStruct(((CONV_W - 1) * nb, LRU), F32),
    ]
    out_specs = [_rows(R, D), _rows(R, RW) if not has_vlora else _full((8, 128)),
                 _full((nb * PAIRS, 128, 128)), _full((nb, RCOLS)), _full((nb, LRU)),
                 _full(((CONV_W - 1) * nb, LRU))]
    scratch = [pltpu.VMEM((nb * PAIRS, 128, 128), F32), pltpu.VMEM((nb, RCOLS), F32),
               pltpu.VMEM(((CONV_W - 1) * nb, LRU), F32), pltpu.VMEM((nb, LRU), F32),
               pltpu.VMEM((PAIRS, 8 * nb, 128), F32)]
    scratch += [pltpu.VMEM((PAIRS, R, 128), F32) for _ in range(8)]
    scratch += [pltpu.VMEM((D // 128, R, 128), F32)] if x_bt else []
    vf_spec = _rows(R, RW) if has_vlora else _full(vf.shape)
    return pl.pallas_call(
        functools.partial(_even_layer_kernel, has_vlora, x_bt, nb),
        grid=(t // L,),
        in_specs=[x_spec, _mod_spec(mod), _full((1, D)), _pick_spec(w_in_bf, True), vf_spec]
                 + [_pick_spec(v) for v in pvals] + [_pick_spec(w_out_bf, True)],
        out_specs=out_specs,
        out_shape=out_shapes,
        scratch_shapes=scratch,
        compiler_params=_cparams(("arbitrary",)),
        name="even_layer_prompt",
    )(x, mod[0], norm_g, w_in_bf[0], vf, *[v[0] for v in pvals], w_out_bf[0])


def _in_proj_kernel(x_ref, mod_ref, g_ref, w_ref, o_ref):
    o_ref[...] = _dot(_modulate(_rms(x_ref[...], g_ref[...]), mod_ref[...]), w_ref[...])


def _in_proj(x, mod, norm_g, w_in_bf):
    rows = x.shape[0]
    return pl.pallas_call(
        _in_proj_kernel,
        grid=(1,),
        in_specs=[_full((rows, D)), _mod_spec(mod), _full((1, D)), _pick_spec(w_in_bf, True)],
        out_specs=_full((rows, IN_COLS)),
        out_shape=jax.ShapeDtypeStruct((rows, IN_COLS), F32),
        compiler_params=_cparams(("arbitrary",)),
        name="even_in_proj",
    )(x, mod[0], norm_g, w_in_bf[0])


def _even_pre_sample_kernel(has_vlora, *refs):
    n_par = len(_EVEN_PARAM_NAMES)
    proj_ref, shift_ref, c0_ref, c1_ref, c2_ref, h_ref, vf_ref = refs[:7]
    p = _load_params(refs[7:7 + n_par])
    (r_o, w_o, k_o, v_o, a_o, b_o, g_o, ylru_o, h_o) = refs[7 + n_par:]
    proj = proj_ref[...]
    pr = proj[:, :RCOLS]
    r, lw, k, v, a, b, g = _rwkv_rows(pr, shift_ref[...], vf_ref[...], p, has_vlora)
    r_o[...] = r
    w_o[...] = jnp.exp(lw)
    k_o[...] = k
    v_o[...] = v
    a_o[...] = a
    b_o[...] = b
    g_o[...] = g
    lx = proj[:, RCOLS:RCOLS + LRU]
    lg = proj[:, RCOLS + LRU:]
    cw = p["conv_w"]
    xc = (p["conv_b"] + c0_ref[...] * cw[0:1] + c1_ref[...] * cw[1:2] + c2_ref[...] * cw[2:3]
          + lx * cw[3:4])
    a_t, b_t = _lru_rows(xc, p)
    hs = a_t * h_ref[...] + b_t
    h_o[...] = hs
    ylru_o[...] = hs * _gelu(lg)


def _wkv_step_kernel(s_ref, r_ref, w_ref, k_ref, v_ref, a_ref, b_ref, so_ref, y_ref):
    G = s_ref.shape[0] * s_ref.shape[1]
    N = HEAD_DIM
    ones = jnp.ones((N, N), BF16)
    eye = (lax.broadcasted_iota(jnp.int32, (N, N), 0)
           == lax.broadcasted_iota(jnp.int32, (N, N), 1)).astype(F32)
    bc = lambda ref: jnp.broadcast_to(ref[...], (G, N, N)).reshape(G * N, N)
    s = s_ref[...].reshape(G * N, N)
    sa = _dot_ones(s * bc(a_ref), ones)
    v_col = _dot_ones((eye[None] * v_ref[...]).reshape(G * N, N), ones)
    s_new = s * bc(w_ref) + sa * bc(b_ref) + v_col * bc(k_ref)
    so_ref[...] = s_new.reshape(so_ref.shape)
    y_bc = _dot_ones(s_new * bc(r_ref), ones).reshape(G, N, N)
    y_ref[...] = jnp.sum(y_bc * eye[None], axis=1, keepdims=True)


def _wkv_step(state, ops):
    stack, e = state
    bh = stack.shape[1] * HEADS
    bb = 8
    gb = bb * HEADS
    s_in = pl.BlockSpec((None, bb, HEADS, HEAD_DIM, HEAD_DIM), lambda i: (e, i, 0, 0, 0))
    s_out = pl.BlockSpec((bb, HEADS, HEAD_DIM, HEAD_DIM), lambda i: (i, 0, 0, 0))
    o_spec = pl.BlockSpec((gb, 1, HEAD_DIM), lambda i: (i, 0, 0))
    return pl.pallas_call(
        _wkv_step_kernel,
        grid=(bh // gb,),
        in_specs=[s_in] + [o_spec] * 6,
        out_specs=(s_out, o_spec),
        out_shape=(jax.ShapeDtypeStruct(stack.shape[1:], F32),
                   jax.ShapeDtypeStruct((bh, 1, HEAD_DIM), F32)),
        compiler_params=_cparams(("parallel",)),
        name="wkv_step_sample",
    )(stack, *ops)


def _even_post_sample_kernel(x_ref, mod_ref, y_ref, r_ref, k_ref, v_ref, g_ref, ylru_ref,
                             rk_ref, lw_ref, lb_ref, hsum_ref, wout_ref, xo_ref):
    p = {"r_k": rk_ref[...], "lnx_w": lw_ref[...], "lnx_b": lb_ref[...], "hsum": hsum_ref[...]}
    y_rwkv = _rwkv_post(y_ref[...], r_ref[...], k_ref[...], v_ref[...], g_ref[...], p)
    out = _dot(y_rwkv, wout_ref[0:RW, :]) + _dot(ylru_ref[...], wout_ref[RW:, :])
    xo_ref[...] = _gated_residual(x_ref[...], out, mod_ref[...])


def _even_layer_sample(x, mod, norm_g, w_in_bf, vf, s_wkv, s_shift, s_lru, s_conv, ep, has_vlora,
                       w_out_bf):
    nb = x.shape[0]
    proj = _in_proj(x, mod, norm_g, w_in_bf)
    pvals = [ep[n] for n in _EVEN_PARAM_NAMES]
    rw = jax.ShapeDtypeStruct((nb, RW), F32)
    ins = [proj, s_shift, s_conv[:, 0], s_conv[:, 1], s_conv[:, 2], s_lru, vf]
    r, w, k, v, a, b, g, y_lru, h_new = pl.pallas_call(
        functools.partial(_even_pre_sample_kernel, has_vlora),
        in_specs=[_full(t.shape) for t in ins] + [_pick_spec(t) for t in pvals],
        out_specs=tuple(_full((nb, RW)) for _ in range(9)),
        out_shape=(rw,) * 9,
        grid=(1,),
        compiler_params=_cparams(("arbitrary",)),
        name="even_pre_sample",
    )(*ins, *[t[0] for t in pvals])
    heads = lambda t: t.reshape(nb * HEADS, 1, HEAD_DIM)
    s_new, y = _wkv_step(s_wkv, [heads(t) for t in (r, w, k, v, a, b)])
    y = y.reshape(nb, RW)
    acts = [y, r, k, v, g, y_lru]
    picks = [ep["r_k"], ep["lnx_w"], ep["lnx_b"], ep["hsum"]]
    x_new = pl.pallas_call(
        _even_post_sample_kernel,
        in_specs=[_full(x.shape), _mod_spec(mod)] + [_full(t.shape) for t in acts]
                 + [_pick_spec(t) for t in picks] + [_pick_spec(w_out_bf, True)],
        out_specs=_full((nb, D)),
        out_shape=jax.ShapeDtypeStruct((nb, D), F32),
        grid=(1,),
        compiler_params=_cparams(("arbitrary",)),
        name="even_post_sample",
    )(x, mod[0], *acts, *[t[0] for t in picks], w_out_bf[0])
    lx = proj[:, RCOLS:RCOLS + LRU]
    conv_new = jnp.concatenate([s_conv[:, 1:], lx[:, None]], axis=1)
    return (x_new, v, s_new, proj[:, :RCOLS], h_new,
            conv_new)


def _s5_param_kernel(are_ref, aim_ref, ldt_ref, bre_ref, bim_ref, rep_ref,
                     abr_ref, abi_ref, bbr_ref, bbi_ref):
    a_re = are_ref[...]
    a_im = aim_ref[...]
    dt = jnp.exp(ldt_ref[...])
    mag = jnp.exp(a_re * dt)
    abr = mag * jnp.cos(a_im * dt)
    abi = mag * jnp.sin(a_im * dt)
    den = a_re * a_re + a_im * a_im
    nr = abr - 1.0
    cr = _dot_ones((nr * a_re + abi * a_im) / den, rep_ref[...], terms=3)
    ci = _dot_ones((abi * a_re - nr * a_im) / den, rep_ref[...], terms=3)
    b_re = bre_ref[...]
    b_im = bim_ref[...]
    abr_ref[...] = abr
    abi_ref[...] = abi
    bbr_ref[...] = cr * b_re - ci * b_im
    bbi_ref[...] = cr * b_im + ci * b_re


def _s5_params(a_re, a_im, log_dt, b_re, b_im, c_re, c_im):
    gp = S5_GROUPS * S5_STATE
    pc = S5_STATE * S5_GROUP
    rep = jnp.repeat(jnp.eye(S5_STATE, dtype=BF16), S5_GROUP, axis=1)
    ins = [a_re, a_im, log_dt.reshape(S5_GROUPS, 1), b_re.reshape(S5_GROUPS, pc),
           b_im.reshape(S5_GROUPS, pc), rep]
    gs = (S5_GROUPS, S5_STATE)
    abr, abi, bbr, bbi = pl.pallas_call(
        _s5_param_kernel,
        in_specs=[_full(t.shape) for t in ins],
        out_specs=(_full(gs), _full(gs), _full((S5_GROUPS, pc)), _full((S5_GROUPS, pc))),
        out_shape=(jax.ShapeDtypeStruct(gs, F32),) * 2
                  + (jax.ShapeDtypeStruct((S5_GROUPS, pc), F32),) * 2,
        grid=(1,),
        compiler_params=_cparams(("arbitrary",)),
        name="s5_discretise",
    )(*ins)
    gl = S5_GROUPS // S5_SUPER
    eye = jnp.eye(gl, dtype=F32)

    def b_bd(bb):
        t = bb.reshape(S5_SUPER, gl, S5_STATE, S5_GROUP)
        return jnp.einsum("sgpc,gh->sgchp", t, eye).reshape(S5_SUPER, gl * S5_GROUP, gl * S5_STATE)

    def c_bd(cc):
        t = cc.reshape(S5_SUPER, gl, S5_GROUP, S5_STATE)
        return jnp.einsum("sgcp,gh->sgphc", t, eye).reshape(S5_SUPER, gl * S5_STATE, gl * S5_GROUP)

    b_mat = jnp.concatenate([b_bd(bbr), b_bd(bbi)], axis=2).astype(BF16)
    c_mat = jnp.concatenate([c_bd(c_re), -c_bd(c_im)], axis=1).astype(BF16)
    return abr.reshape(1, gp), abi.reshape(1, gp), b_mat, c_mat


def _s5_kernel(bb, tc, x_ref, mod_ref, g_ref, abr_ref, abi_ref, bmat_ref, cmat_ref, dsk_ref,
               wglu_ref, bglu_ref, s0r_ref, s0i_ref, xo_ref, sr_ref, si_ref,
               xr_scr, xi_scr, str_scr, sti_scr):
    ti = pl.program_id(0)
    sw = S5_LANES // S5_SUPER

    @pl.when(ti == 0)
    def _():
        str_scr[...] = s0r_ref[...]
        sti_scr[...] = s0i_ref[...]

    x = x_ref[...]
    mod = mod_ref[...]
    u = _modulate(_rms(x, g_ref[...]), mod)
    cw = S5_GROUPS // S5_SUPER * S5_GROUP
    for sg in range(S5_SUPER):
        bu = _dot(u[:, sg * cw:(sg + 1) * cw], bmat_ref[sg])
        xr_scr[:, sg * sw:(sg + 1) * sw] = bu[:, :sw]
        xi_scr[:, sg * sw:(sg + 1) * sw] = bu[:, sw:]

    lw = 1024
    for rg in range(bb // 8):
        for lc in range(S5_LANES // lw):
            ln = slice(lc * lw, (lc + 1) * lw)
            ar = jnp.broadcast_to(abr_ref[:, ln], (8, lw))
            ai = jnp.broadcast_to(abi_ref[:, ln], (8, lw))
            sr = str_scr[rg * 8:(rg + 1) * 8, ln]
            si = sti_scr[rg * 8:(rg + 1) * 8, ln]
            for t in range(tc):
                rr = slice(t * bb + rg * 8, t * bb + rg * 8 + 8)
                sr, si = (ar * sr - ai * si + xr_scr[rr, ln], ar * si + ai * sr + xi_scr[rr, ln])
                xr_scr[rr, ln] = sr
                xi_scr[rr, ln] = si
            str_scr[rg * 8:(rg + 1) * 8, ln] = sr
            sti_scr[rg * 8:(rg + 1) * 8, ln] = si

    ys = []
    for sg in range(S5_SUPER):
        ys.append(_dot(xr_scr[:, sg * sw:(sg + 1) * sw], cmat_ref[sg, 0:sw, :])
                  + _dot(xi_scr[:, sg * sw:(sg + 1) * sw], cmat_ref[sg, sw:2 * sw, :]))
    yy = jnp.concatenate(ys, axis=1) + dsk_ref[...] * u
    gl = _dot(_gelu(yy), wglu_ref[...]) + bglu_ref[...]
    out = gl[:, :D] * _sigmoid(gl[:, D:])
    xo_ref[...] = _gated_residual(x, out, mod)

    @pl.when(ti == pl.num_programs(0) - 1)
    def _():
        sr_ref[...] = str_scr[...]
        si_ref[...] = sti_scr[...]


def _s5_layer(x, mod, norm_g, sp, d_skip, w_glu_bf, b_glu, s0r, s0i, bb, t):
    abr, abi, b_mat, c_mat = sp
    tc = min(S5_STEPS, t)
    rows = tc * bb
    ins = [x, mod[0], norm_g, abr, abi, b_mat, c_mat, d_skip, w_glu_bf[0], b_glu, s0r, s0i]
    in_specs = [_rows(rows, D), _mod_spec(mod)] + [_full(v.shape) for v in ins[2:]]
    in_specs[5], in_specs[6] = _resident(b_mat.shape), _resident(c_mat.shape)
    in_specs[8] = _pick_spec(w_glu_bf, True)
    st = jax.ShapeDtypeStruct((bb, S5_LANES), F32)
    return pl.pallas_call(
        functools.partial(_s5_kernel, bb, tc),
        grid=(t // tc,),
        in_specs=in_specs,
        out_specs=(_rows(rows, D), _full((bb, S5_LANES)), _full((bb, S5_LANES))),
        out_shape=(jax.ShapeDtypeStruct((t * bb, D), F32), st, st),
        scratch_shapes=[pltpu.VMEM((rows, S5_LANES), F32), pltpu.VMEM((rows, S5_LANES), F32),
                        pltpu.VMEM((bb, S5_LANES), F32), pltpu.VMEM((bb, S5_LANES), F32)],
        compiler_params=_cparams(("arbitrary",)),
        name="s5_layer",
    )(*ins)


def _route(logits_t, rb):
    s = _sigmoid(logits_t)
    sel = s + rb
    rows = [sel[e:e + 1] for e in range(N_EXPERTS)]
    scores = []
    for gi in range(N_EGROUPS):
        m = rows[gi * EGROUP:(gi + 1) * EGROUP]
        best = None
        for i in range(EGROUP):
            for j in range(i + 1, EGROUP):
                pair = m[i] + m[j]
                best = pair if best is None else jnp.maximum(best, pair)
        scores.append(best)
    top = scores[0]
    grp = jnp.zeros_like(top, dtype=jnp.int32)
    for gi in range(1, N_EGROUPS):
        better = scores[gi] > top
        grp = jnp.where(better, gi, grp)
        top = jnp.where(better, scores[gi], top)
    picked = []
    for e in range(N_EXPERTS):
        gi = e // EGROUP
        rank = jnp.zeros_like(grp)
        for m in range(gi * EGROUP, (gi + 1) * EGROUP):
            if m == e:
                continue
            ahead = (rows[m] > rows[e]) if m > e else (rows[m] >= rows[e])
            rank = rank + ahead.astype(jnp.int32)
        chosen = (grp == gi) & (rank < 2)
        picked.append(jnp.where(chosen, s[e:e + 1], 0.0))
    w = jnp.concatenate(picked, axis=0)
    return w / jnp.sum(w, axis=0, keepdims=True)


def _experts(hb, gates, wg_ref, wu_ref, wd_ref):
    acc = jnp.zeros((hb.shape[0], D), F32)
    for e in range(N_EXPERTS):
        hg = jnp.dot(hb, wg_ref[e], preferred_element_type=F32)
        hu = jnp.dot(hb, wu_ref[e], preferred_element_type=F32)
        act = _silu(hg) * hu * gates[:, e:e + 1]
        acc = acc + jnp.dot(act.astype(BF16), wd_ref[e], preferred_element_type=F32)
    return acc


def _moe_kernel(final, out_bt, x_ref, mod_ref, g_ref, rw_ref, rb_ref, wg_ref, wu_ref, wd_ref, fg_ref,
                o_ref, *scratch):
    x = x_ref[...]
    mod = mod_ref[...]
    h = _modulate(_rms(x, g_ref[...]), mod)
    rw = rw_ref[...]
    rw_hi = rw.astype(BF16)
    rw_lo = (rw - rw_hi.astype(F32)).astype(BF16)
    h_hi = h.astype(BF16)
    h_lo = (h - h_hi.astype(F32)).astype(BF16)
    logits_t = _dot_nt(rw_hi, h_hi) + (_dot_nt(rw_hi, h_lo) + _dot_nt(rw_lo, h_hi))
    gates_t = _route(logits_t[:N_EXPERTS], rb_ref[...])
    pad = jnp.zeros((ROUTER_ROWS - N_EXPERTS, gates_t.shape[1]), F32)
    gates = jnp.transpose(jnp.concatenate([gates_t, pad], axis=0))
    acc = _experts(h_hi, gates, wg_ref, wu_ref, wd_ref)
    xn = _gated_residual(x, acc, mod)
    if final:
        xn = _rms(xn, fg_ref[...])
    if out_bt:
        nb = mod.shape[0]
        _from_time_major(xn, o_ref, scratch[0], nb, x.shape[0] // nb)
    else:
        o_ref[...] = xn


def _moe_layer(x, mod, norm_g, router_wt, router_b, wg, wu, wd, final_g, final, out_bt):
    rows = x.shape[0]
    nb = mod[0].shape[1]
    tm = min(MOE_ROWS, rows)
    if out_bt:
        out_spec = pl.BlockSpec((nb, tm // nb, D), lambda i: (0, i, 0))
        out_shape = jax.ShapeDtypeStruct((nb, rows // nb, D), F32)
        scratch = [pltpu.VMEM((D // 128, tm, 128), F32)]
    else:
        out_spec, out_shape, scratch = _rows(tm, D), jax.ShapeDtypeStruct((rows, D), F32), []
    return pl.pallas_call(
        functools.partial(_moe_kernel, final, out_bt),
        grid=(rows // tm,),
        in_specs=[_rows(tm, D), _mod_spec(mod),
                  _full((1, D)), _full((ROUTER_ROWS, D)), _full((N_EXPERTS, 1)),
                  _pick_spec(wg, True), _pick_spec(wu, True), _pick_spec(wd, True), _full((1, D))],
        out_specs=out_spec,
        out_shape=out_shape,
        scratch_shapes=scratch,
        compiler_params=_cparams(("parallel",)),
        name="moe",
    )(x, mod[0], norm_g, router_wt, router_b, wg[0], wu[0], wd[0], final_g)


def _trunk(x3, mods, states, P, W):
    B, T, _ = x3.shape
    fresh = states is None
    assert fresh or T == 1, "a group that carries state is stepped one token at a time"
    depth = P["norm_g"].shape[0]
    x = x3 if (fresh and T > 1) else x3.reshape(B, D)
    outs = {k: [] for k in ("wkv", "shift", "lru", "conv", "s5r", "s5i")}
    v_first = jnp.zeros((8, 128), F32)
    for layer in range(depth):
        e = layer // 2
        ng = P["norm_g"][layer]
        mod_a, mod_m = (mods, 2 * layer), (mods, 2 * layer + 1)
        if layer % 2 == 0:
            ep, has_vlora = W["even"][e]
            if fresh:
                x, vf_new, wkv, sh, lr, cv = _even_layer_prompt(
                    x, mod_a, ng[0:1], (W["w_in"], e), v_first, ep, has_vlora, (W["w_out"], e), B, T)
                idx = jnp.arange(HEADS)
                wkv = wkv.reshape(B, PAIRS, 2, HEAD_DIM, 2, HEAD_DIM)
                wkv = wkv[:, idx // 2, idx % 2, :, idx % 2, :]
                wkv = jnp.moveaxis(wkv, 0, 1)
                cv = jnp.swapaxes(cv.reshape(CONV_W - 1, B, LRU), 0, 1)
            else:
                x, vf_new, wkv, sh, lr, cv = _even_layer_sample(
                    x, mod_a, ng[0:1], (W["w_in"], e), v_first, (states["wkv"], e), states["shift"][e],
                    states["lru"][e], states["conv"][e], ep, has_vlora, (W["w_out"], e))
            if not has_vlora:
                v_first = vf_new
            outs["wkv"].append(wkv); outs["shift"].append(sh); outs["lru"].append(lr)
            outs["conv"].append(cv)
        else:
            if fresh:
                s0r = jnp.zeros((B, S5_LANES), F32)
                s0i = s0r
            else:
                s0r = states["s5r"][e].reshape(B, S5_LANES)
                s0i = states["s5i"][e].reshape(B, S5_LANES)
            x, sr, si = _s5_layer(x, mod_a, ng[0:1], W["s5"][e], P["s5_d"][e].reshape(1, D),
                                  (W["w_glu"], e), P["s5_b_glu"][e].reshape(1, 2 * D), s0r, s0i, B, T)
            outs["s5r"].append(sr.reshape(B, S5_GROUPS, S5_STATE))
            outs["s5i"].append(si.reshape(B, S5_GROUPS, S5_STATE))
        last = layer == depth - 1
        x = _moe_layer(x, mod_m, ng[1:2], W["router_wt"], W["router_b"], (W["wg"], layer),
                       (W["wu"], layer), (W["wd"], layer), P["final_norm_g"].reshape(1, D), last,
                       last and x3.shape[1] > 1)
    y = x.reshape(B, T, D)
    return (y, jnp.stack(outs["wkv"]), jnp.stack(outs["shift"]), jnp.stack(outs["lru"]),
            jnp.stack(outs["conv"]), jnp.stack(outs["s5r"]), jnp.stack(outs["s5i"]))


def kernel(x_prompt, x_sample, state_wkv, state_shift, state_lru, state_conv, state_s5_re, state_s5_im, c_prompt, c_sample, norm_g, ada_w, ada_b, final_norm_g, even_w_in, rwkv_mu, rwkv_w0, rwkv_w2, rwkv_a0, rwkv_a2, rwkv_g2, rwkv_k_k, rwkv_k_a, rwkv_r_k, rwkv_lnx_w, rwkv_lnx_b, rwkv_v0, rwkv_v1, rwkv_v2, lru_conv_w, lru_conv_b, lru_wa, lru_ba, lru_wx, lru_bx, lru_lam, even_w_out, s5_a_re, s5_a_im, s5_log_dt, s5_b_re, s5_b_im, s5_c_re, s5_c_im, s5_d, s5_w_glu, s5_b_glu, router_w, router_b, moe_w_gate, moe_w_up, moe_w_down):
    P = dict(norm_g=norm_g, final_norm_g=final_norm_g, rwkv_mu=rwkv_mu, rwkv_w0=rwkv_w0,
             rwkv_w2=rwkv_w2, rwkv_a0=rwkv_a0, rwkv_a2=rwkv_a2, rwkv_g2=rwkv_g2, rwkv_k_k=rwkv_k_k,
             rwkv_k_a=rwkv_k_a, rwkv_r_k=rwkv_r_k.reshape(rwkv_r_k.shape[0], RW),
             rwkv_lnx_w=rwkv_lnx_w, rwkv_lnx_b=rwkv_lnx_b, rwkv_v0=rwkv_v0, rwkv_v1=rwkv_v1,
             rwkv_v2=rwkv_v2, lru_conv_w=lru_conv_w, lru_conv_b=lru_conv_b, lru_wa=lru_wa,
             lru_ba=lru_ba.reshape(lru_ba.shape[0], LRU), lru_wx=lru_wx,
             lru_bx=lru_bx.reshape(lru_bx.shape[0], LRU), lru_lam=lru_lam.reshape(lru_lam.shape[0], LRU),
             s5_d=s5_d, s5_b_glu=s5_b_glu)
    n_even, n_odd = even_w_in.shape[0], s5_w_glu.shape[0]
    W = dict(
        even=_even_params(P),
        w_in=even_w_in.astype(BF16), w_out=even_w_out.astype(BF16), w_glu=s5_w_glu.astype(BF16),
        s5=[_s5_params(s5_a_re[e], s5_a_im[e], s5_log_dt[e], s5_b_re[e], s5_b_im[e], s5_c_re[e],
                       s5_c_im[e]) for e in range(n_odd)],
        router_wt=jnp.pad(router_w.T, ((0, ROUTER_ROWS - N_EXPERTS), (0, 0))),
        router_b=router_b.reshape(N_EXPERTS, 1),
        wg=moe_w_gate.astype(BF16), wu=moe_w_up.astype(BF16), wd=moe_w_down.astype(BF16),
    )
    bp = x_prompt.shape[0]
    mods_p, mods_s = _ada_all(jnp.concatenate([c_prompt, c_sample], axis=0), bp, ada_w, ada_b)
    out_p = _trunk(x_prompt, mods_p, None, P, W)
    st = dict(wkv=state_wkv, shift=state_shift, lru=state_lru, conv=state_conv,
              s5r=state_s5_re, s5i=state_s5_im)
    out_s = _trunk(x_sample, mods_s, st, P, W)
    return (out_p[0], out_s[0]) + tuple(out_p[1:]) + tuple(out_s[1:])
```

```python
import functools
import math

import jax
import jax.numpy as jnp
from jax import lax
from jax.experimental import pallas as pl
from jax.experimental.pallas import tpu as pltpu

F32 = jnp.float32
BF16 = jnp.bfloat16

D = 1024
HEADS = 8
HEAD_DIM = 64
PAIRS = HEADS // 2
RW = HEADS * HEAD_DIM
OFF_W = 3 * RW
OFF_A = OFF_W + 64
OFF_G = OFF_A + 64
RCOLS = OFF_G + 128
LRU = 512
CONV_W = 4
IN_COLS = RCOLS + 2 * LRU
S5_GROUP = 16
S5_GROUPS = 64
S5_STATE = 64
S5_LANES = S5_GROUPS * S5_STATE
S5_SUPER = 4
N_EXPERTS = 16
N_EGROUPS = 4
EGROUP = 4
D_EXPERT = 256
ROUTER_ROWS = 128
RMS_EPS = 1e-6
LNX_EPS = 64e-5
LRU_C = 8.0

WKV_CHUNK = 64
WKV_BATCH = 4
S5_STEPS = 32
MOE_ROWS = 512
VMEM_LIMIT = 56 * 1024 * 1024


def _cparams(sem):
    return pltpu.CompilerParams(dimension_semantics=sem, vmem_limit_bytes=VMEM_LIMIT)


def _dot(a, b):
    return jnp.dot(a.astype(BF16), b.astype(BF16), preferred_element_type=F32)


def _dot_hi(a, b):
    return jnp.dot(a, b, precision=lax.Precision.HIGHEST, preferred_element_type=F32)


def _dot_nt(a, b):
    return lax.dot_general(a.astype(BF16), b.astype(BF16), (((1,), (1,)), ((), ())),
                           preferred_element_type=F32)


def _dot_tn(a, b):
    return lax.dot_general(a.astype(BF16), b.astype(BF16), (((0,), (0,)), ((), ())),
                           preferred_element_type=F32)


_dot_solve = _dot


def _dot_ones(x, ones_bf, terms=2):
    out = None
    for _ in range(terms):
        part = x.astype(BF16)
        x = x - part.astype(F32)
        d = jnp.dot(part, ones_bf, preferred_element_type=F32)
        out = d if out is None else out + d
    return out


def _sigmoid(x):
    return 1.0 / (1.0 + jnp.exp(-x))


def _silu(x):
    return x * _sigmoid(x)


def _softplus(x):
    return jnp.maximum(x, 0.0) + jnp.log1p(jnp.exp(-jnp.abs(x)))


def _gelu(x):
    c = math.sqrt(2.0 / math.pi)
    return 0.5 * x * (1.0 + jnp.tanh(c * (x + 0.044715 * (x * x * x))))


def _rms(x, g):
    ms = jnp.mean(x * x, axis=-1, keepdims=True)
    return x * lax.rsqrt(ms + RMS_EPS) * g


def _per_batch(fn, y, *ms):
    nb = ms[0].shape[0]
    rows, w = y.shape
    return fn(y.reshape(rows // nb, nb, w), *[m[None] for m in ms]).reshape(rows, w)


def _modulate(y, mod):
    return _per_batch(lambda y3, sc, sh: y3 * (1.0 + sc) + sh, y, mod[:, D:2 * D], mod[:, :D])


def _gated_residual(x, out, mod):
    return x + _per_batch(lambda o3, gt: o3 * gt, out, mod[:, 2 * D:])


def _full(shape):
    n = len(shape)
    return pl.BlockSpec(shape, lambda *_: (0,) * n)


def _resident(shape):
    n = len(shape)
    return pl.BlockSpec(shape, lambda *_: (0,) * n, pipeline_mode=pl.Buffered(1))


def _rows(tm, width):
    return pl.BlockSpec((tm, width), lambda i: (i, 0))


def _pick_spec(sel, single_buffer=False):
    stack, i = sel
    tail = (0,) * (stack.ndim - 1)
    kw = dict(pipeline_mode=pl.Buffered(1)) if single_buffer else {}
    return pl.BlockSpec((None,) + stack.shape[1:], lambda *_: (i,) + tail, **kw)


def _mod_spec(mod):
    return _pick_spec(mod)


def _ada_kernel(n_first, c_ref, w_ref, b_ref, o1_ref, o2_ref):
    m = _dot(_silu(c_ref[...]), w_ref[0]) + b_ref[0]
    o1_ref[0] = m[:n_first]
    o2_ref[0] = m[n_first:]


def _ada_all(c_all, n_first, ada_w, ada_b):
    n_sub = ada_w.shape[0] * ada_w.shape[1]
    w = ada_w.reshape(n_sub, D, 3 * D)
    b = ada_b.reshape(n_sub, 1, 3 * D)
    rows = c_all.shape[0]
    tn = 768
    out = lambda r: (pl.BlockSpec((1, r, tn), lambda s, j: (s, 0, j)),
                     jax.ShapeDtypeStruct((n_sub, r, 3 * D), F32))
    (spec1, shape1), (spec2, shape2) = out(n_first), out(rows - n_first)
    return pl.pallas_call(
        functools.partial(_ada_kernel, n_first),
        grid=(n_sub, 3 * D // tn),
        in_specs=[pl.BlockSpec((rows, D), lambda s, j: (0, 0)),
                  pl.BlockSpec((1, D, tn), lambda s, j: (s, 0, j)),
                  pl.BlockSpec((1, 1, tn), lambda s, j: (s, 0, j))],
        out_specs=(spec1, spec2),
        out_shape=(shape1, shape2),
        compiler_params=_cparams(("parallel", "parallel")),
        name="ada_ln",
    )(c_all, w, b)


def _rwkv_rows(pr, prev, vf, p, has_vlora):
    ps = pr + (prev - pr) * p["mu"]
    r = ps[:, :RW]
    k = ps[:, RW:2 * RW]
    v = ps[:, 2 * RW:OFF_W]
    wd = ps[:, OFF_W:OFF_A]
    ad = ps[:, OFF_A:OFF_G]
    gd = ps[:, OFF_G:RCOLS]
    lw = -math.exp(-0.5) * _sigmoid(p["w0"] + _dot(jnp.tanh(wd), p["w2"]))
    if has_vlora:
        v = v + (vf - v) * _sigmoid(p["v0"] + _dot(_dot(v, p["v1"]), p["v2"]))
    a = _sigmoid(p["a0"] + _dot(ad, p["a2"]))
    g = _dot(_sigmoid(gd), p["g2"])
    kk = k * p["k_k"]
    ss = _dot(kk * kk, p["hsum"])
    kk = kk * lax.rsqrt(jnp.maximum(ss, 1e-24))
    k = k * (1.0 + (a - 1.0) * p["k_a"])
    return r, lw, k, v, -kk, kk * a, g


def _rwkv_post(y, r, k, v, g, p):
    hs = p["hsum"]
    mean = _dot(y, hs) * (1.0 / HEAD_DIM)
    d = y - mean
    var = _dot(d * d, hs) * (1.0 / HEAD_DIM)
    yn = d * lax.rsqrt(var + LNX_EPS) * p["lnx_w"] + p["lnx_b"]
    bonus = _dot(r * k * p["r_k"], hs) * v
    return (yn + bonus) * g


def _lru_rows(xc, p):
    gate_r = _sigmoid(_dot(xc, p["wa"]) + p["ba"])
    gate_i = _sigmoid(_dot(xc, p["wx"]) + p["bx"])
    log_a = -LRU_C * gate_r * _softplus(-p["lam"])
    a_t = jnp.exp(log_a)
    b_t = jnp.sqrt(1.0 - jnp.exp(2.0 * log_a)) * gate_i * xc
    return a_t, b_t


_EVEN_PARAM_NAMES = ("mu", "w0", "w2", "a0", "a2", "g2", "k_k", "k_a", "r_k", "lnx_w", "lnx_b",
                     "v0", "v1", "v2", "conv_w", "conv_b", "wa", "ba", "wx", "bx", "lam", "hsum")


def _load_params(refs):
    return {n: r[...] for n, r in zip(_EVEN_PARAM_NAMES, refs)}


def _even_params(P):
    n_even = P["rwkv_mu"].shape[0]
    eye8 = jnp.eye(HEADS, dtype=F32)

    def bdiag(w):
        return jnp.einsum("lncd,nm->lncmd", w, eye8).reshape(w.shape[0], LRU, LRU).astype(BF16)

    rows = lambda v: v.reshape(v.shape[0], 1, -1)
    stacks = {
        "mu": rows(P["rwkv_mu"]), "w0": rows(P["rwkv_w0"]), "w2": P["rwkv_w2"],
        "a0": rows(P["rwkv_a0"]), "a2": P["rwkv_a2"], "g2": P["rwkv_g2"],
        "k_k": rows(P["rwkv_k_k"]), "k_a": rows(P["rwkv_k_a"]), "r_k": rows(P["rwkv_r_k"]),
        "lnx_w": rows(P["rwkv_lnx_w"]), "lnx_b": rows(P["rwkv_lnx_b"]),
        "v0": rows(P["rwkv_v0"]), "v1": P["rwkv_v1"], "v2": P["rwkv_v2"],
        "conv_w": P["lru_conv_w"], "conv_b": rows(P["lru_conv_b"]),
        "wa": bdiag(P["lru_wa"]), "ba": rows(P["lru_ba"]),
        "wx": bdiag(P["lru_wx"]), "bx": rows(P["lru_bx"]),
        "lam": rows(P["lru_lam"]),
        "hsum": jnp.kron(eye8, jnp.ones((HEAD_DIM, HEAD_DIM), F32)).astype(BF16)[None],
    }
    vlora = ("v0", "v1", "v2")
    shared = ("hsum",)
    out = []
    for e in range(n_even):
        idx = lambda n: 0 if n in shared else (max(e - 1, 0) if n in vlora else e)
        out.append(({n: (s, idx(n)) for n, s in stacks.items()}, e > 0))
    return out


def _wkv_consts(L):
    L2 = 2 * L
    ri = lax.broadcasted_iota(jnp.int32, (L2, L2), 0)
    ci = lax.broadcasted_iota(jnp.int32, (L2, L2), 1)
    same = (ri >= L) == (ci >= L)
    lane = lax.broadcasted_iota(jnp.int32, (1, 2 * HEAD_DIM), 1)
    return {
        "smask": same & (ri > ci),
        "imask": same & (ri >= ci),
        "eye": (ri == ci).astype(F32),
        "m0": lane < HEAD_DIM,
    }


def _wkv_chunks(chains, cst):
    L = chains[0][0].shape[0]
    L2 = 2 * L
    m0 = cst["m0"]
    each = lambda fn, *cols: [fn(*xs) for xs in zip(*cols)]

    def stk(x):
        return jnp.concatenate([jnp.where(m0, x, 0.0), jnp.where(m0, 0.0, x)], axis=0)

    rt, at, bt, kt, b2, k2, v, s, egl = [list(c) for c in zip(*chains)]
    a4 = each(lambda a, r: jnp.concatenate([stk(a), stk(r)], axis=0), at, rt)
    b4 = each(lambda b, k: jnp.concatenate([stk(b), stk(k)], axis=0), bt, kt)
    pm = each(_dot_nt, a4, b4)
    ah = each(_dot_nt, a4, s)
    n_ab = [jnp.where(cst["smask"], x[:L2, :L2], 0.0) for x in pm]
    a_ak = [jnp.where(cst["smask"], x[:L2, L2:], 0.0) for x in pm]
    a_r = [jnp.concatenate([jnp.where(cst["imask"], x[L2:, :L2], 0.0),
                            jnp.where(cst["imask"], x[L2:, L2:], 0.0)], axis=1) for x in pm]
    v2 = [stk(x) for x in v]
    rhs = each(lambda h, m, w: h[:L2] + _dot(m, w), ah, a_ak, v2)
    t_inv = [cst["eye"] + n for n in n_ab]
    pk = n_ab
    for _ in range(int(math.log2(L)) - 1):
        pk = each(_dot_solve, pk, pk)
        t_inv = each(lambda t, q: t + _dot_solve(t, q), t_inv, pk)
    u2 = each(_dot_solve, t_inv, rhs)
    uv = each(lambda u, w: jnp.concatenate([u, w], axis=0), u2, v2)
    y2 = each(lambda h, m, w: h[L2:] + _dot(m, w), ah, a_r, uv)
    ys = [x[:L] + x[L:] for x in y2]
    bk = each(lambda b, k: jnp.concatenate([stk(b), stk(k)], axis=0), b2, k2)
    s_new = each(lambda s0, e, w, q: s0 * e + _dot_tn(w, q), s, egl, uv, bk)
    return list(zip(ys, s_new))


def _shift_time(x, fill, nb, d):
    return jnp.concatenate([fill, x[:x.shape[0] - d * nb]], axis=0)


def _time_cumsum(x, nb):
    d = nb
    while d < x.shape[0]:
        x = x + _shift_time(x, jnp.zeros((d, x.shape[1]), F32), 1, d)
        d *= 2
    return x


def _time_linear_scan(a, b, nb):
    d = nb
    while d < a.shape[0]:
        a_sh = _shift_time(a, jnp.ones((d, a.shape[1]), F32), 1, d)
        b_sh = _shift_time(b, jnp.zeros((d, b.shape[1]), F32), 1, d)
        b = a * b_sh + b
        a = a * a_sh
        d *= 2
    return a, b


def _to_time_major(x_ref, scr, nb, steps):
    planes = x_ref.shape[2] // 128
    for c in range(planes):
        for bi in range(nb):
            scr[c, pl.ds(bi, steps, stride=nb), :] = x_ref[bi, :, c * 128:(c + 1) * 128]
    return jnp.concatenate([scr[c] for c in range(planes)], axis=1)


def _from_time_major(x, o_ref, scr, nb, steps):
    planes = x.shape[1] // 128
    for c in range(planes):
        scr[c] = x[:, c * 128:(c + 1) * 128]
    for c in range(planes):
        for bi in range(nb):
            o_ref[bi, :, c * 128:(c + 1) * 128] = scr[c, pl.ds(bi, steps, stride=nb), :]


def _even_layer_kernel(has_vlora, x_bt, nb, *refs):
    n_par = len(_EVEN_PARAM_NAMES)
    x_ref, mod_ref, ng_ref, win_ref, vf_ref = refs[:5]
    p_refs = refs[5:5 + n_par]
    wout_ref = refs[5 + n_par]
    (xo_ref, vfo_ref, wkv_ref, shift_ref, lru_ref, conv_ref) = refs[6 + n_par:12 + n_par]
    (s_scr, prev_scr, hist_scr, h_scr, egl_scr,
     rt_s, at_s, bt_s, kt_s, b2_s, k2_s, v_s, y_s) = refs[12 + n_par:25 + n_par]
    ti = pl.program_id(0)
    L = WKV_CHUNK
    R = L * nb

    @pl.when(ti == 0)
    def _():
        s_scr[...] = jnp.zeros_like(s_scr)
        prev_scr[...] = jnp.zeros_like(prev_scr)
        hist_scr[...] = jnp.zeros_like(hist_scr)
        h_scr[...] = jnp.zeros_like(h_scr)

    p = _load_params(p_refs)
    x = _to_time_major(x_ref, refs[25 + n_par], nb, L) if x_bt else x_ref[...]
    mod = mod_ref[...]
    proj = _dot(_modulate(_rms(x, ng_ref[...]), mod), win_ref[...])
    pr = proj[:, :RCOLS]
    prev = _shift_time(pr, prev_scr[...], nb, 1)
    prev_scr[...] = pr[R - nb:]
    r, lw, k, v, a, b, g = _rwkv_rows(pr, prev, vf_ref[...], p, has_vlora)
    vfo_ref[...] = jnp.zeros_like(vfo_ref) if has_vlora else v

    gc = _time_cumsum(lw, nb)
    g3 = gc.reshape(L, nb, RW)
    gl = g3[L - 1:L]
    egl = jnp.broadcast_to(jnp.exp(gl), (8, nb, RW)).reshape(8 * nb, RW)
    to_l = jnp.exp(gl - g3).reshape(R, RW)
    ieg = jnp.exp(-gc)
    ops = (r * jnp.exp(gc), a * jnp.exp(gc - lw), b * ieg, k * ieg, b * to_l, k * to_l, v)
    op_refs = (rt_s, at_s, bt_s, kt_s, b2_s, k2_s, v_s)
    for pi in range(PAIRS):
        for ref, val in zip(op_refs, ops):
            ref[pi] = val[:, pi * 128:(pi + 1) * 128]
        egl_scr[pi] = egl[:, pi * 128:(pi + 1) * 128]
    cst = _wkv_consts(L)

    def per_batch(it, carry):
        ids = [(it * WKV_BATCH + j, pi) for j in range(WKV_BATCH) for pi in range(PAIRS)]
        rows = lambda bi: pl.ds(bi, L, stride=nb)
        chains = [[ref[pi, rows(bi), :] for ref in op_refs]
                  + [s_scr[bi * PAIRS + pi], egl_scr[pi, pl.ds(bi, 8, stride=nb), :][0:1]]
                  for bi, pi in ids]
        for (bi, pi), (y, s_new) in zip(ids, _wkv_chunks(chains, cst)):
            y_s[pi, rows(bi), :] = y
            s_scr[bi * PAIRS + pi] = s_new
        return carry

    lax.fori_loop(0, nb // WKV_BATCH, per_batch, 0)
    y_all = jnp.concatenate([y_s[pi] for pi in range(PAIRS)], axis=1)
    y_rwkv = _rwkv_post(y_all, r, k, v, g, p)

    lx = proj[:, RCOLS:RCOLS + LRU]
    lg = proj[:, RCOLS + LRU:]
    xpad = jnp.concatenate([hist_scr[...], lx], axis=0)
    cw = p["conv_w"]
    xc = p["conv_b"] + xpad[0:R] * cw[0:1]
    for j in range(1, CONV_W):
        xc = xc + xpad[j * nb:j * nb + R] * cw[j:j + 1]
    hist_scr[...] = xpad[R:]
    a_t, b_t = _lru_rows(xc, p)
    a_cum, b_cum = _time_linear_scan(a_t, b_t, nb)
    hs = _per_batch(lambda a3, h0: a3 * h0, a_cum, h_scr[...]) + b_cum
    h_scr[...] = hs[R - nb:]
    y_lru = hs * _gelu(lg)

    out = _dot(y_rwkv, wout_ref[0:RW, :]) + _dot(y_lru, wout_ref[RW:, :])
    xo_ref[...] = _gated_residual(x, out, mod)

    @pl.when(ti == pl.num_programs(0) - 1)
    def _():
        wkv_ref[...] = s_scr[...]
        shift_ref[...] = prev_scr[...]
        lru_ref[...] = h_scr[...]
        conv_ref[...] = hist_scr[...]


def _even_layer_prompt(x, mod, norm_g, w_in_bf, vf, ep, has_vlora, w_out_bf, nb, t):
    L = WKV_CHUNK
    R = L * nb
    x_bt = x.ndim == 3
    x_spec = pl.BlockSpec((nb, L, D), lambda i: (0, i, 0)) if x_bt else _rows(R, D)
    pvals = [ep[n] for n in _EVEN_PARAM_NAMES]
    out_shapes = [
        jax.ShapeDtypeStruct((t * nb, D), F32),
        jax.ShapeDtypeStruct((t * nb, RW) if not has_vlora else (8, 128), F32),
        jax.ShapeDtypeStruct((nb * PAIRS, 128, 128), F32),
        jax.ShapeDtypeStruct((nb, RCOLS), F32),
        jax.ShapeDtypeStruct((nb, LRU), F32),
        jax.ShapeDtypeStruct(((CONV_W - 1) * nb, LRU), F32),
    ]
    out_specs = [_rows(R, D), _rows(R, RW) if not has_vlora else _full((8, 128)),
                 _full((nb * PAIRS, 128, 128)), _full((nb, RCOLS)), _full((nb, LRU)),
                 _full(((CONV_W - 1) * nb, LRU))]
    scratch = [pltpu.VMEM((nb * PAIRS, 128, 128), F32), pltpu.VMEM((nb, RCOLS), F32),
               pltpu.VMEM(((CONV_W - 1) * nb, LRU), F32), pltpu.VMEM((nb, LRU), F32),
               pltpu.VMEM((PAIRS, 8 * nb, 128), F32)]
    scratch += [pltpu.VMEM((PAIRS, R, 128), F32) for _ in range(8)]
    scratch += [pltpu.VMEM((D // 128, R, 128), F32)] if x_bt else []
    vf_spec = _rows(R, RW) if has_vlora else _full(vf.shape)
    return pl.pallas_call(
        functools.partial(_even_layer_kernel, has_vlora, x_bt, nb),
        grid=(t // L,),
        in_specs=[x_spec, _mod_spec(mod), _full((1, D)), _pick_spec(w_in_bf, True), vf_spec]
                 + [_pick_spec(v) for v in pvals] + [_pick_spec(w_out_bf, True)],
        out_specs=out_specs,
        out_shape=out_shapes,
        scratch_shapes=scratch,
        compiler_params=_cparams(("arbitrary",)),
        name="even_layer_prompt",
    )(x, mod[0], norm_g, w_in_bf[0], vf, *[v[0] for v in pvals], w_out_bf[0])


def _in_proj_kernel(x_ref, mod_ref, g_ref, w_ref, o_ref):
    o_ref[...] = _dot(_modulate(_rms(x_ref[...], g_ref[...]), mod_ref[...]), w_ref[...])


def _in_proj(x, mod, norm_g, w_in_bf):
    rows = x.shape[0]
    return pl.pallas_call(
        _in_proj_kernel,
        grid=(1,),
        in_specs=[_full((rows, D)), _mod_spec(mod), _full((1, D)), _pick_spec(w_in_bf, True)],
        out_specs=_full((rows, IN_COLS)),
        out_shape=jax.ShapeDtypeStruct((rows, IN_COLS), F32),
        compiler_params=_cparams(("arbitrary",)),
        name="even_in_proj",
    )(x, mod[0], norm_g, w_in_bf[0])


def _even_pre_sample_kernel(has_vlora, *refs):
    n_par = len(_EVEN_PARAM_NAMES)
    proj_ref, shift_ref, c0_ref, c1_ref, c2_ref, h_ref, vf_ref = refs[:7]
    p = _load_params(refs[7:7 + n_par])
    (r_o, k_o, v_o, g_o, ylru_o, h_o, rt_o, wt_o, kt_o, vt_o, at_o, bt_o) = refs[7 + n_par:]
    proj = proj_ref[...]
    pr = proj[:, :RCOLS]
    r, lw, k, v, a, b, g = _rwkv_rows(pr, shift_ref[...], vf_ref[...], p, has_vlora)
    r_o[...] = r
    k_o[...] = k
    v_o[...] = v
    g_o[...] = g
    for ref, val in zip((rt_o, wt_o, kt_o, vt_o, at_o, bt_o), (r, jnp.exp(lw), k, v, a, b)):
        ref[...] = jnp.transpose(val)
    lx = proj[:, RCOLS:RCOLS + LRU]
    lg = proj[:, RCOLS + LRU:]
    cw = p["conv_w"]
    xc = (p["conv_b"] + c0_ref[...] * cw[0:1] + c1_ref[...] * cw[1:2] + c2_ref[...] * cw[2:3]
          + lx * cw[3:4])
    a_t, b_t = _lru_rows(xc, p)
    hs = a_t * h_ref[...] + b_t
    h_o[...] = hs
    ylru_o[...] = hs * _gelu(lg)


def _wkv_step_kernel(s_ref, r_ref, w_ref, k_ref, v_ref, a_ref, b_ref, so_ref, y_ref):
    r, w, k, a, b = r_ref[...], w_ref[...], k_ref[...], a_ref[...], b_ref[...]
    for i in range(HEAD_DIM):
        s_i = s_ref[i]
        u_i = jnp.sum(s_i * a, axis=0, keepdims=True)
        s_i = s_i * w + u_i * b + v_ref[i:i + 1, :] * k
        so_ref[i] = s_i
        y_ref[i:i + 1, :] = jnp.sum(s_i * r, axis=0, keepdims=True)


def _wkv_step(state_t, e, ops_t):
    nb = state_t.shape[-1]
    s_in = pl.BlockSpec((None, None, HEAD_DIM, HEAD_DIM, nb), lambda h: (e, h, 0, 0, 0))
    s_out = pl.BlockSpec((None, HEAD_DIM, HEAD_DIM, nb), lambda h: (h, 0, 0, 0))
    o_spec = pl.BlockSpec((HEAD_DIM, nb), lambda h: (h, 0))
    return pl.pallas_call(
        _wkv_step_kernel,
        grid=(HEADS,),
        in_specs=[s_in] + [o_spec] * 6,
        out_specs=(s_out, o_spec),
        out_shape=(jax.ShapeDtypeStruct(state_t.shape[1:], F32),
                   jax.ShapeDtypeStruct((RW, nb), F32)),
        compiler_params=_cparams(("parallel",)),
        name="wkv_step_sample",
    )(state_t, *ops_t)


def _even_post_sample_kernel(x_ref, mod_ref, y_ref, r_ref, k_ref, v_ref, g_ref, ylru_ref,
                             rk_ref, lw_ref, lb_ref, hsum_ref, wout_ref, xo_ref):
    p = {"r_k": rk_ref[...], "lnx_w": lw_ref[...], "lnx_b": lb_ref[...], "hsum": hsum_ref[...]}
    y = jnp.transpose(y_ref[...])
    y_rwkv = _rwkv_post(y, r_ref[...], k_ref[...], v_ref[...], g_ref[...], p)
    out = _dot(y_rwkv, wout_ref[0:RW, :]) + _dot(ylru_ref[...], wout_ref[RW:, :])
    xo_ref[...] = _gated_residual(x_ref[...], out, mod_ref[...])


def _even_layer_sample(x, mod, norm_g, w_in_bf, vf, s_wkv, s_shift, s_lru, s_conv, ep, has_vlora,
                       w_out_bf):
    nb = x.shape[0]
    proj = _in_proj(x, mod, norm_g, w_in_bf)
    pvals = [ep[n] for n in _EVEN_PARAM_NAMES]
    rw = jax.ShapeDtypeStruct((nb, RW), F32)
    rw_t = jax.ShapeDtypeStruct((RW, nb), F32)
    ins = [proj, s_shift, s_conv[:, 0], s_conv[:, 1], s_conv[:, 2], s_lru, vf]
    r, k, v, g, y_lru, h_new, *ops_t = pl.pallas_call(
        functools.partial(_even_pre_sample_kernel, has_vlora),
        in_specs=[_full(t.shape) for t in ins] + [_pick_spec(t) for t in pvals],
        out_specs=tuple(_full((nb, RW)) for _ in range(6)) + tuple(_full((RW, nb)) for _ in range(6)),
        out_shape=(rw,) * 6 + (rw_t,) * 6,
        grid=(1,),
        compiler_params=_cparams(("arbitrary",)),
        name="even_pre_sample",
    )(*ins, *[t[0] for t in pvals])
    s_new, y = _wkv_step(s_wkv[0], s_wkv[1], ops_t)
    acts = [y, r, k, v, g, y_lru]
    picks = [ep["r_k"], ep["lnx_w"], ep["lnx_b"], ep["hsum"]]
    x_new = pl.pallas_call(
        _even_post_sample_kernel,
        in_specs=[_full(x.shape), _mod_spec(mod)] + [_full(t.shape) for t in acts]
                 + [_pick_spec(t) for t in picks] + [_pick_spec(w_out_bf, True)],
        out_specs=_full((nb, D)),
        out_shape=jax.ShapeDtypeStruct((nb, D), F32),
        grid=(1,),
        compiler_params=_cparams(("arbitrary",)),
        name="even_post_sample",
    )(x, mod[0], *acts, *[t[0] for t in picks], w_out_bf[0])
    lx = proj[:, RCOLS:RCOLS + LRU]
    conv_new = jnp.concatenate([s_conv[:, 1:], lx[:, None]], axis=1)
    return (x_new, v, s_new, proj[:, :RCOLS], h_new,
            conv_new)


def _s5_param_kernel(are_ref, aim_ref, ldt_ref, bre_ref, bim_ref, rep_ref,
                     abr_ref, abi_ref, bbr_ref, bbi_ref):
    a_re = are_ref[...]
    a_im = aim_ref[...]
    dt = jnp.exp(ldt_ref[...])
    mag = jnp.exp(a_re * dt)
    abr = mag * jnp.cos(a_im * dt)
    abi = mag * jnp.sin(a_im * dt)
    den = a_re * a_re + a_im * a_im
    nr = abr - 1.0
    cr = _dot_ones((nr * a_re + abi * a_im) / den, rep_ref[...], terms=3)
    ci = _dot_ones((abi * a_re - nr * a_im) / den, rep_ref[...], terms=3)
    b_re = bre_ref[...]
    b_im = bim_ref[...]
    abr_ref[...] = abr
    abi_ref[...] = abi
    bbr_ref[...] = cr * b_re - ci * b_im
    bbi_ref[...] = cr * b_im + ci * b_re


def _s5_params(a_re, a_im, log_dt, b_re, b_im, c_re, c_im):
    gp = S5_GROUPS * S5_STATE
    pc = S5_STATE * S5_GROUP
    rep = jnp.repeat(jnp.eye(S5_STATE, dtype=BF16), S5_GROUP, axis=1)
    ins = [a_re, a_im, log_dt.reshape(S5_GROUPS, 1), b_re.reshape(S5_GROUPS, pc),
           b_im.reshape(S5_GROUPS, pc), rep]
    gs = (S5_GROUPS, S5_STATE)
    abr, abi, bbr, bbi = pl.pallas_call(
        _s5_param_kernel,
        in_specs=[_full(t.shape) for t in ins],
        out_specs=(_full(gs), _full(gs), _full((S5_GROUPS, pc)), _full((S5_GROUPS, pc))),
        out_shape=(jax.ShapeDtypeStruct(gs, F32),) * 2
                  + (jax.ShapeDtypeStruct((S5_GROUPS, pc), F32),) * 2,
        grid=(1,),
        compiler_params=_cparams(("arbitrary",)),
        name="s5_discretise",
    )(*ins)
    gl = S5_GROUPS // S5_SUPER
    eye = jnp.eye(gl, dtype=F32)

    def b_bd(bb):
        t = bb.reshape(S5_SUPER, gl, S5_STATE, S5_GROUP)
        return jnp.einsum("sgpc,gh->sgchp", t, eye).reshape(S5_SUPER, gl * S5_GROUP, gl * S5_STATE)

    def c_bd(cc):
        t = cc.reshape(S5_SUPER, gl, S5_GROUP, S5_STATE)
        return jnp.einsum("sgcp,gh->sgphc", t, eye).reshape(S5_SUPER, gl * S5_STATE, gl * S5_GROUP)

    b_mat = jnp.concatenate([b_bd(bbr), b_bd(bbi)], axis=2).astype(BF16)
    c_mat = jnp.concatenate([c_bd(c_re), -c_bd(c_im)], axis=1).astype(BF16)
    return abr.reshape(1, gp), abi.reshape(1, gp), b_mat, c_mat


def _s5_kernel(bb, tc, x_ref, mod_ref, g_ref, abr_ref, abi_ref, bmat_ref, cmat_ref, dsk_ref,
               wglu_ref, bglu_ref, s0r_ref, s0i_ref, xo_ref, sr_ref, si_ref,
               xr_scr, xi_scr, str_scr, sti_scr):
    ti = pl.program_id(0)
    sw = S5_LANES // S5_SUPER

    @pl.when(ti == 0)
    def _():
        str_scr[...] = s0r_ref[...]
        sti_scr[...] = s0i_ref[...]

    x = x_ref[...]
    mod = mod_ref[...]
    u = _modulate(_rms(x, g_ref[...]), mod)
    cw = S5_GROUPS // S5_SUPER * S5_GROUP
    for sg in range(S5_SUPER):
        bu = _dot(u[:, sg * cw:(sg + 1) * cw], bmat_ref[sg])
        xr_scr[:, sg * sw:(sg + 1) * sw] = bu[:, :sw]
        xi_scr[:, sg * sw:(sg + 1) * sw] = bu[:, sw:]

    lw = 1024
    for rg in range(bb // 8):
        for lc in range(S5_LANES // lw):
            ln = slice(lc * lw, (lc + 1) * lw)
            ar = jnp.broadcast_to(abr_ref[:, ln], (8, lw))
            ai = jnp.broadcast_to(abi_ref[:, ln], (8, lw))
            sr = str_scr[rg * 8:(rg + 1) * 8, ln]
            si = sti_scr[rg * 8:(rg + 1) * 8, ln]
            for t in range(tc):
                rr = slice(t * bb + rg * 8, t * bb + rg * 8 + 8)
                sr, si = (ar * sr - ai * si + xr_scr[rr, ln], ar * si + ai * sr + xi_scr[rr, ln])
                xr_scr[rr, ln] = sr
                xi_scr[rr, ln] = si
            str_scr[rg * 8:(rg + 1) * 8, ln] = sr
            sti_scr[rg * 8:(rg + 1) * 8, ln] = si

    ys = []
    for sg in range(S5_SUPER):
        ys.append(_dot(xr_scr[:, sg * sw:(sg + 1) * sw], cmat_ref[sg, 0:sw, :])
                  + _dot(xi_scr[:, sg * sw:(sg + 1) * sw], cmat_ref[sg, sw:2 * sw, :]))
    yy = jnp.concatenate(ys, axis=1) + dsk_ref[...] * u
    gl = _dot(_gelu(yy), wglu_ref[...]) + bglu_ref[...]
    out = gl[:, :D] * _sigmoid(gl[:, D:])
    xo_ref[...] = _gated_residual(x, out, mod)

    @pl.when(ti == pl.num_programs(0) - 1)
    def _():
        sr_ref[...] = str_scr[...]
        si_ref[...] = sti_scr[...]


def _s5_layer(x, mod, norm_g, sp, d_skip, w_glu_bf, b_glu, s0r, s0i, bb, t):
    abr, abi, b_mat, c_mat = sp
    tc = min(S5_STEPS, t)
    rows = tc * bb
    ins = [x, mod[0], norm_g, abr, abi, b_mat, c_mat, d_skip, w_glu_bf[0], b_glu, s0r, s0i]
    in_specs = [_rows(rows, D), _mod_spec(mod)] + [_full(v.shape) for v in ins[2:]]
    in_specs[5], in_specs[6] = _resident(b_mat.shape), _resident(c_mat.shape)
    in_specs[8] = _pick_spec(w_glu_bf, True)
    st = jax.ShapeDtypeStruct((bb, S5_LANES), F32)
    return pl.pallas_call(
        functools.partial(_s5_kernel, bb, tc),
        grid=(t // tc,),
        in_specs=in_specs,
        out_specs=(_rows(rows, D), _full((bb, S5_LANES)), _full((bb, S5_LANES))),
        out_shape=(jax.ShapeDtypeStruct((t * bb, D), F32), st, st),
        scratch_shapes=[pltpu.VMEM((rows, S5_LANES), F32), pltpu.VMEM((rows, S5_LANES), F32),
                        pltpu.VMEM((bb, S5_LANES), F32), pltpu.VMEM((bb, S5_LANES), F32)],
        compiler_params=_cparams(("arbitrary",)),
        name="s5_layer",
    )(*ins)


def _route(logits_t, rb):
    s = _sigmoid(logits_t)
    sel = s + rb
    rows = [sel[e:e + 1] for e in range(N_EXPERTS)]
    scores = []
    for gi in range(N_EGROUPS):
        m = rows[gi * EGROUP:(gi + 1) * EGROUP]
        best = None
        for i in range(EGROUP):
            for j in range(i + 1, EGROUP):
                pair = m[i] + m[j]
                best = pair if best is None else jnp.maximum(best, pair)
        scores.append(best)
    top = scores[0]
    grp = jnp.zeros_like(top, dtype=jnp.int32)
    for gi in range(1, N_EGROUPS):
        better = scores[gi] > top
        grp = jnp.where(better, gi, grp)
        top = jnp.where(better, scores[gi], top)
    picked = []
    for e in range(N_EXPERTS):
        gi = e // EGROUP
        rank = jnp.zeros_like(grp)
        for m in range(gi * EGROUP, (gi + 1) * EGROUP):
            if m == e:
                continue
            ahead = (rows[m] > rows[e]) if m > e else (rows[m] >= rows[e])
            rank = rank + ahead.astype(jnp.int32)
        chosen = (grp == gi) & (rank < 2)
        picked.append(jnp.where(chosen, s[e:e + 1], 0.0))
    w = jnp.concatenate(picked, axis=0)
    return w / jnp.sum(w, axis=0, keepdims=True)


def _experts(hb, gates, wg_ref, wu_ref, wd_ref):
    acc = jnp.zeros((hb.shape[0], D), F32)
    for e in range(N_EXPERTS):
        hg = jnp.dot(hb, wg_ref[e], preferred_element_type=F32)
        hu = jnp.dot(hb, wu_ref[e], preferred_element_type=F32)
        act = _silu(hg) * hu * gates[:, e:e + 1]
        acc = acc + jnp.dot(act.astype(BF16), wd_ref[e], preferred_element_type=F32)
    return acc


def _moe_kernel(final, out_bt, x_ref, mod_ref, g_ref, rw_ref, rb_ref, wg_ref, wu_ref, wd_ref, fg_ref,
                o_ref, *scratch):
    x = x_ref[...]
    mod = mod_ref[...]
    h = _modulate(_rms(x, g_ref[...]), mod)
    rw = rw_ref[...]
    rw_hi = rw.astype(BF16)
    rw_lo = (rw - rw_hi.astype(F32)).astype(BF16)
    h_hi = h.astype(BF16)
    h_lo = (h - h_hi.astype(F32)).astype(BF16)
    logits_t = _dot_nt(rw_hi, h_hi) + (_dot_nt(rw_hi, h_lo) + _dot_nt(rw_lo, h_hi))
    gates_t = _route(logits_t[:N_EXPERTS], rb_ref[...])
    pad = jnp.zeros((ROUTER_ROWS - N_EXPERTS, gates_t.shape[1]), F32)
    gates = jnp.transpose(jnp.concatenate([gates_t, pad], axis=0))
    acc = _experts(h_hi, gates, wg_ref, wu_ref, wd_ref)
    xn = _gated_residual(x, acc, mod)
    if final:
        xn = _rms(xn, fg_ref[...])
    if out_bt:
        nb = mod.shape[0]
        _from_time_major(xn, o_ref, scratch[0], nb, x.shape[0] // nb)
    else:
        o_ref[...] = xn


def _moe_layer(x, mod, norm_g, router_wt, router_b, wg, wu, wd, final_g, final, out_bt):
    rows = x.shape[0]
    nb = mod[0].shape[1]
    tm = min(MOE_ROWS, rows)
    if out_bt:
        out_spec = pl.BlockSpec((nb, tm // nb, D), lambda i: (0, i, 0))
        out_shape = jax.ShapeDtypeStruct((nb, rows // nb, D), F32)
        scratch = [pltpu.VMEM((D // 128, tm, 128), F32)]
    else:
        out_spec, out_shape, scratch = _rows(tm, D), jax.ShapeDtypeStruct((rows, D), F32), []
    return pl.pallas_call(
        functools.partial(_moe_kernel, final, out_bt),
        grid=(rows // tm,),
        in_specs=[_rows(tm, D), _mod_spec(mod),
                  _full((1, D)), _full((ROUTER_ROWS, D)), _full((N_EXPERTS, 1)),
                  _pick_spec(wg, True), _pick_spec(wu, True), _pick_spec(wd, True), _full((1, D))],
        out_specs=out_spec,
        out_shape=out_shape,
        scratch_shapes=scratch,
        compiler_params=_cparams(("parallel",)),
        name="moe",
    )(x, mod[0], norm_g, router_wt, router_b, wg[0], wu[0], wd[0], final_g)


def _trunk(x3, mods, states, P, W):
    B, T, _ = x3.shape
    fresh = states is None
    assert fresh or T == 1, "a group that carries state is stepped one token at a time"
    depth = P["norm_g"].shape[0]
    x = x3 if (fresh and T > 1) else x3.reshape(B, D)
    outs = {k: [] for k in ("wkv", "shift", "lru", "conv", "s5r", "s5i")}
    v_first = jnp.zeros((8, 128), F32)
    wkv_t = None if fresh else jnp.transpose(states["wkv"], (0, 2, 3, 4, 1))
    for layer in range(depth):
        e = layer // 2
        ng = P["norm_g"][layer]
        mod_a, mod_m = (mods, 2 * layer), (mods, 2 * layer + 1)
        if layer % 2 == 0:
            ep, has_vlora = W["even"][e]
            if fresh:
                x, vf_new, wkv, sh, lr, cv = _even_layer_prompt(
                    x, mod_a, ng[0:1], (W["w_in"], e), v_first, ep, has_vlora, (W["w_out"], e), B, T)
                idx = jnp.arange(HEADS)
                wkv = wkv.reshape(B, PAIRS, 2, HEAD_DIM, 2, HEAD_DIM)
                wkv = wkv[:, idx // 2, idx % 2, :, idx % 2, :]
                wkv = jnp.moveaxis(wkv, 0, 1)
                cv = jnp.swapaxes(cv.reshape(CONV_W - 1, B, LRU), 0, 1)
            else:
                x, vf_new, wkv, sh, lr, cv = _even_layer_sample(
                    x, mod_a, ng[0:1], (W["w_in"], e), v_first, (wkv_t, e), states["shift"][e],
                    states["lru"][e], states["conv"][e], ep, has_vlora, (W["w_out"], e))
            if not has_vlora:
                v_first = vf_new
            outs["wkv"].append(wkv); outs["shift"].append(sh); outs["lru"].append(lr)
            outs["conv"].append(cv)
        else:
            if fresh:
                s0r = jnp.zeros((B, S5_LANES), F32)
                s0i = s0r
            else:
                s0r = states["s5r"][e].reshape(B, S5_LANES)
                s0i = states["s5i"][e].reshape(B, S5_LANES)
            x, sr, si = _s5_layer(x, mod_a, ng[0:1], W["s5"][e], P["s5_d"][e].reshape(1, D),
                                  (W["w_glu"], e), P["s5_b_glu"][e].reshape(1, 2 * D), s0r, s0i, B, T)
            outs["s5r"].append(sr.reshape(B, S5_GROUPS, S5_STATE))
            outs["s5i"].append(si.reshape(B, S5_GROUPS, S5_STATE))
        last = layer == depth - 1
        x = _moe_layer(x, mod_m, ng[1:2], W["router_wt"], W["router_b"], (W["wg"], layer),
                       (W["wu"], layer), (W["wd"], layer), P["final_norm_g"].reshape(1, D), last,
                       last and x3.shape[1] > 1)
    y = x.reshape(B, T, D)
    wkv = jnp.stack(outs["wkv"])
    if not fresh:
        wkv = jnp.transpose(wkv, (0, 4, 1, 2, 3))
    return (y, wkv, jnp.stack(outs["shift"]), jnp.stack(outs["lru"]),
            jnp.stack(outs["conv"]), jnp.stack(outs["s5r"]), jnp.stack(outs["s5i"]))


def kernel(x_prompt, x_sample, state_wkv, state_shift, state_lru, state_conv, state_s5_re, state_s5_im, c_prompt, c_sample, norm_g, ada_w, ada_b, final_norm_g, even_w_in, rwkv_mu, rwkv_w0, rwkv_w2, rwkv_a0, rwkv_a2, rwkv_g2, rwkv_k_k, rwkv_k_a, rwkv_r_k, rwkv_lnx_w, rwkv_lnx_b, rwkv_v0, rwkv_v1, rwkv_v2, lru_conv_w, lru_conv_b, lru_wa, lru_ba, lru_wx, lru_bx, lru_lam, even_w_out, s5_a_re, s5_a_im, s5_log_dt, s5_b_re, s5_b_im, s5_c_re, s5_c_im, s5_d, s5_w_glu, s5_b_glu, router_w, router_b, moe_w_gate, moe_w_up, moe_w_down):
    P = dict(norm_g=norm_g, final_norm_g=final_norm_g, rwkv_mu=rwkv_mu, rwkv_w0=rwkv_w0,
             rwkv_w2=rwkv_w2, rwkv_a0=rwkv_a0, rwkv_a2=rwkv_a2, rwkv_g2=rwkv_g2, rwkv_k_k=rwkv_k_k,
             rwkv_k_a=rwkv_k_a, rwkv_r_k=rwkv_r_k.reshape(rwkv_r_k.shape[0], RW),
             rwkv_lnx_w=rwkv_lnx_w, rwkv_lnx_b=rwkv_lnx_b, rwkv_v0=rwkv_v0, rwkv_v1=rwkv_v1,
             rwkv_v2=rwkv_v2, lru_conv_w=lru_conv_w, lru_conv_b=lru_conv_b, lru_wa=lru_wa,
             lru_ba=lru_ba.reshape(lru_ba.shape[0], LRU), lru_wx=lru_wx,
             lru_bx=lru_bx.reshape(lru_bx.shape[0], LRU), lru_lam=lru_lam.reshape(lru_lam.shape[0], LRU),
             s5_d=s5_d, s5_b_glu=s5_b_glu)
    n_even, n_odd = even_w_in.shape[0], s5_w_glu.shape[0]
    W = dict(
        even=_even_params(P),
        w_in=even_w_in.astype(BF16), w_out=even_w_out.astype(BF16), w_glu=s5_w_glu.astype(BF16),
        s5=[_s5_params(s5_a_re[e], s5_a_im[e], s5_log_dt[e], s5_b_re[e], s5_b_im[e], s5_c_re[e],
                       s5_c_im[e]) for e in range(n_odd)],
        router_wt=jnp.pad(router_w.T, ((0, ROUTER_ROWS - N_EXPERTS), (0, 0))),
        router_b=router_b.reshape(N_EXPERTS, 1),
        wg=moe_w_gate.astype(BF16), wu=moe_w_up.astype(BF16), wd=moe_w_down.astype(BF16),
    )
    bp = x_prompt.shape[0]
    mods_p, mods_s = _ada_all(jnp.concatenate([c_prompt, c_sample], axis=0), bp, ada_w, ada_b)
    out_p = _trunk(x_prompt, mods_p, None, P, W)
    st = dict(wkv=state_wkv, shift=state_shift, lru=state_lru, conv=state_conv,
              s5r=state_s5_re, s5i=state_s5_im)
    out_s = _trunk(x_sample, mods_s, st, P, W)
    return (out_p[0], out_s[0]) + tuple(out_p[1:]) + tuple(out_s[1:])
```

```python
import functools
import math

import jax
import jax.numpy as jnp
from jax import lax
from jax.experimental import pallas as pl
from jax.experimental.pallas import tpu as pltpu

F32 = jnp.float32
BF16 = jnp.bfloat16

D = 1024
HEADS = 8
HEAD_DIM = 64
PAIRS = HEADS // 2
RW = HEADS * HEAD_DIM
OFF_W = 3 * RW
OFF_A = OFF_W + 64
OFF_G = OFF_A + 64
RCOLS = OFF_G + 128
LRU = 512
CONV_W = 4
IN_COLS = RCOLS + 2 * LRU
S5_GROUP = 16
S5_GROUPS = 64
S5_STATE = 64
S5_LANES = S5_GROUPS * S5_STATE
S5_SUPER = 4
N_EXPERTS = 16
N_EGROUPS = 4
EGROUP = 4
D_EXPERT = 256
ROUTER_ROWS = 128
RMS_EPS = 1e-6
LNX_EPS = 64e-5
LRU_C = 8.0

WKV_CHUNK = 64
WKV_BATCH = 8
S5_STEPS = 64
MOE_ROWS = 512
VMEM_LIMIT = 56 * 1024 * 1024


def _cparams(sem):
    return pltpu.CompilerParams(dimension_semantics=sem, vmem_limit_bytes=VMEM_LIMIT)


def _dot(a, b):
    return jnp.dot(a.astype(BF16), b.astype(BF16), preferred_element_type=F32)


def _dot_hi(a, b):
    return jnp.dot(a, b, precision=lax.Precision.HIGHEST, preferred_element_type=F32)


def _dot_nt(a, b):
    return lax.dot_general(a.astype(BF16), b.astype(BF16), (((1,), (1,)), ((), ())),
                           preferred_element_type=F32)


def _dot_tn(a, b):
    return lax.dot_general(a.astype(BF16), b.astype(BF16), (((0,), (0,)), ((), ())),
                           preferred_element_type=F32)


_dot_solve = _dot


def _dot_ones(x, ones_bf, terms=2):
    out = None
    for _ in range(terms):
        part = x.astype(BF16)
        x = x - part.astype(F32)
        d = jnp.dot(part, ones_bf, preferred_element_type=F32)
        out = d if out is None else out + d
    return out


def _sigmoid(x):
    return 1.0 / (1.0 + jnp.exp(-x))


def _silu(x):
    return x * _sigmoid(x)


def _softplus(x):
    return jnp.maximum(x, 0.0) + jnp.log1p(jnp.exp(-jnp.abs(x)))


def _gelu(x):
    c = math.sqrt(2.0 / math.pi)
    return 0.5 * x * (1.0 + jnp.tanh(c * (x + 0.044715 * (x * x * x))))


def _rms(x, g):
    ms = jnp.mean(x * x, axis=-1, keepdims=True)
    return x * lax.rsqrt(ms + RMS_EPS) * g


def _per_batch(fn, y, *ms):
    nb = ms[0].shape[0]
    rows, w = y.shape
    return fn(y.reshape(rows // nb, nb, w), *[m[None] for m in ms]).reshape(rows, w)


def _modulate(y, mod):
    return _per_batch(lambda y3, sc, sh: y3 * (1.0 + sc) + sh, y, mod[:, D:2 * D], mod[:, :D])


def _gated_residual(x, out, mod):
    return x + _per_batch(lambda o3, gt: o3 * gt, out, mod[:, 2 * D:])


def _full(shape):
    n = len(shape)
    return pl.BlockSpec(shape, lambda *_: (0,) * n)


def _resident(shape):
    n = len(shape)
    return pl.BlockSpec(shape, lambda *_: (0,) * n, pipeline_mode=pl.Buffered(1))


def _rows(tm, width):
    return pl.BlockSpec((tm, width), lambda i: (i, 0))


def _pick_spec(sel, single_buffer=False):
    stack, i = sel
    tail = (0,) * (stack.ndim - 1)
    kw = dict(pipeline_mode=pl.Buffered(1)) if single_buffer else {}
    return pl.BlockSpec((None,) + stack.shape[1:], lambda *_: (i,) + tail, **kw)


def _mod_spec(mod):
    return _pick_spec(mod)


def _ada_kernel(n_first, c_ref, w_ref, b_ref, o1_ref, o2_ref):
    m = _dot(_silu(c_ref[...]), w_ref[0]) + b_ref[0]
    o1_ref[0] = m[:n_first]
    o2_ref[0] = m[n_first:]


def _ada_all(c_all, n_first, ada_w, ada_b):
    n_sub = ada_w.shape[0] * ada_w.shape[1]
    w = ada_w.reshape(n_sub, D, 3 * D)
    b = ada_b.reshape(n_sub, 1, 3 * D)
    rows = c_all.shape[0]
    tn = 768
    out = lambda r: (pl.BlockSpec((1, r, tn), lambda s, j: (s, 0, j)),
                     jax.ShapeDtypeStruct((n_sub, r, 3 * D), F32))
    (spec1, shape1), (spec2, shape2) = out(n_first), out(rows - n_first)
    return pl.pallas_call(
        functools.partial(_ada_kernel, n_first),
        grid=(n_sub, 3 * D // tn),
        in_specs=[pl.BlockSpec((rows, D), lambda s, j: (0, 0)),
                  pl.BlockSpec((1, D, tn), lambda s, j: (s, 0, j)),
                  pl.BlockSpec((1, 1, tn), lambda s, j: (s, 0, j))],
        out_specs=(spec1, spec2),
        out_shape=(shape1, shape2),
        compiler_params=_cparams(("parallel", "parallel")),
        name="ada_ln",
    )(c_all, w, b)


def _rwkv_rows(pr, prev, vf, p, has_vlora):
    ps = pr + (prev - pr) * p["mu"]
    r = ps[:, :RW]
    k = ps[:, RW:2 * RW]
    v = ps[:, 2 * RW:OFF_W]
    wd = ps[:, OFF_W:OFF_A]
    ad = ps[:, OFF_A:OFF_G]
    gd = ps[:, OFF_G:RCOLS]
    lw = -math.exp(-0.5) * _sigmoid(p["w0"] + _dot(jnp.tanh(wd), p["w2"]))
    if has_vlora:
        v = v + (vf - v) * _sigmoid(p["v0"] + _dot(_dot(v, p["v1"]), p["v2"]))
    a = _sigmoid(p["a0"] + _dot(ad, p["a2"]))
    g = _dot(_sigmoid(gd), p["g2"])
    kk = k * p["k_k"]
    ss = _dot(kk * kk, p["hsum"])
    kk = kk * lax.rsqrt(jnp.maximum(ss, 1e-24))
    k = k * (1.0 + (a - 1.0) * p["k_a"])
    return r, lw, k, v, -kk, kk * a, g


def _rwkv_post(y, r, k, v, g, p):
    hs = p["hsum"]
    mean = _dot(y, hs) * (1.0 / HEAD_DIM)
    d = y - mean
    var = _dot(d * d, hs) * (1.0 / HEAD_DIM)
    yn = d * lax.rsqrt(var + LNX_EPS) * p["lnx_w"] + p["lnx_b"]
    bonus = _dot(r * k * p["r_k"], hs) * v
    return (yn + bonus) * g


def _lru_rows(xc, p):
    gate_r = _sigmoid(_dot(xc, p["wa"]) + p["ba"])
    gate_i = _sigmoid(_dot(xc, p["wx"]) + p["bx"])
    log_a = -LRU_C * gate_r * _softplus(-p["lam"])
    a_t = jnp.exp(log_a)
    b_t = jnp.sqrt(1.0 - jnp.exp(2.0 * log_a)) * gate_i * xc
    return a_t, b_t


_EVEN_PARAM_NAMES = ("mu", "w0", "w2", "a0", "a2", "g2", "k_k", "k_a", "r_k", "lnx_w", "lnx_b",
                     "v0", "v1", "v2", "conv_w", "conv_b", "wa", "ba", "wx", "bx", "lam", "hsum")


def _load_params(refs):
    return {n: r[...] for n, r in zip(_EVEN_PARAM_NAMES, refs)}


def _even_params(P):
    n_even = P["rwkv_mu"].shape[0]
    eye8 = jnp.eye(HEADS, dtype=F32)

    def bdiag(w):
        return jnp.einsum("lncd,nm->lncmd", w, eye8).reshape(w.shape[0], LRU, LRU).astype(BF16)

    rows = lambda v: v.reshape(v.shape[0], 1, -1)
    stacks = {
        "mu": rows(P["rwkv_mu"]), "w0": rows(P["rwkv_w0"]), "w2": P["rwkv_w2"],
        "a0": rows(P["rwkv_a0"]), "a2": P["rwkv_a2"], "g2": P["rwkv_g2"],
        "k_k": rows(P["rwkv_k_k"]), "k_a": rows(P["rwkv_k_a"]), "r_k": rows(P["rwkv_r_k"]),
        "lnx_w": rows(P["rwkv_lnx_w"]), "lnx_b": rows(P["rwkv_lnx_b"]),
        "v0": rows(P["rwkv_v0"]), "v1": P["rwkv_v1"], "v2": P["rwkv_v2"],
        "conv_w": P["lru_conv_w"], "conv_b": rows(P["lru_conv_b"]),
        "wa": bdiag(P["lru_wa"]), "ba": rows(P["lru_ba"]),
        "wx": bdiag(P["lru_wx"]), "bx": rows(P["lru_bx"]),
        "lam": rows(P["lru_lam"]),
        "hsum": jnp.kron(eye8, jnp.ones((HEAD_DIM, HEAD_DIM), F32)).astype(BF16)[None],
    }
    vlora = ("v0", "v1", "v2")
    shared = ("hsum",)
    out = []
    for e in range(n_even):
        idx = lambda n: 0 if n in shared else (max(e - 1, 0) if n in vlora else e)
        out.append(({n: (s, idx(n)) for n, s in stacks.items()}, e > 0))
    return out


def _wkv_consts(L):
    L2 = 2 * L
    ri = lax.broadcasted_iota(jnp.int32, (L2, L2), 0)
    ci = lax.broadcasted_iota(jnp.int32, (L2, L2), 1)
    same = (ri >= L) == (ci >= L)
    lane = lax.broadcasted_iota(jnp.int32, (1, 2 * HEAD_DIM), 1)
    return {
        "smask": same & (ri > ci),
        "imask": same & (ri >= ci),
        "eye": (ri == ci).astype(F32),
        "m0": lane < HEAD_DIM,
    }


def _wkv_chunks(chains, cst):
    L = chains[0][0].shape[0]
    L2 = 2 * L
    m0 = cst["m0"]
    each = lambda fn, *cols: [fn(*xs) for xs in zip(*cols)]

    def stk(x):
        return jnp.concatenate([jnp.where(m0, x, 0.0), jnp.where(m0, 0.0, x)], axis=0)

    rt, at, bt, kt, b2, k2, v, s, egl = [list(c) for c in zip(*chains)]
    a4 = each(lambda a, r: jnp.concatenate([stk(a), stk(r)], axis=0), at, rt)
    b4 = each(lambda b, k: jnp.concatenate([stk(b), stk(k)], axis=0), bt, kt)
    pm = each(_dot_nt, a4, b4)
    ah = each(_dot_nt, a4, s)
    n_ab = [jnp.where(cst["smask"], x[:L2, :L2], 0.0) for x in pm]
    a_ak = [jnp.where(cst["smask"], x[:L2, L2:], 0.0) for x in pm]
    a_r = [jnp.concatenate([jnp.where(cst["imask"], x[L2:, :L2], 0.0),
                            jnp.where(cst["imask"], x[L2:, L2:], 0.0)], axis=1) for x in pm]
    v2 = [stk(x) for x in v]
    rhs = each(lambda h, m, w: h[:L2] + _dot(m, w), ah, a_ak, v2)
    t_inv = [cst["eye"] + n for n in n_ab]
    pk = n_ab
    for _ in range(int(math.log2(L)) - 1):
        pk = each(_dot_solve, pk, pk)
        t_inv = each(lambda t, q: t + _dot_solve(t, q), t_inv, pk)
    u2 = each(_dot_solve, t_inv, rhs)
    uv = each(lambda u, w: jnp.concatenate([u, w], axis=0), u2, v2)
    y2 = each(lambda h, m, w: h[L2:] + _dot(m, w), ah, a_r, uv)
    ys = [x[:L] + x[L:] for x in y2]
    bk = each(lambda b, k: jnp.concatenate([stk(b), stk(k)], axis=0), b2, k2)
    s_new = each(lambda s0, e, w, q: s0 * e + _dot_tn(w, q), s, egl, uv, bk)
    return list(zip(ys, s_new))


def _shift_time(x, fill, nb, d):
    return jnp.concatenate([fill, x[:x.shape[0] - d * nb]], axis=0)


def _time_cumsum(x, nb):
    d = nb
    while d < x.shape[0]:
        x = x + _shift_time(x, jnp.zeros((d, x.shape[1]), F32), 1, d)
        d *= 2
    return x


def _time_linear_scan(a, b, nb):
    d = nb
    while d < a.shape[0]:
        a_sh = _shift_time(a, jnp.ones((d, a.shape[1]), F32), 1, d)
        b_sh = _shift_time(b, jnp.zeros((d, b.shape[1]), F32), 1, d)
        b = a * b_sh + b
        a = a * a_sh
        d *= 2
    return a, b


def _to_time_major(x_ref, scr, nb, steps):
    planes = x_ref.shape[2] // 128
    for c in range(planes):
        for bi in range(nb):
            scr[c, pl.ds(bi, steps, stride=nb), :] = x_ref[bi, :, c * 128:(c + 1) * 128]
    return jnp.concatenate([scr[c] for c in range(planes)], axis=1)


def _from_time_major(x, o_ref, scr, nb, steps):
    planes = x.shape[1] // 128
    for c in range(planes):
        scr[c] = x[:, c * 128:(c + 1) * 128]
    for c in range(planes):
        for bi in range(nb):
            o_ref[bi, :, c * 128:(c + 1) * 128] = scr[c, pl.ds(bi, steps, stride=nb), :]


def _even_layer_kernel(has_vlora, x_bt, nb, *refs):
    n_par = len(_EVEN_PARAM_NAMES)
    x_ref, mod_ref, ng_ref, win_ref, vf_ref = refs[:5]
    p_refs = refs[5:5 + n_par]
    wout_ref = refs[5 + n_par]
    (xo_ref, vfo_ref, wkv_ref, shift_ref, lru_ref, conv_ref) = refs[6 + n_par:12 + n_par]
    (s_scr, prev_scr, hist_scr, h_scr, egl_scr,
     rt_s, at_s, bt_s, kt_s, b2_s, k2_s, v_s, y_s) = refs[12 + n_par:25 + n_par]
    ti = pl.program_id(0)
    L = WKV_CHUNK
    R = L * nb

    @pl.when(ti == 0)
    def _():
        s_scr[...] = jnp.zeros_like(s_scr)
        prev_scr[...] = jnp.zeros_like(prev_scr)
        hist_scr[...] = jnp.zeros_like(hist_scr)
        h_scr[...] = jnp.zeros_like(h_scr)

    p = _load_params(p_refs)
    x = _to_time_major(x_ref, refs[25 + n_par], nb, L) if x_bt else x_ref[...]
    mod = mod_ref[...]
    proj = _dot(_modulate(_rms(x, ng_ref[...]), mod), win_ref[...])
    pr = proj[:, :RCOLS]
    prev = _shift_time(pr, prev_scr[...], nb, 1)
    prev_scr[...] = pr[R - nb:]
    r, lw, k, v, a, b, g = _rwkv_rows(pr, prev, vf_ref[...], p, has_vlora)
    vfo_ref[...] = jnp.zeros_like(vfo_ref) if has_vlora else v

    gc = _time_cumsum(lw, nb)
    g3 = gc.reshape(L, nb, RW)
    gl = g3[L - 1:L]
    egl = jnp.broadcast_to(jnp.exp(gl), (8, nb, RW)).reshape(8 * nb, RW)
    to_l = jnp.exp(gl - g3).reshape(R, RW)
    ieg = jnp.exp(-gc)
    ops = (r * jnp.exp(gc), a * jnp.exp(gc - lw), b * ieg, k * ieg, b * to_l, k * to_l, v)
    op_refs = (rt_s, at_s, bt_s, kt_s, b2_s, k2_s, v_s)
    for pi in range(PAIRS):
        for ref, val in zip(op_refs, ops):
            ref[pi] = val[:, pi * 128:(pi + 1) * 128]
        egl_scr[pi] = egl[:, pi * 128:(pi + 1) * 128]
    cst = _wkv_consts(L)

    def per_batch(it, carry):
        ids = [(it * WKV_BATCH + j, pi) for j in range(WKV_BATCH) for pi in range(PAIRS)]
        rows = lambda bi: pl.ds(bi, L, stride=nb)
        chains = [[ref[pi, rows(bi), :] for ref in op_refs]
                  + [s_scr[bi * PAIRS + pi], egl_scr[pi, pl.ds(bi, 8, stride=nb), :][0:1]]
                  for bi, pi in ids]
        for (bi, pi), (y, s_new) in zip(ids, _wkv_chunks(chains, cst)):
            y_s[pi, rows(bi), :] = y
            s_scr[bi * PAIRS + pi] = s_new
        return carry

    lax.fori_loop(0, nb // WKV_BATCH, per_batch, 0)
    y_all = jnp.concatenate([y_s[pi] for pi in range(PAIRS)], axis=1)
    y_rwkv = _rwkv_post(y_all, r, k, v, g, p)

    lx = proj[:, RCOLS:RCOLS + LRU]
    lg = proj[:, RCOLS + LRU:]
    xpad = jnp.concatenate([hist_scr[...], lx], axis=0)
    cw = p["conv_w"]
    xc = p["conv_b"] + xpad[0:R] * cw[0:1]
    for j in range(1, CONV_W):
        xc = xc + xpad[j * nb:j * nb + R] * cw[j:j + 1]
    hist_scr[...] = xpad[R:]
    a_t, b_t = _lru_rows(xc, p)
    a_cum, b_cum = _time_linear_scan(a_t, b_t, nb)
    hs = _per_batch(lambda a3, h0: a3 * h0, a_cum, h_scr[...]) + b_cum
    h_scr[...] = hs[R - nb:]
    y_lru = hs * _gelu(lg)

    out = _dot(y_rwkv, wout_ref[0:RW, :]) + _dot(y_lru, wout_ref[RW:, :])
    xo_ref[...] = _gated_residual(x, out, mod)

    @pl.when(ti == pl.num_programs(0) - 1)
    def _():
        wkv_ref[...] = s_scr[...]
        shift_ref[...] = prev_scr[...]
        lru_ref[...] = h_scr[...]
        conv_ref[...] = hist_scr[...]


def _even_layer_prompt(x, mod, norm_g, w_in_bf, vf, ep, has_vlora, w_out_bf, nb, t):
    L = WKV_CHUNK
    R = L * nb
    x_bt = x.ndim == 3
    x_spec = pl.BlockSpec((nb, L, D), lambda i: (0, i, 0)) if x_bt else _rows(R, D)
    pvals = [ep[n] for n in _EVEN_PARAM_NAMES]
    out_shapes = [
        jax.ShapeDtypeStruct((t * nb, D), F32),
        jax.ShapeDtypeStruct((t * nb, RW) if not has_vlora else (8, 128), F32),
        jax.ShapeDtypeStruct((nb * PAIRS, 128, 128), F32),
        jax.ShapeDtypeStruct((nb, RCOLS), F32),
        jax.ShapeDtypeStruct((nb, LRU), F32),
        jax.ShapeDtypeStruct(((CONV_W - 1) * nb, LRU), F32),
    ]
    out_specs = [_rows(R, D), _rows(R, RW) if not has_vlora else _full((8, 128)),
                 _full((nb * PAIRS, 128, 128)), _full((nb, RCOLS)), _full((nb, LRU)),
                 _full(((CONV_W - 1) * nb, LRU))]
    scratch = [pltpu.VMEM((nb * PAIRS, 128, 128), F32), pltpu.VMEM((nb, RCOLS), F32),
               pltpu.VMEM(((CONV_W - 1) * nb, LRU), F32), pltpu.VMEM((nb, LRU), F32),
               pltpu.VMEM((PAIRS, 8 * nb, 128), F32)]
    scratch += [pltpu.VMEM((PAIRS, R, 128), F32) for _ in range(8)]
    scratch += [pltpu.VMEM((D // 128, R, 128), F32)] if x_bt else []
    vf_spec = _rows(R, RW) if has_vlora else _full(vf.shape)
    return pl.pallas_call(
        functools.partial(_even_layer_kernel, has_vlora, x_bt, nb),
        grid=(t // L,),
        in_specs=[x_spec, _mod_spec(mod), _full((1, D)), _pick_spec(w_in_bf, True), vf_spec]
                 + [_pick_spec(v) for v in pvals] + [_pick_spec(w_out_bf, True)],
        out_specs=out_specs,
        out_shape=out_shapes,
        scratch_shapes=scratch,
        compiler_params=_cparams(("arbitrary",)),
        name="even_layer_prompt",
    )(x, mod[0], norm_g, w_in_bf[0], vf, *[v[0] for v in pvals], w_out_bf[0])


def _in_proj_kernel(x_ref, mod_ref, g_ref, w_ref, o_ref):
    o_ref[...] = _dot(_modulate(_rms(x_ref[...], g_ref[...]), mod_ref[...]), w_ref[...])


def _in_proj(x, mod, norm_g, w_in_bf):
    rows = x.shape[0]
    return pl.pallas_call(
        _in_proj_kernel,
        grid=(1,),
        in_specs=[_full((rows, D)), _mod_spec(mod), _full((1, D)), _pick_spec(w_in_bf, True)],
        out_specs=_full((rows, IN_COLS)),
        out_shape=jax.ShapeDtypeStruct((rows, IN_COLS), F32),
        compiler_params=_cparams(("arbitrary",)),
        name="even_in_proj",
    )(x, mod[0], norm_g, w_in_bf[0])


def _even_pre_sample_kernel(has_vlora, *refs):
    n_par = len(_EVEN_PARAM_NAMES)
    proj_ref, shift_ref, c0_ref, c1_ref, c2_ref, h_ref, vf_ref = refs[:7]
    p = _load_params(refs[7:7 + n_par])
    (r_o, k_o, v_o, g_o, ylru_o, h_o, rt_o, wt_o, kt_o, vt_o, at_o, bt_o) = refs[7 + n_par:]
    proj = proj_ref[...]
    pr = proj[:, :RCOLS]
    r, lw, k, v, a, b, g = _rwkv_rows(pr, shift_ref[...], vf_ref[...], p, has_vlora)
    r_o[...] = r
    k_o[...] = k
    v_o[...] = v
    g_o[...] = g
    for ref, val in zip((rt_o, wt_o, kt_o, vt_o, at_o, bt_o), (r, jnp.exp(lw), k, v, a, b)):
        ref[...] = jnp.transpose(val)
    lx = proj[:, RCOLS:RCOLS + LRU]
    lg = proj[:, RCOLS + LRU:]
    cw = p["conv_w"]
    xc = (p["conv_b"] + c0_ref[...] * cw[0:1] + c1_ref[...] * cw[1:2] + c2_ref[...] * cw[2:3]
          + lx * cw[3:4])
    a_t, b_t = _lru_rows(xc, p)
    hs = a_t * h_ref[...] + b_t
    h_o[...] = hs
    ylru_o[...] = hs * _gelu(lg)


def _wkv_step_kernel(s_ref, r_ref, w_ref, k_ref, v_ref, a_ref, b_ref, so_ref, y_ref):
    r, w, k, a, b = r_ref[...], w_ref[...], k_ref[...], a_ref[...], b_ref[...]
    for i in range(HEAD_DIM):
        s_i = s_ref[i]
        u_i = jnp.sum(s_i * a, axis=0, keepdims=True)
        s_i = s_i * w + u_i * b + v_ref[i:i + 1, :] * k
        so_ref[i] = s_i
        y_ref[i:i + 1, :] = jnp.sum(s_i * r, axis=0, keepdims=True)


def _wkv_step(state_t, e, ops_t):
    nb = state_t.shape[-1]
    s_in = pl.BlockSpec((None, None, HEAD_DIM, HEAD_DIM, nb), lambda h: (e, h, 0, 0, 0))
    s_out = pl.BlockSpec((None, HEAD_DIM, HEAD_DIM, nb), lambda h: (h, 0, 0, 0))
    o_spec = pl.BlockSpec((HEAD_DIM, nb), lambda h: (h, 0))
    return pl.pallas_call(
        _wkv_step_kernel,
        grid=(HEADS,),
        in_specs=[s_in] + [o_spec] * 6,
        out_specs=(s_out, o_spec),
        out_shape=(jax.ShapeDtypeStruct(state_t.shape[1:], F32),
                   jax.ShapeDtypeStruct((RW, nb), F32)),
        compiler_params=_cparams(("parallel",)),
        name="wkv_step_sample",
    )(state_t, *ops_t)


def _even_post_sample_kernel(x_ref, mod_ref, y_ref, r_ref, k_ref, v_ref, g_ref, ylru_ref,
                             rk_ref, lw_ref, lb_ref, hsum_ref, wout_ref, xo_ref):
    p = {"r_k": rk_ref[...], "lnx_w": lw_ref[...], "lnx_b": lb_ref[...], "hsum": hsum_ref[...]}
    y = jnp.transpose(y_ref[...])
    y_rwkv = _rwkv_post(y, r_ref[...], k_ref[...], v_ref[...], g_ref[...], p)
    out = _dot(y_rwkv, wout_ref[0:RW, :]) + _dot(ylru_ref[...], wout_ref[RW:, :])
    xo_ref[...] = _gated_residual(x_ref[...], out, mod_ref[...])


def _even_layer_sample(x, mod, norm_g, w_in_bf, vf, s_wkv, s_shift, s_lru, s_conv, ep, has_vlora,
                       w_out_bf):
    nb = x.shape[0]
    proj = _in_proj(x, mod, norm_g, w_in_bf)
    pvals = [ep[n] for n in _EVEN_PARAM_NAMES]
    rw = jax.ShapeDtypeStruct((nb, RW), F32)
    rw_t = jax.ShapeDtypeStruct((RW, nb), F32)
    ins = [proj, s_shift, s_conv[:, 0], s_conv[:, 1], s_conv[:, 2], s_lru, vf]
    r, k, v, g, y_lru, h_new, *ops_t = pl.pallas_call(
        functools.partial(_even_pre_sample_kernel, has_vlora),
        in_specs=[_full(t.shape) for t in ins] + [_pick_spec(t) for t in pvals],
        out_specs=tuple(_full((nb, RW)) for _ in range(6)) + tuple(_full((RW, nb)) for _ in range(6)),
        out_shape=(rw,) * 6 + (rw_t,) * 6,
        grid=(1,),
        compiler_params=_cparams(("arbitrary",)),
        name="even_pre_sample",
    )(*ins, *[t[0] for t in pvals])
    s_new, y = _wkv_step(s_wkv[0], s_wkv[1], ops_t)
    acts = [y, r, k, v, g, y_lru]
    picks = [ep["r_k"], ep["lnx_w"], ep["lnx_b"], ep["hsum"]]
    x_new = pl.pallas_call(
        _even_post_sample_kernel,
        in_specs=[_full(x.shape), _mod_spec(mod)] + [_full(t.shape) for t in acts]
                 + [_pick_spec(t) for t in picks] + [_pick_spec(w_out_bf, True)],
        out_specs=_full((nb, D)),
        out_shape=jax.ShapeDtypeStruct((nb, D), F32),
        grid=(1,),
        compiler_params=_cparams(("arbitrary",)),
        name="even_post_sample",
    )(x, mod[0], *acts, *[t[0] for t in picks], w_out_bf[0])
    lx = proj[:, RCOLS:RCOLS + LRU]
    conv_new = jnp.concatenate([s_conv[:, 1:], lx[:, None]], axis=1)
    return (x_new, v, s_new, proj[:, :RCOLS], h_new,
            conv_new)


def _s5_param_kernel(are_ref, aim_ref, ldt_ref, bre_ref, bim_ref, rep_ref,
                     abr_ref, abi_ref, bbr_ref, bbi_ref):
    a_re = are_ref[...]
    a_im = aim_ref[...]
    dt = jnp.exp(ldt_ref[...])
    mag = jnp.exp(a_re * dt)
    abr = mag * jnp.cos(a_im * dt)
    abi = mag * jnp.sin(a_im * dt)
    den = a_re * a_re + a_im * a_im
    nr = abr - 1.0
    cr = _dot_ones((nr * a_re + abi * a_im) / den, rep_ref[...], terms=3)
    ci = _dot_ones((abi * a_re - nr * a_im) / den, rep_ref[...], terms=3)
    b_re = bre_ref[...]
    b_im = bim_ref[...]
    abr_ref[...] = abr
    abi_ref[...] = abi
    bbr_ref[...] = cr * b_re - ci * b_im
    bbi_ref[...] = cr * b_im + ci * b_re


def _s5_params(a_re, a_im, log_dt, b_re, b_im, c_re, c_im):
    gp = S5_GROUPS * S5_STATE
    pc = S5_STATE * S5_GROUP
    rep = jnp.repeat(jnp.eye(S5_STATE, dtype=BF16), S5_GROUP, axis=1)
    ins = [a_re, a_im, log_dt.reshape(S5_GROUPS, 1), b_re.reshape(S5_GROUPS, pc),
           b_im.reshape(S5_GROUPS, pc), rep]
    gs = (S5_GROUPS, S5_STATE)
    abr, abi, bbr, bbi = pl.pallas_call(
        _s5_param_kernel,
        in_specs=[_full(t.shape) for t in ins],
        out_specs=(_full(gs), _full(gs), _full((S5_GROUPS, pc)), _full((S5_GROUPS, pc))),
        out_shape=(jax.ShapeDtypeStruct(gs, F32),) * 2
                  + (jax.ShapeDtypeStruct((S5_GROUPS, pc), F32),) * 2,
        grid=(1,),
        compiler_params=_cparams(("arbitrary",)),
        name="s5_discretise",
    )(*ins)
    gl = S5_GROUPS // S5_SUPER
    eye = jnp.eye(gl, dtype=F32)

    def b_bd(bb):
        t = bb.reshape(S5_SUPER, gl, S5_STATE, S5_GROUP)
        return jnp.einsum("sgpc,gh->sgchp", t, eye).reshape(S5_SUPER, gl * S5_GROUP, gl * S5_STATE)

    def c_bd(cc):
        t = cc.reshape(S5_SUPER, gl, S5_GROUP, S5_STATE)
        return jnp.einsum("sgcp,gh->sgphc", t, eye).reshape(S5_SUPER, gl * S5_STATE, gl * S5_GROUP)

    b_mat = jnp.concatenate([b_bd(bbr), b_bd(bbi)], axis=2).astype(BF16)
    c_mat = jnp.concatenate([c_bd(c_re), -c_bd(c_im)], axis=1).astype(BF16)
    return abr.reshape(1, gp), abi.reshape(1, gp), b_mat, c_mat


def _s5_kernel(bb, tc, x_ref, mod_ref, g_ref, abr_ref, abi_ref, bmat_ref, cmat_ref, dsk_ref,
               wglu_ref, bglu_ref, s0r_ref, s0i_ref, xo_ref, sr_ref, si_ref,
               xr_scr, xi_scr, str_scr, sti_scr):
    ti = pl.program_id(0)
    sw = S5_LANES // S5_SUPER

    @pl.when(ti == 0)
    def _():
        str_scr[...] = s0r_ref[...]
        sti_scr[...] = s0i_ref[...]

    mod = mod_ref[...]
    cw = S5_GROUPS // S5_SUPER * S5_GROUP
    lw = 1024
    halves = 2 if tc % 2 == 0 else 1
    th = tc // halves

    def project_in(h):
        rows = slice(h * th * bb, (h + 1) * th * bb)
        x = x_ref[rows, :]
        u = _modulate(_rms(x, g_ref[...]), mod)
        for sg in range(S5_SUPER):
            bu = _dot(u[:, sg * cw:(sg + 1) * cw], bmat_ref[sg])
            xr_scr[rows, sg * sw:(sg + 1) * sw] = bu[:, :sw]
            xi_scr[rows, sg * sw:(sg + 1) * sw] = bu[:, sw:]
        return x, u

    def recur(h):
        for rg in range(bb // 8):
            for lc in range(S5_LANES // lw):
                ln = slice(lc * lw, (lc + 1) * lw)
                ar = jnp.broadcast_to(abr_ref[:, ln], (8, lw))
                ai = jnp.broadcast_to(abi_ref[:, ln], (8, lw))
                sr = str_scr[rg * 8:(rg + 1) * 8, ln]
                si = sti_scr[rg * 8:(rg + 1) * 8, ln]
                for t in range(h * th, (h + 1) * th):
                    rr = slice(t * bb + rg * 8, t * bb + rg * 8 + 8)
                    sr, si = (ar * sr - ai * si + xr_scr[rr, ln], ar * si + ai * sr + xi_scr[rr, ln])
                    xr_scr[rr, ln] = sr
                    xi_scr[rr, ln] = si
                str_scr[rg * 8:(rg + 1) * 8, ln] = sr
                sti_scr[rg * 8:(rg + 1) * 8, ln] = si

    def project_out(h, x, u):
        rows = slice(h * th * bb, (h + 1) * th * bb)
        ys = []
        for sg in range(S5_SUPER):
            ys.append(_dot(xr_scr[rows, sg * sw:(sg + 1) * sw], cmat_ref[sg, 0:sw, :])
                      + _dot(xi_scr[rows, sg * sw:(sg + 1) * sw], cmat_ref[sg, sw:2 * sw, :]))
        yy = jnp.concatenate(ys, axis=1) + dsk_ref[...] * u
        gl = _dot(_gelu(yy), wglu_ref[...]) + bglu_ref[...]
        out = gl[:, :D] * _sigmoid(gl[:, D:])
        xo_ref[rows, :] = _gated_residual(x, out, mod)

    ins = [project_in(h) for h in range(halves)]
    for h in range(halves):
        recur(h)
        project_out(h, *ins[h])

    @pl.when(ti == pl.num_programs(0) - 1)
    def _():
        sr_ref[...] = str_scr[...]
        si_ref[...] = sti_scr[...]


def _s5_layer(x, mod, norm_g, sp, d_skip, w_glu_bf, b_glu, s0r, s0i, bb, t):
    abr, abi, b_mat, c_mat = sp
    tc = min(S5_STEPS, t)
    rows = tc * bb
    ins = [x, mod[0], norm_g, abr, abi, b_mat, c_mat, d_skip, w_glu_bf[0], b_glu, s0r, s0i]
    in_specs = [_rows(rows, D), _mod_spec(mod)] + [_full(v.shape) for v in ins[2:]]
    in_specs[5], in_specs[6] = _resident(b_mat.shape), _resident(c_mat.shape)
    in_specs[8] = _pick_spec(w_glu_bf, True)
    st = jax.ShapeDtypeStruct((bb, S5_LANES), F32)
    return pl.pallas_call(
        functools.partial(_s5_kernel, bb, tc),
        grid=(t // tc,),
        in_specs=in_specs,
        out_specs=(_rows(rows, D), _full((bb, S5_LANES)), _full((bb, S5_LANES))),
        out_shape=(jax.ShapeDtypeStruct((t * bb, D), F32), st, st),
        scratch_shapes=[pltpu.VMEM((rows, S5_LANES), F32), pltpu.VMEM((rows, S5_LANES), F32),
                        pltpu.VMEM((bb, S5_LANES), F32), pltpu.VMEM((bb, S5_LANES), F32)],
        compiler_params=_cparams(("arbitrary",)),
        name="s5_layer",
    )(*ins)


def _route(logits_t, rb):
    s = _sigmoid(logits_t)
    sel = s + rb
    rows = [sel[e:e + 1] for e in range(N_EXPERTS)]
    scores = []
    for gi in range(N_EGROUPS):
        m = rows[gi * EGROUP:(gi + 1) * EGROUP]
        best = None
        for i in range(EGROUP):
            for j in range(i + 1, EGROUP):
                pair = m[i] + m[j]
                best = pair if best is None else jnp.maximum(best, pair)
        scores.append(best)
    top = scores[0]
    grp = jnp.zeros_like(top, dtype=jnp.int32)
    for gi in range(1, N_EGROUPS):
        better = scores[gi] > top
        grp = jnp.where(better, gi, grp)
        top = jnp.where(better, scores[gi], top)
    picked = []
    for e in range(N_EXPERTS):
        gi = e // EGROUP
        rank = jnp.zeros_like(grp)
        for m in range(gi * EGROUP, (gi + 1) * EGROUP):
            if m == e:
                continue
            ahead = (rows[m] > rows[e]) if m > e else (rows[m] >= rows[e])
            rank = rank + ahead.astype(jnp.int32)
        chosen = (grp == gi) & (rank < 2)
        picked.append(jnp.where(chosen, s[e:e + 1], 0.0))
    w = jnp.concatenate(picked, axis=0)
    return w / jnp.sum(w, axis=0, keepdims=True)


def _experts(hb, gates, wg_ref, wu_ref, wd_ref):
    acc = jnp.zeros((hb.shape[0], D), F32)
    for e in range(N_EXPERTS):
        hg = jnp.dot(hb, wg_ref[e], preferred_element_type=F32)
        hu = jnp.dot(hb, wu_ref[e], preferred_element_type=F32)
        act = _silu(hg) * hu * gates[:, e:e + 1]
        acc = acc + jnp.dot(act.astype(BF16), wd_ref[e], preferred_element_type=F32)
    return acc


def _moe_kernel(final, out_bt, x_ref, mod_ref, g_ref, rw_ref, rb_ref, wg_ref, wu_ref, wd_ref, fg_ref,
                o_ref, *scratch):
    x = x_ref[...]
    mod = mod_ref[...]
    h = _modulate(_rms(x, g_ref[...]), mod)
    rw = rw_ref[...]
    rw_hi = rw.astype(BF16)
    rw_lo = (rw - rw_hi.astype(F32)).astype(BF16)
    h_hi = h.astype(BF16)
    h_lo = (h - h_hi.astype(F32)).astype(BF16)
    logits_t = _dot_nt(rw_hi, h_hi) + (_dot_nt(rw_hi, h_lo) + _dot_nt(rw_lo, h_hi))
    gates_t = _route(logits_t[:N_EXPERTS], rb_ref[...])
    pad = jnp.zeros((ROUTER_ROWS - N_EXPERTS, gates_t.shape[1]), F32)
    gates = jnp.transpose(jnp.concatenate([gates_t, pad], axis=0))
    acc = _experts(h_hi, gates, wg_ref, wu_ref, wd_ref)
    xn = _gated_residual(x, acc, mod)
    if final:
        xn = _rms(xn, fg_ref[...])
    if out_bt:
        nb = mod.shape[0]
        _from_time_major(xn, o_ref, scratch[0], nb, x.shape[0] // nb)
    else:
        o_ref[...] = xn


def _moe_layer(x, mod, norm_g, router_wt, router_b, wg, wu, wd, final_g, final, out_bt):
    rows = x.shape[0]
    nb = mod[0].shape[1]
    tm = min(MOE_ROWS, rows)
    if out_bt:
        out_spec = pl.BlockSpec((nb, tm // nb, D), lambda i: (0, i, 0))
        out_shape = jax.ShapeDtypeStruct((nb, rows // nb, D), F32)
        scratch = [pltpu.VMEM((D // 128, tm, 128), F32)]
    else:
        out_spec, out_shape, scratch = _rows(tm, D), jax.ShapeDtypeStruct((rows, D), F32), []
    return pl.pallas_call(
        functools.partial(_moe_kernel, final, out_bt),
        grid=(rows // tm,),
        in_specs=[_rows(tm, D), _mod_spec(mod),
                  _full((1, D)), _full((ROUTER_ROWS, D)), _full((N_EXPERTS, 1)),
                  _pick_spec(wg, True), _pick_spec(wu, True), _pick_spec(wd, True), _full((1, D))],
        out_specs=out_spec,
        out_shape=out_shape,
        scratch_shapes=scratch,
        compiler_params=_cparams(("parallel",)),
        name="moe",
    )(x, mod[0], norm_g, router_wt, router_b, wg[0], wu[0], wd[0], final_g)


def _trunk(x3, mods, states, P, W):
    B, T, _ = x3.shape
    fresh = states is None
    assert fresh or T == 1, "a group that carries state is stepped one token at a time"
    depth = P["norm_g"].shape[0]
    x = x3 if (fresh and T > 1) else x3.reshape(B, D)
    outs = {k: [] for k in ("wkv", "shift", "lru", "conv", "s5r", "s5i")}
    v_first = jnp.zeros((8, 128), F32)
    wkv_t = None if fresh else jnp.transpose(states["wkv"], (0, 2, 3, 4, 1))
    for layer in range(depth):
        e = layer // 2
        ng = P["norm_g"][layer]
        mod_a, mod_m = (mods, 2 * layer), (mods, 2 * layer + 1)
        if layer % 2 == 0:
            ep, has_vlora = W["even"][e]
            if fresh:
                x, vf_new, wkv, sh, lr, cv = _even_layer_prompt(
                    x, mod_a, ng[0:1], (W["w_in"], e), v_first, ep, has_vlora, (W["w_out"], e), B, T)
                idx = jnp.arange(HEADS)
                wkv = wkv.reshape(B, PAIRS, 2, HEAD_DIM, 2, HEAD_DIM)
                wkv = wkv[:, idx // 2, idx % 2, :, idx % 2, :]
                wkv = jnp.moveaxis(wkv, 0, 1)
                cv = jnp.swapaxes(cv.reshape(CONV_W - 1, B, LRU), 0, 1)
            else:
                x, vf_new, wkv, sh, lr, cv = _even_layer_sample(
                    x, mod_a, ng[0:1], (W["w_in"], e), v_first, (wkv_t, e), states["shift"][e],
                    states["lru"][e], states["conv"][e], ep, has_vlora, (W["w_out"], e))
            if not has_vlora:
                v_first = vf_new
            outs["wkv"].append(wkv); outs["shift"].append(sh); outs["lru"].append(lr)
            outs["conv"].append(cv)
        else:
            if fresh:
                s0r = jnp.zeros((B, S5_LANES), F32)
                s0i = s0r
            else:
                s0r = states["s5r"][e].reshape(B, S5_LANES)
                s0i = states["s5i"][e].reshape(B, S5_LANES)
            x, sr, si = _s5_layer(x, mod_a, ng[0:1], W["s5"][e], P["s5_d"][e].reshape(1, D),
                                  (W["w_glu"], e), P["s5_b_glu"][e].reshape(1, 2 * D), s0r, s0i, B, T)
            outs["s5r"].append(sr.reshape(B, S5_GROUPS, S5_STATE))
            outs["s5i"].append(si.reshape(B, S5_GROUPS, S5_STATE))
        last = layer == depth - 1
        x = _moe_layer(x, mod_m, ng[1:2], W["router_wt"], W["router_b"], (W["wg"], layer),
                       (W["wu"], layer), (W["wd"], layer), P["final_norm_g"].reshape(1, D), last,
                       last and x3.shape[1] > 1)
    y = x.reshape(B, T, D)
    wkv = jnp.stack(outs["wkv"])
    if not fresh:
        wkv = jnp.transpose(wkv, (0, 4, 1, 2, 3))
    return (y, wkv, jnp.stack(outs["shift"]), jnp.stack(outs["lru"]),
            jnp.stack(outs["conv"]), jnp.stack(outs["s5r"]), jnp.stack(outs["s5i"]))


def kernel(x_prompt, x_sample, state_wkv, state_shift, state_lru, state_conv, state_s5_re, state_s5_im, c_prompt, c_sample, norm_g, ada_w, ada_b, final_norm_g, even_w_in, rwkv_mu, rwkv_w0, rwkv_w2, rwkv_a0, rwkv_a2, rwkv_g2, rwkv_k_k, rwkv_k_a, rwkv_r_k, rwkv_lnx_w, rwkv_lnx_b, rwkv_v0, rwkv_v1, rwkv_v2, lru_conv_w, lru_conv_b, lru_wa, lru_ba, lru_wx, lru_bx, lru_lam, even_w_out, s5_a_re, s5_a_im, s5_log_dt, s5_b_re, s5_b_im, s5_c_re, s5_c_im, s5_d, s5_w_glu, s5_b_glu, router_w, router_b, moe_w_gate, moe_w_up, moe_w_down):
    P = dict(norm_g=norm_g, final_norm_g=final_norm_g, rwkv_mu=rwkv_mu, rwkv_w0=rwkv_w0,
             rwkv_w2=rwkv_w2, rwkv_a0=rwkv_a0, rwkv_a2=rwkv_a2, rwkv_g2=rwkv_g2, rwkv_k_k=rwkv_k_k,
             rwkv_k_a=rwkv_k_a, rwkv_r_k=rwkv_r_k.reshape(rwkv_r_k.shape[0], RW),
             rwkv_lnx_w=rwkv_lnx_w, rwkv_lnx_b=rwkv_lnx_b, rwkv_v0=rwkv_v0, rwkv_v1=rwkv_v1,
             rwkv_v2=rwkv_v2, lru_conv_w=lru_conv_w, lru_conv_b=lru_conv_b, lru_wa=lru_wa,
             lru_ba=lru_ba.reshape(lru_ba.shape[0], LRU), lru_wx=lru_wx,
             lru_bx=lru_bx.reshape(lru_bx.shape[0], LRU), lru_lam=lru_lam.reshape(lru_lam.shape[0], LRU),
             s5_d=s5_d, s5_b_glu=s5_b_glu)
    n_even, n_odd = even_w_in.shape[0], s5_w_glu.shape[0]
    W = dict(
        even=_even_params(P),
        w_in=even_w_in.astype(BF16), w_out=even_w_out.astype(BF16), w_glu=s5_w_glu.astype(BF16),
        s5=[_s5_params(s5_a_re[e], s5_a_im[e], s5_log_dt[e], s5_b_re[e], s5_b_im[e], s5_c_re[e],
                       s5_c_im[e]) for e in range(n_odd)],
        router_wt=jnp.pad(router_w.T, ((0, ROUTER_ROWS - N_EXPERTS), (0, 0))),
        router_b=router_b.reshape(N_EXPERTS, 1),
        wg=moe_w_gate.astype(BF16), wu=moe_w_up.astype(BF16), wd=moe_w_down.astype(BF16),
    )
    bp = x_prompt.shape[0]
    mods_p, mods_s = _ada_all(jnp.concatenate([c_prompt, c_sample], axis=0), bp, ada_w, ada_b)
    out_p = _trunk(x_prompt, mods_p, None, P, W)
    st = dict(wkv=state_wkv, shift=state_shift, lru=state_lru, conv=state_conv,
              s5r=state_s5_re, s5i=state_s5_im)
    out_s = _trunk(x_sample, mods_s, st, P, W)
    return (out_p[0], out_s[0]) + tuple(out_p[1:]) + tuple(out_s[1:])
```

```python
import functools
import math

import jax
import jax.numpy as jnp
from jax import lax
from jax.experimental import pallas as pl
from jax.experimental.pallas import tpu as pltpu

F32 = jnp.float32
BF16 = jnp.bfloat16

D = 1024
HEADS = 8
HEAD_DIM = 64
PAIRS = HEADS // 2
RW = HEADS * HEAD_DIM
OFF_W = 3 * RW
OFF_A = OFF_W + 64
OFF_G = OFF_A + 64
RCOLS = OFF_G + 128
LRU = 512
CONV_W = 4
IN_COLS = RCOLS + 2 * LRU
S5_GROUP = 16
S5_GROUPS = 64
S5_STATE = 64
S5_LANES = S5_GROUPS * S5_STATE
S5_SUPER = 4
N_EXPERTS = 16
N_EGROUPS = 4
EGROUP = 4
D_EXPERT = 256
ROUTER_ROWS = 128
RMS_EPS = 1e-6
LNX_EPS = 64e-5
LRU_C = 8.0

WKV_CHUNK = 64
WKV_BATCH = 8
S5_STEPS = 64
MOE_ROWS = 512
VMEM_LIMIT = 56 * 1024 * 1024


def _cparams(sem):
    return pltpu.CompilerParams(dimension_semantics=sem, vmem_limit_bytes=VMEM_LIMIT)


def _dot(a, b):
    return jnp.dot(a.astype(BF16), b.astype(BF16), preferred_element_type=F32)


def _dot_hi(a, b):
    return jnp.dot(a, b, precision=lax.Precision.HIGHEST, preferred_element_type=F32)


def _dot_nt(a, b):
    return lax.dot_general(a.astype(BF16), b.astype(BF16), (((1,), (1,)), ((), ())),
                           preferred_element_type=F32)


def _dot_tn(a, b):
    return lax.dot_general(a.astype(BF16), b.astype(BF16), (((0,), (0,)), ((), ())),
                           preferred_element_type=F32)


_dot_solve = _dot


def _dot_ones(x, ones_bf, terms=2):
    out = None
    for _ in range(terms):
        part = x.astype(BF16)
        x = x - part.astype(F32)
        d = jnp.dot(part, ones_bf, preferred_element_type=F32)
        out = d if out is None else out + d
    return out


def _sigmoid(x):
    return 1.0 / (1.0 + jnp.exp(-x))


def _silu(x):
    return x * _sigmoid(x)


def _softplus(x):
    return jnp.maximum(x, 0.0) + jnp.log1p(jnp.exp(-jnp.abs(x)))


def _gelu(x):
    c = math.sqrt(2.0 / math.pi)
    return 0.5 * x * (1.0 + jnp.tanh(c * (x + 0.044715 * (x * x * x))))


def _rms(x, g):
    ms = jnp.mean(x * x, axis=-1, keepdims=True)
    return x * lax.rsqrt(ms + RMS_EPS) * g


def _per_batch(fn, y, *ms):
    nb = ms[0].shape[0]
    rows, w = y.shape
    return fn(y.reshape(rows // nb, nb, w), *[m[None] for m in ms]).reshape(rows, w)


def _modulate(y, mod):
    return _per_batch(lambda y3, sc, sh: y3 * (1.0 + sc) + sh, y, mod[:, D:2 * D], mod[:, :D])


def _gated_residual(x, out, mod):
    return x + _per_batch(lambda o3, gt: o3 * gt, out, mod[:, 2 * D:])


def _full(shape):
    n = len(shape)
    return pl.BlockSpec(shape, lambda *_: (0,) * n)


def _resident(shape):
    n = len(shape)
    return pl.BlockSpec(shape, lambda *_: (0,) * n, pipeline_mode=pl.Buffered(1))


def _rows(tm, width):
    return pl.BlockSpec((tm, width), lambda i: (i, 0))


def _pick_spec(sel, single_buffer=False):
    stack, i = sel
    tail = (0,) * (stack.ndim - 1)
    kw = dict(pipeline_mode=pl.Buffered(1)) if single_buffer else {}
    return pl.BlockSpec((None,) + stack.shape[1:], lambda *_: (i,) + tail, **kw)


def _mod_spec(mod):
    return _pick_spec(mod)


def _ada_kernel(n_first, c_ref, w_ref, b_ref, o1_ref, o2_ref):
    m = _dot(_silu(c_ref[...]), w_ref[0]) + b_ref[0]
    o1_ref[0] = m[:n_first]
    o2_ref[0] = m[n_first:]


def _ada_all(c_all, n_first, ada_w, ada_b):
    n_sub = ada_w.shape[0] * ada_w.shape[1]
    w = ada_w.reshape(n_sub, D, 3 * D)
    b = ada_b.reshape(n_sub, 1, 3 * D)
    rows = c_all.shape[0]
    tn = 1536
    out = lambda r: (pl.BlockSpec((1, r, tn), lambda s, j: (s, 0, j)),
                     jax.ShapeDtypeStruct((n_sub, r, 3 * D), F32))
    (spec1, shape1), (spec2, shape2) = out(n_first), out(rows - n_first)
    return pl.pallas_call(
        functools.partial(_ada_kernel, n_first),
        grid=(n_sub, 3 * D // tn),
        in_specs=[pl.BlockSpec((rows, D), lambda s, j: (0, 0)),
                  pl.BlockSpec((1, D, tn), lambda s, j: (s, 0, j)),
                  pl.BlockSpec((1, 1, tn), lambda s, j: (s, 0, j))],
        out_specs=(spec1, spec2),
        out_shape=(shape1, shape2),
        compiler_params=_cparams(("parallel", "parallel")),
        name="ada_ln",
    )(c_all, w, b)


def _rwkv_rows(pr, prev, vf, p, has_vlora):
    ps = pr + (prev - pr) * p["mu"]
    r = ps[:, :RW]
    k = ps[:, RW:2 * RW]
    v = ps[:, 2 * RW:OFF_W]
    wd = ps[:, OFF_W:OFF_A]
    ad = ps[:, OFF_A:OFF_G]
    gd = ps[:, OFF_G:RCOLS]
    lw = -math.exp(-0.5) * _sigmoid(p["w0"] + _dot(jnp.tanh(wd), p["w2"]))
    if has_vlora:
        v = v + (vf - v) * _sigmoid(p["v0"] + _dot(_dot(v, p["v1"]), p["v2"]))
    a = _sigmoid(p["a0"] + _dot(ad, p["a2"]))
    g = _dot(_sigmoid(gd), p["g2"])
    kk = k * p["k_k"]
    ss = _dot(kk * kk, p["hsum"])
    kk = kk * lax.rsqrt(jnp.maximum(ss, 1e-24))
    k = k * (1.0 + (a - 1.0) * p["k_a"])
    return r, lw, k, v, -kk, kk * a, g


def _rwkv_post(y, r, k, v, g, p):
    hs = p["hsum"]
    mean = _dot(y, hs) * (1.0 / HEAD_DIM)
    d = y - mean
    var = _dot(d * d, hs) * (1.0 / HEAD_DIM)
    yn = d * lax.rsqrt(var + LNX_EPS) * p["lnx_w"] + p["lnx_b"]
    bonus = _dot(r * k * p["r_k"], hs) * v
    return (yn + bonus) * g


def _lru_rows(xc, p):
    gate_r = _sigmoid(_dot(xc, p["wa"]) + p["ba"])
    gate_i = _sigmoid(_dot(xc, p["wx"]) + p["bx"])
    log_a = -LRU_C * gate_r * _softplus(-p["lam"])
    a_t = jnp.exp(log_a)
    b_t = jnp.sqrt(1.0 - jnp.exp(2.0 * log_a)) * gate_i * xc
    return a_t, b_t


_EVEN_PARAM_NAMES = ("mu", "w0", "w2", "a0", "a2", "g2", "k_k", "k_a", "r_k", "lnx_w", "lnx_b",
                     "v0", "v1", "v2", "conv_w", "conv_b", "wa", "ba", "wx", "bx", "lam", "hsum")


def _load_params(refs):
    return {n: r[...] for n, r in zip(_EVEN_PARAM_NAMES, refs)}


def _even_params(P):
    n_even = P["rwkv_mu"].shape[0]
    eye8 = jnp.eye(HEADS, dtype=F32)

    def bdiag(w):
        return jnp.einsum("lncd,nm->lncmd", w, eye8).reshape(w.shape[0], LRU, LRU).astype(BF16)

    rows = lambda v: v.reshape(v.shape[0], 1, -1)
    stacks = {
        "mu": rows(P["rwkv_mu"]), "w0": rows(P["rwkv_w0"]), "w2": P["rwkv_w2"],
        "a0": rows(P["rwkv_a0"]), "a2": P["rwkv_a2"], "g2": P["rwkv_g2"],
        "k_k": rows(P["rwkv_k_k"]), "k_a": rows(P["rwkv_k_a"]), "r_k": rows(P["rwkv_r_k"]),
        "lnx_w": rows(P["rwkv_lnx_w"]), "lnx_b": rows(P["rwkv_lnx_b"]),
        "v0": rows(P["rwkv_v0"]), "v1": P["rwkv_v1"], "v2": P["rwkv_v2"],
        "conv_w": P["lru_conv_w"], "conv_b": rows(P["lru_conv_b"]),
        "wa": bdiag(P["lru_wa"]), "ba": rows(P["lru_ba"]),
        "wx": bdiag(P["lru_wx"]), "bx": rows(P["lru_bx"]),
        "lam": rows(P["lru_lam"]),
        "hsum": jnp.kron(eye8, jnp.ones((HEAD_DIM, HEAD_DIM), F32)).astype(BF16)[None],
    }
    vlora = ("v0", "v1", "v2")
    shared = ("hsum",)
    out = []
    for e in range(n_even):
        idx = lambda n: 0 if n in shared else (max(e - 1, 0) if n in vlora else e)
        out.append(({n: (s, idx(n)) for n, s in stacks.items()}, e > 0))
    return out


def _wkv_consts(L):
    L2 = 2 * L
    ri = lax.broadcasted_iota(jnp.int32, (L2, L2), 0)
    ci = lax.broadcasted_iota(jnp.int32, (L2, L2), 1)
    same = (ri >= L) == (ci >= L)
    lane = lax.broadcasted_iota(jnp.int32, (1, 2 * HEAD_DIM), 1)
    return {
        "smask": same & (ri > ci),
        "imask": same & (ri >= ci),
        "eye": (ri == ci).astype(F32),
        "m0": lane < HEAD_DIM,
    }


def _wkv_chunks(chains, cst):
    L = chains[0][0].shape[0]
    L2 = 2 * L
    m0 = cst["m0"]
    each = lambda fn, *cols: [fn(*xs) for xs in zip(*cols)]

    def stk(x):
        return jnp.concatenate([jnp.where(m0, x, 0.0), jnp.where(m0, 0.0, x)], axis=0)

    rt, at, bt, kt, b2, k2, v, s, egl = [list(c) for c in zip(*chains)]
    a4 = each(lambda a, r: jnp.concatenate([stk(a), stk(r)], axis=0), at, rt)
    b4 = each(lambda b, k: jnp.concatenate([stk(b), stk(k)], axis=0), bt, kt)
    pm = each(_dot_nt, a4, b4)
    ah = each(_dot_nt, a4, s)
    n_ab = [jnp.where(cst["smask"], x[:L2, :L2], 0.0) for x in pm]
    a_ak = [jnp.where(cst["smask"], x[:L2, L2:], 0.0) for x in pm]
    a_r = [jnp.concatenate([jnp.where(cst["imask"], x[L2:, :L2], 0.0),
                            jnp.where(cst["imask"], x[L2:, L2:], 0.0)], axis=1) for x in pm]
    v2 = [stk(x) for x in v]
    rhs = each(lambda h, m, w: h[:L2] + _dot(m, w), ah, a_ak, v2)
    t_inv = [cst["eye"] + n for n in n_ab]
    pk = n_ab
    for _ in range(int(math.log2(L)) - 1):
        pk = each(_dot_solve, pk, pk)
        t_inv = each(lambda t, q: t + _dot_solve(t, q), t_inv, pk)
    u2 = each(_dot_solve, t_inv, rhs)
    uv = each(lambda u, w: jnp.concatenate([u, w], axis=0), u2, v2)
    y2 = each(lambda h, m, w: h[L2:] + _dot(m, w), ah, a_r, uv)
    ys = [x[:L] + x[L:] for x in y2]
    bk = each(lambda b, k: jnp.concatenate([stk(b), stk(k)], axis=0), b2, k2)
    s_new = each(lambda s0, e, w, q: s0 * e + _dot_tn(w, q), s, egl, uv, bk)
    return list(zip(ys, s_new))


def _shift_time(x, fill, nb, d):
    return jnp.concatenate([fill, x[:x.shape[0] - d * nb]], axis=0)


def _time_cumsum(x, nb):
    d = nb
    while d < x.shape[0]:
        x = x + _shift_time(x, jnp.zeros((d, x.shape[1]), F32), 1, d)
        d *= 2
    return x


def _time_linear_scan(a, b, nb):
    d = nb
    while d < a.shape[0]:
        a_sh = _shift_time(a, jnp.ones((d, a.shape[1]), F32), 1, d)
        b_sh = _shift_time(b, jnp.zeros((d, b.shape[1]), F32), 1, d)
        b = a * b_sh + b
        a = a * a_sh
        d *= 2
    return a, b


def _to_time_major(x_ref, scr, nb, steps):
    planes = x_ref.shape[2] // 128
    for c in range(planes):
        for bi in range(nb):
            scr[c, pl.ds(bi, steps, stride=nb), :] = x_ref[bi, :, c * 128:(c + 1) * 128]
    return jnp.concatenate([scr[c] for c in range(planes)], axis=1)


def _from_time_major(x, o_ref, scr, nb, steps):
    planes = x.shape[1] // 128
    for c in range(planes):
        scr[c] = x[:, c * 128:(c + 1) * 128]
    for c in range(planes):
        for bi in range(nb):
            o_ref[bi, :, c * 128:(c + 1) * 128] = scr[c, pl.ds(bi, steps, stride=nb), :]


def _even_layer_kernel(has_vlora, x_bt, nb, *refs):
    n_par = len(_EVEN_PARAM_NAMES)
    x_ref, mod_ref, ng_ref, win_ref, vf_ref = refs[:5]
    p_refs = refs[5:5 + n_par]
    wout_ref = refs[5 + n_par]
    (xo_ref, vfo_ref, wkv_ref, shift_ref, lru_ref, conv_ref) = refs[6 + n_par:12 + n_par]
    (s_scr, prev_scr, hist_scr, h_scr, egl_scr,
     rt_s, at_s, bt_s, kt_s, b2_s, k2_s, v_s, y_s) = refs[12 + n_par:25 + n_par]
    ti = pl.program_id(0)
    L = WKV_CHUNK
    R = L * nb

    @pl.when(ti == 0)
    def _():
        s_scr[...] = jnp.zeros_like(s_scr)
        prev_scr[...] = jnp.zeros_like(prev_scr)
        hist_scr[...] = jnp.zeros_like(hist_scr)
        h_scr[...] = jnp.zeros_like(h_scr)

    p = _load_params(p_refs)
    x = _to_time_major(x_ref, refs[25 + n_par], nb, L) if x_bt else x_ref[...]
    mod = mod_ref[...]
    proj = _dot(_modulate(_rms(x, ng_ref[...]), mod), win_ref[...])
    pr = proj[:, :RCOLS]
    prev = _shift_time(pr, prev_scr[...], nb, 1)
    prev_scr[...] = pr[R - nb:]
    r, lw, k, v, a, b, g = _rwkv_rows(pr, prev, vf_ref[...], p, has_vlora)
    vfo_ref[...] = jnp.zeros_like(vfo_ref) if has_vlora else v

    gc = _time_cumsum(lw, nb)
    g3 = gc.reshape(L, nb, RW)
    gl = g3[L - 1:L]
    egl = jnp.broadcast_to(jnp.exp(gl), (8, nb, RW)).reshape(8 * nb, RW)
    to_l = jnp.exp(gl - g3).reshape(R, RW)
    ieg = jnp.exp(-gc)
    ops = (r * jnp.exp(gc), a * jnp.exp(gc - lw), b * ieg, k * ieg, b * to_l, k * to_l, v)
    op_refs = (rt_s, at_s, bt_s, kt_s, b2_s, k2_s, v_s)
    for pi in range(PAIRS):
        for ref, val in zip(op_refs, ops):
            ref[pi] = val[:, pi * 128:(pi + 1) * 128]
        egl_scr[pi] = egl[:, pi * 128:(pi + 1) * 128]
    cst = _wkv_consts(L)

    def per_batch(it, carry):
        ids = [(it * WKV_BATCH + j, pi) for j in range(WKV_BATCH) for pi in range(PAIRS)]
        rows = lambda bi: pl.ds(bi, L, stride=nb)
        chains = [[ref[pi, rows(bi), :] for ref in op_refs]
                  + [s_scr[bi * PAIRS + pi], egl_scr[pi, pl.ds(bi, 8, stride=nb), :][0:1]]
                  for bi, pi in ids]
        for (bi, pi), (y, s_new) in zip(ids, _wkv_chunks(chains, cst)):
            y_s[pi, rows(bi), :] = y
            s_scr[bi * PAIRS + pi] = s_new
        return carry

    lax.fori_loop(0, nb // WKV_BATCH, per_batch, 0)
    y_all = jnp.concatenate([y_s[pi] for pi in range(PAIRS)], axis=1)
    y_rwkv = _rwkv_post(y_all, r, k, v, g, p)

    lx = proj[:, RCOLS:RCOLS + LRU]
    lg = proj[:, RCOLS + LRU:]
    xpad = jnp.concatenate([hist_scr[...], lx], axis=0)
    cw = p["conv_w"]
    xc = p["conv_b"] + xpad[0:R] * cw[0:1]
    for j in range(1, CONV_W):
        xc = xc + xpad[j * nb:j * nb + R] * cw[j:j + 1]
    hist_scr[...] = xpad[R:]
    a_t, b_t = _lru_rows(xc, p)
    a_cum, b_cum = _time_linear_scan(a_t, b_t, nb)
    hs = _per_batch(lambda a3, h0: a3 * h0, a_cum, h_scr[...]) + b_cum
    h_scr[...] = hs[R - nb:]
    y_lru = hs * _gelu(lg)

    out = _dot(y_rwkv, wout_ref[0:RW, :]) + _dot(y_lru, wout_ref[RW:, :])
    xo_ref[...] = _gated_residual(x, out, mod)

    @pl.when(ti == pl.num_programs(0) - 1)
    def _():
        wkv_ref[...] = s_scr[...]
        shift_ref[...] = prev_scr[...]
        lru_ref[...] = h_scr[...]
        conv_ref[...] = hist_scr[...]


def _even_layer_prompt(x, mod, norm_g, w_in_bf, vf, ep, has_vlora, w_out_bf, nb, t):
    L = WKV_CHUNK
    R = L * nb
    x_bt = x.ndim == 3
    x_spec = pl.BlockSpec((nb, L, D), lambda i: (0, i, 0)) if x_bt else _rows(R, D)
    pvals = [ep[n] for n in _EVEN_PARAM_NAMES]
    out_shapes = [
        jax.ShapeDtypeStruct((t * nb, D), F32),
        jax.ShapeDtypeStruct((t * nb, RW) if not has_vlora else (8, 128), F32),
        jax.ShapeDtypeStruct((nb * PAIRS, 128, 128), F32),
        jax.ShapeDtypeStruct((nb, RCOLS), F32),
        jax.ShapeDtypeStruct((nb, LRU), F32),
        jax.ShapeDtypeStruct(((CONV_W - 1) * nb, LRU), F32),
    ]
    out_specs = [_rows(R, D), _rows(R, RW) if not has_vlora else _full((8, 128)),
                 _full((nb * PAIRS, 128, 128)), _full((nb, RCOLS)), _full((nb, LRU)),
                 _full(((CONV_W - 1) * nb, LRU))]
    scratch = [pltpu.VMEM((nb * PAIRS, 128, 128), F32), pltpu.VMEM((nb, RCOLS), F32),
               pltpu.VMEM(((CONV_W - 1) * nb, LRU), F32), pltpu.VMEM((nb, LRU), F32),
               pltpu.VMEM((PAIRS, 8 * nb, 128), F32)]
    scratch += [pltpu.VMEM((PAIRS, R, 128), F32) for _ in range(8)]
    scratch += [pltpu.VMEM((D // 128, R, 128), F32)] if x_bt else []
    vf_spec = _rows(R, RW) if has_vlora else _full(vf.shape)
    return pl.pallas_call(
        functools.partial(_even_layer_kernel, has_vlora, x_bt, nb),
        grid=(t // L,),
        in_specs=[x_spec, _mod_spec(mod), _full((1, D)), _pick_spec(w_in_bf, True), vf_spec]
                 + [_pick_spec(v) for v in pvals] + [_pick_spec(w_out_bf, True)],
        out_specs=out_specs,
        out_shape=out_shapes,
        scratch_shapes=scratch,
        compiler_params=_cparams(("arbitrary",)),
        name="even_layer_prompt",
    )(x, mod[0], norm_g, w_in_bf[0], vf, *[v[0] for v in pvals], w_out_bf[0])


def _in_proj_kernel(x_ref, mod_ref, g_ref, w_ref, o_ref):
    o_ref[...] = _dot(_modulate(_rms(x_ref[...], g_ref[...]), mod_ref[...]), w_ref[...])


def _in_proj(x, mod, norm_g, w_in_bf):
    rows = x.shape[0]
    return pl.pallas_call(
        _in_proj_kernel,
        grid=(1,),
        in_specs=[_full((rows, D)), _mod_spec(mod), _full((1, D)), _pick_spec(w_in_bf, True)],
        out_specs=_full((rows, IN_COLS)),
        out_shape=jax.ShapeDtypeStruct((rows, IN_COLS), F32),
        compiler_params=_cparams(("arbitrary",)),
        name="even_in_proj",
    )(x, mod[0], norm_g, w_in_bf[0])


def _even_pre_sample_kernel(has_vlora, *refs):
    n_par = len(_EVEN_PARAM_NAMES)
    proj_ref, shift_ref, c0_ref, c1_ref, c2_ref, h_ref, vf_ref = refs[:7]
    p = _load_params(refs[7:7 + n_par])
    (r_o, k_o, v_o, g_o, ylru_o, h_o, rt_o, wt_o, kt_o, vt_o, at_o, bt_o) = refs[7 + n_par:]
    proj = proj_ref[...]
    pr = proj[:, :RCOLS]
    r, lw, k, v, a, b, g = _rwkv_rows(pr, shift_ref[...], vf_ref[...], p, has_vlora)
    r_o[...] = r
    k_o[...] = k
    v_o[...] = v
    g_o[...] = g
    for ref, val in zip((rt_o, wt_o, kt_o, vt_o, at_o, bt_o), (r, jnp.exp(lw), k, v, a, b)):
        ref[...] = jnp.transpose(val)
    lx = proj[:, RCOLS:RCOLS + LRU]
    lg = proj[:, RCOLS + LRU:]
    cw = p["conv_w"]
    xc = (p["conv_b"] + c0_ref[...] * cw[0:1] + c1_ref[...] * cw[1:2] + c2_ref[...] * cw[2:3]
          + lx * cw[3:4])
    a_t, b_t = _lru_rows(xc, p)
    hs = a_t * h_ref[...] + b_t
    h_o[...] = hs
    ylru_o[...] = hs * _gelu(lg)


def _wkv_step_kernel(s_ref, r_ref, w_ref, k_ref, v_ref, a_ref, b_ref, so_ref, y_ref):
    r, w, k, a, b = r_ref[...], w_ref[...], k_ref[...], a_ref[...], b_ref[...]
    for i in range(HEAD_DIM):
        s_i = s_ref[i]
        u_i = jnp.sum(s_i * a, axis=0, keepdims=True)
        s_i = s_i * w + u_i * b + v_ref[i:i + 1, :] * k
        so_ref[i] = s_i
        y_ref[i:i + 1, :] = jnp.sum(s_i * r, axis=0, keepdims=True)


def _wkv_step(state_t, e, ops_t):
    nb = state_t.shape[-1]
    s_in = pl.BlockSpec((None, None, HEAD_DIM, HEAD_DIM, nb), lambda h: (e, h, 0, 0, 0))
    s_out = pl.BlockSpec((None, HEAD_DIM, HEAD_DIM, nb), lambda h: (h, 0, 0, 0))
    o_spec = pl.BlockSpec((HEAD_DIM, nb), lambda h: (h, 0))
    return pl.pallas_call(
        _wkv_step_kernel,
        grid=(HEADS,),
        in_specs=[s_in] + [o_spec] * 6,
        out_specs=(s_out, o_spec),
        out_shape=(jax.ShapeDtypeStruct(state_t.shape[1:], F32),
                   jax.ShapeDtypeStruct((RW, nb), F32)),
        compiler_params=_cparams(("parallel",)),
        name="wkv_step_sample",
    )(state_t, *ops_t)


def _even_post_sample_kernel(x_ref, mod_ref, y_ref, r_ref, k_ref, v_ref, g_ref, ylru_ref,
                             rk_ref, lw_ref, lb_ref, hsum_ref, wout_ref, xo_ref):
    p = {"r_k": rk_ref[...], "lnx_w": lw_ref[...], "lnx_b": lb_ref[...], "hsum": hsum_ref[...]}
    y = jnp.transpose(y_ref[...])
    y_rwkv = _rwkv_post(y, r_ref[...], k_ref[...], v_ref[...], g_ref[...], p)
    out = _dot(y_rwkv, wout_ref[0:RW, :]) + _dot(ylru_ref[...], wout_ref[RW:, :])
    xo_ref[...] = _gated_residual(x_ref[...], out, mod_ref[...])


def _even_layer_sample(x, mod, norm_g, w_in_bf, vf, s_wkv, s_shift, s_lru, s_conv, ep, has_vlora,
                       w_out_bf):
    nb = x.shape[0]
    proj = _in_proj(x, mod, norm_g, w_in_bf)
    pvals = [ep[n] for n in _EVEN_PARAM_NAMES]
    rw = jax.ShapeDtypeStruct((nb, RW), F32)
    rw_t = jax.ShapeDtypeStruct((RW, nb), F32)
    ins = [proj, s_shift, s_conv[:, 0], s_conv[:, 1], s_conv[:, 2], s_lru, vf]
    r, k, v, g, y_lru, h_new, *ops_t = pl.pallas_call(
        functools.partial(_even_pre_sample_kernel, has_vlora),
        in_specs=[_full(t.shape) for t in ins] + [_pick_spec(t) for t in pvals],
        out_specs=tuple(_full((nb, RW)) for _ in range(6)) + tuple(_full((RW, nb)) for _ in range(6)),
        out_shape=(rw,) * 6 + (rw_t,) * 6,
        grid=(1,),
        compiler_params=_cparams(("arbitrary",)),
        name="even_pre_sample",
    )(*ins, *[t[0] for t in pvals])
    s_new, y = _wkv_step(s_wkv[0], s_wkv[1], ops_t)
    acts = [y, r, k, v, g, y_lru]
    picks = [ep["r_k"], ep["lnx_w"], ep["lnx_b"], ep["hsum"]]
    x_new = pl.pallas_call(
        _even_post_sample_kernel,
        in_specs=[_full(x.shape), _mod_spec(mod)] + [_full(t.shape) for t in acts]
                 + [_pick_spec(t) for t in picks] + [_pick_spec(w_out_bf, True)],
        out_specs=_full((nb, D)),
        out_shape=jax.ShapeDtypeStruct((nb, D), F32),
        grid=(1,),
        compiler_params=_cparams(("arbitrary",)),
        name="even_post_sample",
    )(x, mod[0], *acts, *[t[0] for t in picks], w_out_bf[0])
    lx = proj[:, RCOLS:RCOLS + LRU]
    conv_new = jnp.concatenate([s_conv[:, 1:], lx[:, None]], axis=1)
    return (x_new, v, s_new, proj[:, :RCOLS], h_new,
            conv_new)


def _s5_param_kernel(are_ref, aim_ref, ldt_ref, bre_ref, bim_ref, rep_ref,
                     abr_ref, abi_ref, bbr_ref, bbi_ref):
    a_re = are_ref[...]
    a_im = aim_ref[...]
    dt = jnp.exp(ldt_ref[...])
    mag = jnp.exp(a_re * dt)
    abr = mag * jnp.cos(a_im * dt)
    abi = mag * jnp.sin(a_im * dt)
    den = a_re * a_re + a_im * a_im
    nr = abr - 1.0
    cr = _dot_ones((nr * a_re + abi * a_im) / den, rep_ref[...], terms=3)
    ci = _dot_ones((abi * a_re - nr * a_im) / den, rep_ref[...], terms=3)
    b_re = bre_ref[...]
    b_im = bim_ref[...]
    abr_ref[...] = abr
    abi_ref[...] = abi
    bbr_ref[...] = cr * b_re - ci * b_im
    bbi_ref[...] = cr * b_im + ci * b_re


def _s5_params(a_re, a_im, log_dt, b_re, b_im, c_re, c_im):
    gp = S5_GROUPS * S5_STATE
    pc = S5_STATE * S5_GROUP
    rep = jnp.repeat(jnp.eye(S5_STATE, dtype=BF16), S5_GROUP, axis=1)
    ins = [a_re, a_im, log_dt.reshape(S5_GROUPS, 1), b_re.reshape(S5_GROUPS, pc),
           b_im.reshape(S5_GROUPS, pc), rep]
    gs = (S5_GROUPS, S5_STATE)
    abr, abi, bbr, bbi = pl.pallas_call(
        _s5_param_kernel,
        in_specs=[_full(t.shape) for t in ins],
        out_specs=(_full(gs), _full(gs), _full((S5_GROUPS, pc)), _full((S5_GROUPS, pc))),
        out_shape=(jax.ShapeDtypeStruct(gs, F32),) * 2
                  + (jax.ShapeDtypeStruct((S5_GROUPS, pc), F32),) * 2,
        grid=(1,),
        compiler_params=_cparams(("arbitrary",)),
        name="s5_discretise",
    )(*ins)
    gl = S5_GROUPS // S5_SUPER
    eye = jnp.eye(gl, dtype=F32)

    def b_bd(bb):
        t = bb.reshape(S5_SUPER, gl, S5_STATE, S5_GROUP)
        return jnp.einsum("sgpc,gh->sgchp", t, eye).reshape(S5_SUPER, gl * S5_GROUP, gl * S5_STATE)

    def c_bd(cc):
        t = cc.reshape(S5_SUPER, gl, S5_GROUP, S5_STATE)
        return jnp.einsum("sgcp,gh->sgphc", t, eye).reshape(S5_SUPER, gl * S5_STATE, gl * S5_GROUP)

    b_mat = jnp.concatenate([b_bd(bbr), b_bd(bbi)], axis=2).astype(BF16)
    c_mat = jnp.concatenate([c_bd(c_re), -c_bd(c_im)], axis=1).astype(BF16)
    return abr.reshape(1, gp), abi.reshape(1, gp), b_mat, c_mat


def _s5_kernel(bb, tc, x_ref, mod_ref, g_ref, abr_ref, abi_ref, bmat_ref, cmat_ref, dsk_ref,
               wglu_ref, bglu_ref, s0r_ref, s0i_ref, xo_ref, sr_ref, si_ref,
               xr_scr, xi_scr, str_scr, sti_scr):
    ti = pl.program_id(0)
    sw = S5_LANES // S5_SUPER

    @pl.when(ti == 0)
    def _():
        str_scr[...] = s0r_ref[...]
        sti_scr[...] = s0i_ref[...]

    mod = mod_ref[...]
    cw = S5_GROUPS // S5_SUPER * S5_GROUP
    lw = 1024
    halves = 2 if tc % 2 == 0 else 1
    th = tc // halves

    def project_in(h):
        rows = slice(h * th * bb, (h + 1) * th * bb)
        x = x_ref[rows, :]
        u = _modulate(_rms(x, g_ref[...]), mod)
        for sg in range(S5_SUPER):
            bu = _dot(u[:, sg * cw:(sg + 1) * cw], bmat_ref[sg])
            xr_scr[rows, sg * sw:(sg + 1) * sw] = bu[:, :sw]
            xi_scr[rows, sg * sw:(sg + 1) * sw] = bu[:, sw:]
        return x, u

    def recur(h):
        for rg in range(bb // 8):
            for lc in range(S5_LANES // lw):
                ln = slice(lc * lw, (lc + 1) * lw)
                ar = jnp.broadcast_to(abr_ref[:, ln], (8, lw))
                ai = jnp.broadcast_to(abi_ref[:, ln], (8, lw))
                sr = str_scr[rg * 8:(rg + 1) * 8, ln]
                si = sti_scr[rg * 8:(rg + 1) * 8, ln]
                for t in range(h * th, (h + 1) * th):
                    rr = slice(t * bb + rg * 8, t * bb + rg * 8 + 8)
                    sr, si = (ar * sr - ai * si + xr_scr[rr, ln], ar * si + ai * sr + xi_scr[rr, ln])
                    xr_scr[rr, ln] = sr
                    xi_scr[rr, ln] = si
                str_scr[rg * 8:(rg + 1) * 8, ln] = sr
                sti_scr[rg * 8:(rg + 1) * 8, ln] = si

    def project_out(h, x, u):
        rows = slice(h * th * bb, (h + 1) * th * bb)
        ys = []
        for sg in range(S5_SUPER):
            ys.append(_dot(xr_scr[rows, sg * sw:(sg + 1) * sw], cmat_ref[sg, 0:sw, :])
                      + _dot(xi_scr[rows, sg * sw:(sg + 1) * sw], cmat_ref[sg, sw:2 * sw, :]))
        yy = jnp.concatenate(ys, axis=1) + dsk_ref[...] * u
        gl = _dot(_gelu(yy), wglu_ref[...]) + bglu_ref[...]
        out = gl[:, :D] * _sigmoid(gl[:, D:])
        xo_ref[rows, :] = _gated_residual(x, out, mod)

    ins = [project_in(h) for h in range(halves)]
    for h in range(halves):
        recur(h)
        project_out(h, *ins[h])

    @pl.when(ti == pl.num_programs(0) - 1)
    def _():
        sr_ref[...] = str_scr[...]
        si_ref[...] = sti_scr[...]


def _s5_layer(x, mod, norm_g, sp, d_skip, w_glu_bf, b_glu, s0r, s0i, bb, t):
    abr, abi, b_mat, c_mat = sp
    tc = min(S5_STEPS, t)
    rows = tc * bb
    ins = [x, mod[0], norm_g, abr, abi, b_mat, c_mat, d_skip, w_glu_bf[0], b_glu, s0r, s0i]
    in_specs = [_rows(rows, D), _mod_spec(mod)] + [_full(v.shape) for v in ins[2:]]
    in_specs[5], in_specs[6] = _resident(b_mat.shape), _resident(c_mat.shape)
    in_specs[8] = _pick_spec(w_glu_bf, True)
    st = jax.ShapeDtypeStruct((bb, S5_LANES), F32)
    return pl.pallas_call(
        functools.partial(_s5_kernel, bb, tc),
        grid=(t // tc,),
        in_specs=in_specs,
        out_specs=(_rows(rows, D), _full((bb, S5_LANES)), _full((bb, S5_LANES))),
        out_shape=(jax.ShapeDtypeStruct((t * bb, D), F32), st, st),
        scratch_shapes=[pltpu.VMEM((rows, S5_LANES), F32), pltpu.VMEM((rows, S5_LANES), F32),
                        pltpu.VMEM((bb, S5_LANES), F32), pltpu.VMEM((bb, S5_LANES), F32)],
        compiler_params=_cparams(("arbitrary",)),
        name="s5_layer",
    )(*ins)


def _route(logits_t, rb):
    s = _sigmoid(logits_t)
    sel = s + rb
    rows = [sel[e:e + 1] for e in range(N_EXPERTS)]
    scores = []
    for gi in range(N_EGROUPS):
        m = rows[gi * EGROUP:(gi + 1) * EGROUP]
        best = None
        for i in range(EGROUP):
            for j in range(i + 1, EGROUP):
                pair = m[i] + m[j]
                best = pair if best is None else jnp.maximum(best, pair)
        scores.append(best)
    top = scores[0]
    grp = jnp.zeros_like(top, dtype=jnp.int32)
    for gi in range(1, N_EGROUPS):
        better = scores[gi] > top
        grp = jnp.where(better, gi, grp)
        top = jnp.where(better, scores[gi], top)
    picked = []
    for e in range(N_EXPERTS):
        gi = e // EGROUP
        rank = jnp.zeros_like(grp)
        for m in range(gi * EGROUP, (gi + 1) * EGROUP):
            if m == e:
                continue
            ahead = (rows[m] > rows[e]) if m > e else (rows[m] >= rows[e])
            rank = rank + ahead.astype(jnp.int32)
        chosen = (grp == gi) & (rank < 2)
        picked.append(jnp.where(chosen, s[e:e + 1], 0.0))
    w = jnp.concatenate(picked, axis=0)
    return w / jnp.sum(w, axis=0, keepdims=True)


def _experts(hb, gates, wg_ref, wu_ref, wd_ref):
    acc = jnp.zeros((hb.shape[0], D), F32)
    for e in range(N_EXPERTS):
        hg = jnp.dot(hb, wg_ref[e], preferred_element_type=F32)
        hu = jnp.dot(hb, wu_ref[e], preferred_element_type=F32)
        act = _silu(hg) * hu * gates[:, e:e + 1]
        acc = acc + jnp.dot(act.astype(BF16), wd_ref[e], preferred_element_type=F32)
    return acc


def _moe_inputs(x, mod, g_ref, rw_ref, rb_ref):
    h = _modulate(_rms(x, g_ref[...]), mod)
    rw = rw_ref[...]
    rw_hi = rw.astype(BF16)
    rw_lo = (rw - rw_hi.astype(F32)).astype(BF16)
    h_hi = h.astype(BF16)
    h_lo = (h - h_hi.astype(F32)).astype(BF16)
    logits_t = _dot_nt(rw_hi, h_hi) + (_dot_nt(rw_hi, h_lo) + _dot_nt(rw_lo, h_hi))
    gates_t = _route(logits_t[:N_EXPERTS], rb_ref[...])
    pad = jnp.zeros((ROUTER_ROWS - N_EXPERTS, gates_t.shape[1]), F32)
    return h_hi, jnp.transpose(jnp.concatenate([gates_t, pad], axis=0))


def _moe_stream_kernel(final, x_ref, mod_ref, g_ref, rw_ref, rb_ref, wg_ref, wu_ref, wd_ref, fg_ref,
                       o_ref, h_scr, gate_scr, acc_scr):
    e = pl.program_id(0)

    @pl.when(e == 0)
    def _():
        h_hi, gates = _moe_inputs(x_ref[...], mod_ref[...], g_ref, rw_ref, rb_ref)
        h_scr[...] = h_hi
        gate_scr[...] = gates
        acc_scr[...] = jnp.zeros_like(acc_scr)

    hb = h_scr[...]
    lane = lax.broadcasted_iota(jnp.int32, (1, ROUTER_ROWS), 1)
    gate = jnp.sum(jnp.where(lane == e, gate_scr[...], 0.0), axis=1, keepdims=True)
    hg = jnp.dot(hb, wg_ref[...], preferred_element_type=F32)
    hu = jnp.dot(hb, wu_ref[...], preferred_element_type=F32)
    act = _silu(hg) * hu * gate
    acc_scr[...] += jnp.dot(act.astype(BF16), wd_ref[...], preferred_element_type=F32)

    @pl.when(e == pl.num_programs(0) - 1)
    def _():
        xn = _gated_residual(x_ref[...], acc_scr[...], mod_ref[...])
        if final:
            xn = _rms(xn, fg_ref[...])
        o_ref[...] = xn


def _moe_kernel(final, out_bt, x_ref, mod_ref, g_ref, rw_ref, rb_ref, wg_ref, wu_ref, wd_ref, fg_ref,
                o_ref, *scratch):
    x = x_ref[...]
    mod = mod_ref[...]
    h_hi, gates = _moe_inputs(x, mod, g_ref, rw_ref, rb_ref)
    acc = _experts(h_hi, gates, wg_ref, wu_ref, wd_ref)
    xn = _gated_residual(x, acc, mod)
    if final:
        xn = _rms(xn, fg_ref[...])
    if out_bt:
        nb = mod.shape[0]
        _from_time_major(xn, o_ref, scratch[0], nb, x.shape[0] // nb)
    else:
        o_ref[...] = xn


def _moe_layer(x, mod, norm_g, router_wt, router_b, wg, wu, wd, final_g, final, out_bt):
    rows = x.shape[0]
    nb = mod[0].shape[1]
    tm = min(MOE_ROWS, rows)
    if rows == tm and not out_bt:
        layer = wg[1]
        w_spec = lambda s: pl.BlockSpec((None, None) + s.shape[2:], lambda e: (layer, e, 0, 0))
        return pl.pallas_call(
            functools.partial(_moe_stream_kernel, final),
            grid=(N_EXPERTS,),
            in_specs=[_full((rows, D)), _mod_spec(mod), _full((1, D)), _full((ROUTER_ROWS, D)),
                      _full((N_EXPERTS, 1)), w_spec(wg[0]), w_spec(wu[0]), w_spec(wd[0]), _full((1, D))],
            out_specs=_full((rows, D)),
            out_shape=jax.ShapeDtypeStruct((rows, D), F32),
            scratch_shapes=[pltpu.VMEM((rows, D), BF16), pltpu.VMEM((rows, ROUTER_ROWS), F32),
                            pltpu.VMEM((rows, D), F32)],
            compiler_params=_cparams(("arbitrary",)),
            name="moe_stream",
        )(x, mod[0], norm_g, router_wt, router_b, wg[0], wu[0], wd[0], final_g)
    if out_bt:
        out_spec = pl.BlockSpec((nb, tm // nb, D), lambda i: (0, i, 0))
        out_shape = jax.ShapeDtypeStruct((nb, rows // nb, D), F32)
        scratch = [pltpu.VMEM((D // 128, tm, 128), F32)]
    else:
        out_spec, out_shape, scratch = _rows(tm, D), jax.ShapeDtypeStruct((rows, D), F32), []
    return pl.pallas_call(
        functools.partial(_moe_kernel, final, out_bt),
        grid=(rows // tm,),
        in_specs=[_rows(tm, D), _mod_spec(mod),
                  _full((1, D)), _full((ROUTER_ROWS, D)), _full((N_EXPERTS, 1)),
                  _pick_spec(wg, True), _pick_spec(wu, True), _pick_spec(wd, True), _full((1, D))],
        out_specs=out_spec,
        out_shape=out_shape,
        scratch_shapes=scratch,
        compiler_params=_cparams(("parallel",)),
        name="moe",
    )(x, mod[0], norm_g, router_wt, router_b, wg[0], wu[0], wd[0], final_g)


def _trunk(x3, mods, states, P, W):
    B, T, _ = x3.shape
    fresh = states is None
    assert fresh or T == 1, "a group that carries state is stepped one token at a time"
    depth = P["norm_g"].shape[0]
    x = x3 if (fresh and T > 1) else x3.reshape(B, D)
    outs = {k: [] for k in ("wkv", "shift", "lru", "conv", "s5r", "s5i")}
    v_first = jnp.zeros((8, 128), F32)
    wkv_t = None if fresh else jnp.transpose(states["wkv"], (0, 2, 3, 4, 1))
    for layer in range(depth):
        e = layer // 2
        ng = P["norm_g"][layer]
        mod_a, mod_m = (mods, 2 * layer), (mods, 2 * layer + 1)
        if layer % 2 == 0:
            ep, has_vlora = W["even"][e]
            if fresh:
                x, vf_new, wkv, sh, lr, cv = _even_layer_prompt(
                    x, mod_a, ng[0:1], (W["w_in"], e), v_first, ep, has_vlora, (W["w_out"], e), B, T)
                idx = jnp.arange(HEADS)
                wkv = wkv.reshape(B, PAIRS, 2, HEAD_DIM, 2, HEAD_DIM)
                wkv = wkv[:, idx // 2, idx % 2, :, idx % 2, :]
                wkv = jnp.moveaxis(wkv, 0, 1)
                cv = jnp.swapaxes(cv.reshape(CONV_W - 1, B, LRU), 0, 1)
            else:
                x, vf_new, wkv, sh, lr, cv = _even_layer_sample(
                    x, mod_a, ng[0:1], (W["w_in"], e), v_first, (wkv_t, e), states["shift"][e],
                    states["lru"][e], states["conv"][e], ep, has_vlora, (W["w_out"], e))
            if not has_vlora:
                v_first = vf_new
            outs["wkv"].append(wkv); outs["shift"].append(sh); outs["lru"].append(lr)
            outs["conv"].append(cv)
        else:
            if fresh:
                s0r = jnp.zeros((B, S5_LANES), F32)
                s0i = s0r
            else:
                s0r = states["s5r"][e].reshape(B, S5_LANES)
                s0i = states["s5i"][e].reshape(B, S5_LANES)
            x, sr, si = _s5_layer(x, mod_a, ng[0:1], W["s5"][e], P["s5_d"][e].reshape(1, D),
                                  (W["w_glu"], e), P["s5_b_glu"][e].reshape(1, 2 * D), s0r, s0i, B, T)
            outs["s5r"].append(sr.reshape(B, S5_GROUPS, S5_STATE))
            outs["s5i"].append(si.reshape(B, S5_GROUPS, S5_STATE))
        last = layer == depth - 1
        x = _moe_layer(x, mod_m, ng[1:2], W["router_wt"], W["router_b"], (W["wg"], layer),
                       (W["wu"], layer), (W["wd"], layer), P["final_norm_g"].reshape(1, D), last,
                       last and x3.shape[1] > 1)
    y = x.reshape(B, T, D)
    wkv = jnp.stack(outs["wkv"])
    if not fresh:
        wkv = jnp.transpose(wkv, (0, 4, 1, 2, 3))
    return (y, wkv, jnp.stack(outs["shift"]), jnp.stack(outs["lru"]),
            jnp.stack(outs["conv"]), jnp.stack(outs["s5r"]), jnp.stack(outs["s5i"]))


def kernel(x_prompt, x_sample, state_wkv, state_shift, state_lru, state_conv, state_s5_re, state_s5_im, c_prompt, c_sample, norm_g, ada_w, ada_b, final_norm_g, even_w_in, rwkv_mu, rwkv_w0, rwkv_w2, rwkv_a0, rwkv_a2, rwkv_g2, rwkv_k_k, rwkv_k_a, rwkv_r_k, rwkv_lnx_w, rwkv_lnx_b, rwkv_v0, rwkv_v1, rwkv_v2, lru_conv_w, lru_conv_b, lru_wa, lru_ba, lru_wx, lru_bx, lru_lam, even_w_out, s5_a_re, s5_a_im, s5_log_dt, s5_b_re, s5_b_im, s5_c_re, s5_c_im, s5_d, s5_w_glu, s5_b_glu, router_w, router_b, moe_w_gate, moe_w_up, moe_w_down):
    P = dict(norm_g=norm_g, final_norm_g=final_norm_g, rwkv_mu=rwkv_mu, rwkv_w0=rwkv_w0,
             rwkv_w2=rwkv_w2, rwkv_a0=rwkv_a0, rwkv_a2=rwkv_a2, rwkv_g2=rwkv_g2, rwkv_k_k=rwkv_k_k,
             rwkv_k_a=rwkv_k_a, rwkv_r_k=rwkv_r_k.reshape(rwkv_r_k.shape[0], RW),
             rwkv_lnx_w=rwkv_lnx_w, rwkv_lnx_b=rwkv_lnx_b, rwkv_v0=rwkv_v0, rwkv_v1=rwkv_v1,
             rwkv_v2=rwkv_v2, lru_conv_w=lru_conv_w, lru_conv_b=lru_conv_b, lru_wa=lru_wa,
             lru_ba=lru_ba.reshape(lru_ba.shape[0], LRU), lru_wx=lru_wx,
             lru_bx=lru_bx.reshape(lru_bx.shape[0], LRU), lru_lam=lru_lam.reshape(lru_lam.shape[0], LRU),
             s5_d=s5_d, s5_b_glu=s5_b_glu)
    n_even, n_odd = even_w_in.shape[0], s5_w_glu.shape[0]
    W = dict(
        even=_even_params(P),
        w_in=even_w_in.astype(BF16), w_out=even_w_out.astype(BF16), w_glu=s5_w_glu.astype(BF16),
        s5=[_s5_params(s5_a_re[e], s5_a_im[e], s5_log_dt[e], s5_b_re[e], s5_b_im[e], s5_c_re[e],
                       s5_c_im[e]) for e in range(n_odd)],
        router_wt=jnp.pad(router_w.T, ((0, ROUTER_ROWS - N_EXPERTS), (0, 0))),
        router_b=router_b.reshape(N_EXPERTS, 1),
        wg=moe_w_gate.astype(BF16), wu=moe_w_up.astype(BF16), wd=moe_w_down.astype(BF16),
    )
    bp = x_prompt.shape[0]
    mods_p, mods_s = _ada_all(jnp.concatenate([c_prompt, c_sample], axis=0), bp, ada_w, ada_b)
    out_p = _trunk(x_prompt, mods_p, None, P, W)
    st = dict(wkv=state_wkv, shift=state_shift, lru=state_lru, conv=state_conv,
              s5r=state_s5_re, s5i=state_s5_im)
    out_s = _trunk(x_sample, mods_s, st, P, W)
    return (out_p[0], out_s[0]) + tuple(out_p[1:]) + tuple(out_s[1:])
```
